```python
import jax, jax.numpy as jnp
from jax import lax
import numpy as np

D_MODEL = 2048
BATCH = 4
SEQ = 4096
DEPTH = 1
DEC_BATCH = 128
DEC_SEQ = 1
PAST_LEN = 16384
PAGE_SIZE = 128

N_META = 16
MIX_DIM = D_MODEL
CONV_DIM = MIX_DIM // 2
CONV_W = 3
HEAD_DIM = 64
ATTN_DIM = MIX_DIM - CONV_DIM
N_HEADS = ATTN_DIM // HEAD_DIM
N_KV_HEADS = 4
GROUP = N_HEADS // N_KV_HEADS
KV_DIM = N_KV_HEADS * HEAD_DIM
WINDOW = 128
BLOCK = WINDOW
ROPE_THETA = 10000.0
D_FF = ((8 * D_MODEL + 767) // 768) * 256
RMS_EPS = 1e-6
NEG_INF = -1e30
IN_COLS = 3 * CONV_DIM + ATTN_DIM + 2 * KV_DIM
SPLIT_POINTS = (CONV_DIM, 2 * CONV_DIM, 3 * CONV_DIM, 3 * CONV_DIM + ATTN_DIM, 3 * CONV_DIM + ATTN_DIM + KV_DIM)

kernel_name = "hymba_conv_swa_sink_decoder_step"


def rms_norm(x, g):
    x32 = x.astype(jnp.float32)
    y = x32 * lax.rsqrt(jnp.mean(x32 * x32, axis=-1, keepdims=True) + RMS_EPS)
    return (y * g.astype(jnp.float32)).astype(x.dtype)


def rope(x, pos):
    half = HEAD_DIM // 2
    inv = ROPE_THETA ** (-jnp.arange(half, dtype=jnp.float32) / half)
    ang = pos.astype(jnp.float32)[:, None] * inv[None, :]
    cos = jnp.cos(ang)[:, None, :]
    sin = jnp.sin(ang)[:, None, :]
    x32 = x.astype(jnp.float32)
    x1, x2 = x32[..., :half], x32[..., half:]
    return jnp.concatenate([x1 * cos - x2 * sin, x2 * cos + x1 * sin], axis=-1).astype(x.dtype)


def project(hn, w_in):
    n, t, _ = hn.shape
    z = hn @ w_in
    b_gate, c_gate, xc, q, k, v = jnp.split(z, SPLIT_POINTS, axis=-1)
    q = q.reshape(n, t, N_HEADS, HEAD_DIM)
    k = k.reshape(n, t, N_KV_HEADS, HEAD_DIM)
    v = v.reshape(n, t, N_KV_HEADS, HEAD_DIM)
    return b_gate, c_gate, xc, q, k, v


def short_conv(u_ext, w):
    t = u_ext.shape[1] - (CONV_W - 1)
    return sum(w[j] * u_ext[:, j:j + t] for j in range(CONV_W))


def sink_softmax(s, mask, sinks):
    sink = sinks.astype(jnp.float32).reshape(N_KV_HEADS, GROUP, 1, 1)
    s = jnp.where(mask, s, NEG_INF)
    m = jnp.maximum(jnp.max(s, axis=-1, keepdims=True), sink)
    p = jnp.exp(s - m)
    denom = jnp.sum(p, axis=-1, keepdims=True) + jnp.exp(sink - m)
    return p / denom


def attn_prompt(q, k, v, sinks):
    b, l = q.shape[0], q.shape[1]
    pad = BLOCK - N_META
    nb = (l + pad) // BLOCK
    qb = jnp.pad(q, ((0, 0), (pad, 0), (0, 0), (0, 0))).reshape(b, nb, BLOCK, N_KV_HEADS, GROUP, HEAD_DIM)
    kp = jnp.pad(k, ((0, 0), (pad + BLOCK, 0), (0, 0), (0, 0))).reshape(b, nb + 1, BLOCK, N_KV_HEADS, HEAD_DIM)
    vp = jnp.pad(v, ((0, 0), (pad + BLOCK, 0), (0, 0), (0, 0))).reshape(b, nb + 1, BLOCK, N_KV_HEADS, HEAD_DIM)
    kb = jnp.concatenate([kp[:, :-1], kp[:, 1:]], axis=2)
    vb = jnp.concatenate([vp[:, :-1], vp[:, 1:]], axis=2)
    s = jnp.einsum('bnqkgd,bnskd->bnkgqs', qb, kb, preferred_element_type=jnp.float32) * (HEAD_DIM ** -0.5)
    blk = jnp.arange(nb)[:, None, None]
    qi = jnp.arange(BLOCK)[None, :, None]
    kj = jnp.arange(2 * BLOCK)[None, None, :]
    pos_q = blk * BLOCK + qi - pad
    pos_k = blk * BLOCK + kj - BLOCK - pad
    diff = pos_q - pos_k
    mask = ((pos_k >= 0) & (diff >= 0) & (diff <= WINDOW))[:, None, None]
    p = sink_softmax(s, mask, sinks)
    o = jnp.einsum('bnkgqs,bnskd->bnqkgd', p.astype(v.dtype), vb)
    return o.reshape(b, nb * BLOCK, ATTN_DIM)[:, pad:]


def attn_sample(q, k_all, v_all, sinks):
    n, t = q.shape[0], q.shape[1]
    qg = q.reshape(n, t, N_KV_HEADS, GROUP, HEAD_DIM)
    s = jnp.einsum('btkgd,bskd->bkgts', qg, k_all, preferred_element_type=jnp.float32) * (HEAD_DIM ** -0.5)
    pos_q = PAST_LEN + jnp.arange(t)
    pos_k = PAST_LEN - WINDOW + jnp.arange(k_all.shape[1])
    diff = pos_q[:, None] - pos_k[None, :]
    mask = (diff >= 0) & (diff <= WINDOW)
    p = sink_softmax(s, mask, sinks)
    o = jnp.einsum('bkgts,bskd->btkgd', p.astype(v_all.dtype), v_all)
    return o.reshape(n, t, ATTN_DIM)


def merge_groups(b_gate, conv_y, attn_o, conv_g, attn_g, w_out):
    yc = rms_norm(b_gate * conv_y, conv_g)
    ya = rms_norm(attn_o, attn_g)
    return jnp.concatenate([yc, ya], axis=-1) @ w_out


def swiglu(hn, w_gate, w_up, w_down):
    return (jax.nn.silu(hn @ w_gate) * (hn @ w_up)) @ w_down


def setup_inputs(seed: int = 0) -> dict:
    key = jax.random.key(seed)
    ks = jax.random.split(key, 20)
    f32 = jnp.float32
    nrm = lambda k, shape, sc: jax.random.normal(k, shape, f32) * sc
    return {
        "x_prompt": nrm(ks[0], (BATCH, SEQ, D_MODEL), 1.0),
        "x_sample": nrm(ks[1], (DEC_BATCH, DEC_SEQ, D_MODEL), 1.0),
        "cache_k": nrm(ks[2], (DEPTH, DEC_BATCH, WINDOW, N_KV_HEADS, HEAD_DIM), 1.0),
        "cache_v": nrm(ks[3], (DEPTH, DEC_BATCH, WINDOW, N_KV_HEADS, HEAD_DIM), 1.0),
        "state_conv": nrm(ks[4], (DEPTH, DEC_BATCH, CONV_W - 1, CONV_DIM), 1.0),
        "meta_tokens": nrm(ks[5], (N_META, D_MODEL), 1.0),
        "norm1_g": 1.0 + nrm(ks[6], (DEPTH, D_MODEL), 0.02),
        "w_in": nrm(ks[7], (DEPTH, D_MODEL, IN_COLS), D_MODEL ** -0.5),
        "conv_w": nrm(ks[8], (DEPTH, CONV_W, CONV_DIM), CONV_W ** -0.5),
        "conv_norm_g": 1.0 + nrm(ks[9], (DEPTH, CONV_DIM), 0.02),
        "attn_norm_g": 1.0 + nrm(ks[10], (DEPTH, ATTN_DIM), 0.02),
        "attn_sinks": nrm(ks[11], (DEPTH, N_HEADS), 1.0),
        "w_out": nrm(ks[12], (DEPTH, MIX_DIM, D_MODEL), MIX_DIM ** -0.5),
        "norm2_g": 1.0 + nrm(ks[13], (DEPTH, D_MODEL), 0.02),
        "w_gate": nrm(ks[14], (DEPTH, D_MODEL, D_FF), D_MODEL ** -0.5),
        "w_up": nrm(ks[15], (DEPTH, D_MODEL, D_FF), D_MODEL ** -0.5),
        "w_down": nrm(ks[16], (DEPTH, D_FF, D_MODEL), D_FF ** -0.5),
        "final_norm_g": 1.0 + nrm(ks[17], (D_MODEL,), 0.02),
    }


def reference(x_prompt, x_sample, cache_k, cache_v, state_conv, meta_tokens, norm1_g, w_in, conv_w,
              conv_norm_g, attn_norm_g, attn_sinks, w_out, norm2_g, w_gate, w_up, w_down, final_norm_g):
    nbp = x_prompt.shape[0]
    meta = jnp.broadcast_to(meta_tokens[None].astype(x_prompt.dtype), (nbp, N_META, D_MODEL))
    hp = jnp.concatenate([meta, x_prompt], axis=1)
    hs = x_sample
    pos_p = jnp.arange(hp.shape[1])
    pos_s = PAST_LEN + jnp.arange(hs.shape[1])
    nkp, nvp, ncp, nks, nvs, ncs = [], [], [], [], [], []
    for l in range(DEPTH):
        hn = rms_norm(hp, norm1_g[l])
        bg, cg, xc, q, k, v = project(hn, w_in[l])
        u = cg * xc
        cy = short_conv(jnp.pad(u, ((0, 0), (CONV_W - 1, 0), (0, 0))), conv_w[l])
        q = rope(q, pos_p)
        k = rope(k, pos_p)
        ao = attn_prompt(q, k, v, attn_sinks[l])
        hp = hp + merge_groups(bg, cy, ao, conv_norm_g[l], attn_norm_g[l], w_out[l])
        hp = hp + swiglu(rms_norm(hp, norm2_g[l]), w_gate[l], w_up[l], w_down[l])
        nkp.append(k[:, -WINDOW:])
        nvp.append(v[:, -WINDOW:])
        ncp.append(u[:, -(CONV_W - 1):])
        hn = rms_norm(hs, norm1_g[l])
        bg, cg, xc, q, k, v = project(hn, w_in[l])
        u_ext = jnp.concatenate([state_conv[l].astype(hs.dtype), cg * xc], axis=1)
        cy = short_conv(u_ext, conv_w[l])
        q = rope(q, pos_s)
        k = rope(k, pos_s)
        k_all = jnp.concatenate([cache_k[l].astype(k.dtype), k], axis=1)
        v_all = jnp.concatenate([cache_v[l].astype(v.dtype), v], axis=1)
        ao = attn_sample(q, k_all, v_all, attn_sinks[l])
        hs = hs + merge_groups(bg, cy, ao, conv_norm_g[l], attn_norm_g[l], w_out[l])
        hs = hs + swiglu(rms_norm(hs, norm2_g[l]), w_gate[l], w_up[l], w_down[l])
        nks.append(k_all[:, -WINDOW:])
        nvs.append(v_all[:, -WINDOW:])
        ncs.append(u_ext[:, -(CONV_W - 1):])
    y_prompt = rms_norm(hp[:, N_META:], final_norm_g)
    y_sample = rms_norm(hs, final_norm_g)
    return (y_prompt, y_sample, jnp.stack(nkp), jnp.stack(nvp), jnp.stack(ncp),
            jnp.stack(nks), jnp.stack(nvs), jnp.stack(ncs))
```

```python
import functools

import jax
import jax.numpy as jnp
from jax import lax
from jax.experimental import pallas as pl
from jax.experimental.pallas import tpu as pltpu

D_MODEL = 2048
N_META = 16
CONV_DIM = 1024
CONV_W = 3
HEAD_DIM = 64
HALF_HEAD = HEAD_DIM // 2
ATTN_DIM = 1024
N_HEADS = 16
N_KV_HEADS = 4
GROUP = N_HEADS // N_KV_HEADS
KV_DIM = N_KV_HEADS * HEAD_DIM
WINDOW = 128
ROPE_THETA = 10000.0
RMS_EPS = 1e-6
NEG_INF = -1e30
PAST_LEN = 16384

COL_BG = 0
COL_CG = CONV_DIM
COL_XC = 2 * CONV_DIM
COL_Q = 3 * CONV_DIM
COL_K = COL_Q + ATTN_DIM
COL_V = COL_K + KV_DIM
IN_COLS = COL_V + KV_DIM

LANES = 128
HEADS_PER_TILE = LANES // HEAD_DIM
VMEM_LIMIT = 56 * 1024 * 1024

F32 = jnp.float32
BF16 = jnp.bfloat16


def _rms(x, g):
    return x * lax.rsqrt(jnp.mean(x * x, axis=-1, keepdims=True) + RMS_EPS) * g


def _lane_tile(t, width):
    return jnp.concatenate([t] * (width // t.shape[-1]), axis=-1)


def _rope(t, cos, sin_signed):
    w = t.shape[-1]
    lane = lax.broadcasted_iota(jnp.int32, t.shape, 1)
    first_half = (lane & (HEAD_DIM - 1)) < HALF_HEAD
    swapped = jnp.where(first_half, pltpu.roll(t, w - HALF_HEAD, axis=1), pltpu.roll(t, HALF_HEAD, axis=1))
    return t * _lane_tile(cos, w) + swapped * _lane_tile(sin_signed, w)


def _proj_kernel(sample, x_ref, g1_ref, w_ref, cw_ref, cg_ref, cos_ref, sin_ref, prev_ref,
                 yc_ref, q_ref, k_ref, v_ref, u_ref, carry_ref):
    tm = x_ref.shape[1]
    hn = _rms(x_ref[0], g1_ref[...]).astype(BF16)

    def section(lo, width):
        return jnp.dot(hn, w_ref[:, lo:lo + width], preferred_element_type=F32)

    u = section(COL_CG, CONV_DIM) * section(COL_XC, CONV_DIM)
    cw = cw_ref[...]
    if sample:
        u2 = prev_ref[:, :CONV_DIM]
        u1 = prev_ref[:, CONV_DIM:]
        u_ref[0] = jnp.concatenate([u1, u], axis=-1)
    else:
        @pl.when(pl.program_id(1) == 0)
        def _():
            carry_ref[...] = prev_ref[...]

        row = lax.broadcasted_iota(jnp.int32, u.shape, 0)
        p2 = carry_ref[0:1, :]
        p1 = carry_ref[1:2, :]
        u1 = jnp.where(row == 0, p1, pltpu.roll(u, 1, axis=0))
        u2 = jnp.where(row == 0, p2, jnp.where(row == 1, p1, pltpu.roll(u, 2, axis=0)))
        tail = u[tm - (CONV_W - 1):, :]
        carry_ref[...] = tail
        u_ref[0] = tail
    cy = cw[0:1, :] * u2 + cw[1:2, :] * u1 + cw[2:3, :] * u
    yc_ref[0] = _rms(section(COL_BG, CONV_DIM) * cy, cg_ref[...]).astype(BF16)

    cos = cos_ref[...]
    sin = sin_ref[...]
    q_ref[0] = (_rope(section(COL_Q, ATTN_DIM), cos, sin) * (HEAD_DIM ** -0.5)).astype(BF16)
    k_ref[0] = _rope(section(COL_K, KV_DIM), cos, sin)
    v_ref[0] = section(COL_V, KV_DIM)


def _proj_call(x, g1, w_in, conv_w, conv_g, cos, sin, prev, *, tm, sample):
    nb, s, _ = x.shape
    prev_rows, prev_cols = prev.shape[-2:]
    u_rows, u_cols = (tm, 2 * CONV_DIM) if sample else (CONV_W - 1, CONV_DIM)
    const = lambda b, i: (0, 0)
    rows = lambda b, i: (b, i, 0)
    if sample:
        prev_spec = pl.BlockSpec((tm, prev_cols), lambda b, i: (i, 0))
        u_spec = pl.BlockSpec((1, u_rows, u_cols), rows)
        u_shape = (nb, s, u_cols)
    else:
        prev_spec = pl.BlockSpec((prev_rows, prev_cols), const)
        u_spec = pl.BlockSpec((1, u_rows, u_cols), lambda b, i: (b, 0, 0))
        u_shape = (nb, u_rows, u_cols)
    return pl.pallas_call(
        functools.partial(_proj_kernel, sample),
        grid=(nb, s // tm),
        in_specs=[
            pl.BlockSpec((1, tm, D_MODEL), rows),
            pl.BlockSpec((1, D_MODEL), const),
            pl.BlockSpec((D_MODEL, IN_COLS), const, pipeline_mode=pl.Buffered(1)),
            pl.BlockSpec((CONV_W, CONV_DIM), const),
            pl.BlockSpec((1, CONV_DIM), const),
            pl.BlockSpec((tm, LANES), lambda b, i: (i, 0)),
            pl.BlockSpec((tm, LANES), lambda b, i: (i, 0)),
            prev_spec,
        ],
        out_specs=[
            pl.BlockSpec((1, tm, CONV_DIM), rows),
            pl.BlockSpec((1, tm, ATTN_DIM), rows),
            pl.BlockSpec((1, tm, KV_DIM), rows),
            pl.BlockSpec((1, tm, KV_DIM), rows),
            u_spec,
        ],
        out_shape=[
            jax.ShapeDtypeStruct((nb, s, CONV_DIM), BF16),
            jax.ShapeDtypeStruct((nb, s, ATTN_DIM), BF16),
            jax.ShapeDtypeStruct((nb, s, KV_DIM), F32),
            jax.ShapeDtypeStruct((nb, s, KV_DIM), F32),
            jax.ShapeDtypeStruct(u_shape, F32),
        ],
        scratch_shapes=[pltpu.VMEM((CONV_W - 1, CONV_DIM), F32)],
        compiler_params=pltpu.CompilerParams(
            dimension_semantics=("arbitrary", "arbitrary"), vmem_limit_bytes=VMEM_LIMIT),
        name="proj_sample" if sample else "proj_prompt",
    )(x, g1, w_in, conv_w, conv_g, cos, sin, prev)


def _dup_heads(pair_tile, half):
    lane = lax.broadcasted_iota(jnp.int32, pair_tile.shape, 1)
    swapped = pltpu.roll(pair_tile, HEAD_DIM, axis=1)
    if half == 0:
        return jnp.where(lane < HEAD_DIM, pair_tile, swapped)
    return jnp.where(lane < HEAD_DIM, swapped, pair_tile)


def _attn_prompt_kernel(sinks_ref, q_ref, kc_ref, kp_ref, vc_ref, vp_ref, km_ref, vm_ref, ag_ref,
                        ya_ref, o_scr):
    j = pl.program_id(1)
    blk = q_ref.shape[1]
    first = j == 0
    keys = jnp.concatenate([jnp.where(first, km_ref[...], kp_ref[0]), kc_ref[0]], axis=0)
    vals = jnp.concatenate([jnp.where(first, vm_ref[...], vp_ref[0]), vc_ref[0]], axis=0)
    nk = keys.shape[0]

    qi = jnp.concatenate([lax.broadcasted_iota(jnp.int32, (blk, nk), 0)] * GROUP, axis=0)
    kj = lax.broadcasted_iota(jnp.int32, (GROUP * blk, nk), 1)
    first_valid = jnp.where(first, blk - N_META, 0)
    mask = (kj >= qi) & (kj <= qi + WINDOW) & (kj >= first_valid)

    lane = lax.broadcasted_iota(jnp.int32, (blk, LANES), 1)
    for g in range(N_KV_HEADS):
        pair, half = divmod(g, HEADS_PER_TILE)
        kd = _dup_heads(keys[:, pair * LANES:(pair + 1) * LANES], half).astype(BF16)
        vd = _dup_heads(vals[:, pair * LANES:(pair + 1) * LANES], half).astype(BF16)
        q_rows, sink_rows = [], []
        for i in range(GROUP):
            h = g * GROUP + i
            tile, hh = divmod(h, HEADS_PER_TILE)
            qt = q_ref[0, :, tile * LANES:(tile + 1) * LANES]
            keep = (lane < HEAD_DIM) if hh == 0 else (lane >= HEAD_DIM)
            q_rows.append(jnp.where(keep, qt, jnp.zeros_like(qt)))
            sink_rows.append(jnp.full((blk, 1), sinks_ref[h], F32))
        qs = jnp.concatenate(q_rows, axis=0)
        sink = jnp.concatenate(sink_rows, axis=0)
        s = lax.dot_general(qs, kd, (((1,), (1,)), ((), ())), preferred_element_type=F32)
        s = jnp.where(mask, s, NEG_INF)
        m = jnp.maximum(jnp.max(s, axis=-1, keepdims=True), sink)
        p = jnp.exp(s - m)
        denom = jnp.sum(p, axis=-1, keepdims=True) + jnp.exp(sink - m)
        o = jnp.dot(p.astype(BF16), vd, preferred_element_type=F32) / denom
        for t in range(GROUP // HEADS_PER_TILE):
            lo = o[(2 * t) * blk:(2 * t + 1) * blk]
            hi = o[(2 * t + 1) * blk:(2 * t + 2) * blk]
            tile = (g * GROUP) // HEADS_PER_TILE + t
            o_scr[:, tile * LANES:(tile + 1) * LANES] = jnp.where(lane < HEAD_DIM, lo, hi)
    ya_ref[0] = _rms(o_scr[...], ag_ref[...]).astype(BF16)


def _attn_prompt(q, k, v, k_meta, v_meta, sinks, attn_g):
    nb, s, _ = q.shape
    blk = WINDOW
    cur = lambda b, j: (b, j, 0)
    prv = lambda b, j: (b, jnp.maximum(j - 1, 0), 0)
    const = lambda b, j: (0, 0)
    return pl.pallas_call(
        _attn_prompt_kernel,
        grid=(nb, s // blk),
        in_specs=[
            pl.BlockSpec(memory_space=pltpu.SMEM),
            pl.BlockSpec((1, blk, ATTN_DIM), cur),
            pl.BlockSpec((1, blk, KV_DIM), cur),
            pl.BlockSpec((1, blk, KV_DIM), prv),
            pl.BlockSpec((1, blk, KV_DIM), cur),
            pl.BlockSpec((1, blk, KV_DIM), prv),
            pl.BlockSpec((blk, KV_DIM), const),
            pl.BlockSpec((blk, KV_DIM), const),
            pl.BlockSpec((1, ATTN_DIM), const),
        ],
        out_specs=pl.BlockSpec((1, blk, ATTN_DIM), cur),
        out_shape=jax.ShapeDtypeStruct((nb, s, ATTN_DIM), BF16),
        scratch_shapes=[pltpu.VMEM((blk, ATTN_DIM), F32)],
        compiler_params=pltpu.CompilerParams(
            dimension_semantics=("arbitrary", "arbitrary"), vmem_limit_bytes=VMEM_LIMIT),
        name="attn_prompt",
    )(sinks, q, k, k, v, v, k_meta, v_meta, attn_g)


def _attn_sample_kernel(sinks_ref, q_ref, kn_ref, vn_ref, ck_ref, cv_ref, ag_ref,
                        ya_ref, nk_ref, nv_ref):
    nseq = q_ref.shape[0]
    head_of_lane = lax.broadcasted_iota(jnp.int32, (N_HEADS, ATTN_DIM), 1) // HEAD_DIM
    head_of_row = lax.broadcasted_iota(jnp.int32, (N_HEADS, ATTN_DIM), 0)
    own = head_of_lane == head_of_row
    sink = jnp.concatenate([jnp.full((1, 1), sinks_ref[h], F32) for h in range(N_HEADS)], axis=0)
    key_row = lax.broadcasted_iota(jnp.int32, (WINDOW, KV_DIM), 0)

    def expand(t):
        tiles = []
        for g in range(N_KV_HEADS):
            pair, half = divmod(g, HEADS_PER_TILE)
            d = _dup_heads(t[:, pair * LANES:(pair + 1) * LANES], half)
            tiles += [d] * (GROUP // HEADS_PER_TILE)
        return jnp.concatenate(tiles, axis=-1)

    for n in range(nseq):
        kc = ck_ref[n]
        vc = cv_ref[n]
        kn = kn_ref[n]
        vn = vn_ref[n]
        q16 = jnp.where(own, jnp.broadcast_to(q_ref[n].astype(F32), (N_HEADS, ATTN_DIM)), 0.0)
        s_c = lax.dot_general(q16.astype(BF16), expand(kc).astype(BF16), (((1,), (1,)), ((), ())),
                              preferred_element_type=F32)
        s_n = jnp.sum(q16 * expand(kn), axis=-1, keepdims=True)
        m = jnp.maximum(jnp.maximum(jnp.max(s_c, axis=-1, keepdims=True), s_n), sink)
        p_c = jnp.exp(s_c - m)
        p_n = jnp.exp(s_n - m)
        denom = jnp.sum(p_c, axis=-1, keepdims=True) + p_n + jnp.exp(sink - m)
        o16 = jnp.dot(p_c.astype(BF16), expand(vc).astype(BF16), preferred_element_type=F32)
        o16 = (o16 + p_n * expand(vn)) / denom
        o = jnp.sum(jnp.where(own, o16, 0.0), axis=0, keepdims=True)
        ya_ref[n] = _rms(o, ag_ref[...]).astype(BF16)
        nk_ref[n] = jnp.where(key_row == WINDOW - 1, kn, pltpu.roll(kc, WINDOW - 1, axis=0))
        nv_ref[n] = jnp.where(key_row == WINDOW - 1, vn, pltpu.roll(vc, WINDOW - 1, axis=0))


def _attn_sample(q, k_new, v_new, cache_k, cache_v, sinks, attn_g, *, nseq):
    n = q.shape[0]
    row = lambda i: (i, 0, 0)
    return pl.pallas_call(
        _attn_sample_kernel,
        grid=(n // nseq,),
        in_specs=[
            pl.BlockSpec(memory_space=pltpu.SMEM),
            pl.BlockSpec((nseq, 1, ATTN_DIM), row),
            pl.BlockSpec((nseq, 1, KV_DIM), row),
            pl.BlockSpec((nseq, 1, KV_DIM), row),
            pl.BlockSpec((nseq, WINDOW, KV_DIM), row),
            pl.BlockSpec((nseq, WINDOW, KV_DIM), row),
            pl.BlockSpec((1, ATTN_DIM), lambda i: (0, 0)),
        ],
        out_specs=[
            pl.BlockSpec((nseq, 1, ATTN_DIM), row),
            pl.BlockSpec((nseq, WINDOW, KV_DIM), row),
            pl.BlockSpec((nseq, WINDOW, KV_DIM), row),
        ],
        out_shape=[
            jax.ShapeDtypeStruct((n, 1, ATTN_DIM), BF16),
            jax.ShapeDtypeStruct((n, WINDOW, KV_DIM), F32),
            jax.ShapeDtypeStruct((n, WINDOW, KV_DIM), F32),
        ],
        compiler_params=pltpu.CompilerParams(
            dimension_semantics=("arbitrary",), vmem_limit_bytes=VMEM_LIMIT),
        name="attn_sample",
    )(sinks, q, k_new, v_new, cache_k, cache_v, attn_g)


def _mlp_kernel(x_ref, yc_ref, ya_ref, wo_ref, g2_ref, wg_ref, wu_ref, wd_ref, gf_ref,
                y_ref, hn_scr):
    c = pl.program_id(1)

    @pl.when(c == 0)
    def _():
        mix = (jnp.dot(yc_ref[...], wo_ref[:CONV_DIM, :], preferred_element_type=F32)
               + jnp.dot(ya_ref[...], wo_ref[CONV_DIM:, :], preferred_element_type=F32))
        h = x_ref[...] + mix
        y_ref[...] = h
        hn_scr[...] = _rms(h, g2_ref[...]).astype(BF16)

    hn = hn_scr[...]
    gate = jnp.dot(hn, wg_ref[...], preferred_element_type=F32)
    up = jnp.dot(hn, wu_ref[...], preferred_element_type=F32)
    act = (gate * jax.nn.sigmoid(gate) * up).astype(BF16)
    y_ref[...] += jnp.dot(act, wd_ref[...], preferred_element_type=F32)

    @pl.when(c == pl.num_programs(1) - 1)
    def _():
        y_ref[...] = _rms(y_ref[...], gf_ref[...])


def _mlp_call(x, yc, ya, w_out, g2, w_gate, w_up, w_down, gf, *, tm, tf, name):
    r = x.shape[0]
    d_ff = w_gate.shape[1]
    rows = lambda i, c: (i, 0)
    const = lambda i, c: (0, 0)
    return pl.pallas_call(
        _mlp_kernel,
        grid=(r // tm, d_ff // tf),
        in_specs=[
            pl.BlockSpec((tm, D_MODEL), rows),
            pl.BlockSpec((tm, CONV_DIM), rows),
            pl.BlockSpec((tm, ATTN_DIM), rows),
            pl.BlockSpec((D_MODEL, D_MODEL), const, pipeline_mode=pl.Buffered(1)),
            pl.BlockSpec((1, D_MODEL), const),
            pl.BlockSpec((D_MODEL, tf), lambda i, c: (0, c)),
            pl.BlockSpec((D_MODEL, tf), lambda i, c: (0, c)),
            pl.BlockSpec((tf, D_MODEL), lambda i, c: (c, 0)),
            pl.BlockSpec((1, D_MODEL), const),
        ],
        out_specs=pl.BlockSpec((tm, D_MODEL), rows),
        out_shape=jax.ShapeDtypeStruct((r, D_MODEL), F32),
        scratch_shapes=[pltpu.VMEM((tm, D_MODEL), BF16)],
        compiler_params=pltpu.CompilerParams(
            dimension_semantics=("arbitrary", "arbitrary"), vmem_limit_bytes=VMEM_LIMIT),
        name=name,
    )(x, yc, ya, w_out, g2, w_gate, w_up, w_down, gf)


def _rope_tables(pos):
    inv = ROPE_THETA ** (-jnp.arange(HALF_HEAD, dtype=F32) / HALF_HEAD)
    ang = pos.astype(F32)[:, None] * inv[None, :]
    cos = jnp.cos(ang)
    sin = jnp.sin(ang)
    reps = LANES // HEAD_DIM
    return (jnp.tile(jnp.concatenate([cos, cos], axis=-1), (1, reps)),
            jnp.tile(jnp.concatenate([-sin, sin], axis=-1), (1, reps)))


def kernel(x_prompt, x_sample, cache_k, cache_v, state_conv, meta_tokens, norm1_g, w_in, conv_w, conv_norm_g,
           attn_norm_g, attn_sinks, w_out, norm2_g, w_gate, w_up, w_down, final_norm_g):
    depth = w_in.shape[0]
    assert depth == 1, "single-layer step only"
    nb, seq, _ = x_prompt.shape
    ns, dec_seq, _ = x_sample.shape
    assert dec_seq == 1

    g1 = norm1_g[0][None]
    g2 = norm2_g[0][None]
    gf = final_norm_g[None]
    cg = conv_norm_g[0][None]
    ag = attn_norm_g[0][None]
    cw = conv_w[0]
    sinks = attn_sinks[0]
    w_in_b = w_in[0].astype(BF16)
    w_out_b = w_out[0].astype(BF16)
    w_gate_b = w_gate[0].astype(BF16)
    w_up_b = w_up[0].astype(BF16)
    w_down_b = w_down[0].astype(BF16)

    cos_m, sin_m = _rope_tables(jnp.arange(N_META))
    zero_prev = jnp.zeros((CONV_W - 1, CONV_DIM), F32)
    _, _, k_m, v_m, u_m = _proj_call(meta_tokens[None], g1, w_in_b, cw, cg, cos_m, sin_m, zero_prev,
                                     tm=N_META, sample=False)
    pad = ((WINDOW - N_META, 0), (0, 0))
    k_meta = jnp.pad(k_m[0], pad)
    v_meta = jnp.pad(v_m[0], pad)

    cos_p, sin_p = _rope_tables(N_META + jnp.arange(seq))
    yc, q, k, v, u_tail = _proj_call(x_prompt, g1, w_in_b, cw, cg, cos_p, sin_p, u_m[0], tm=512, sample=False)
    ya = _attn_prompt(q, k, v, k_meta, v_meta, sinks, ag)
    rp = nb * seq
    y_prompt = _mlp_call(x_prompt.reshape(rp, D_MODEL), yc.reshape(rp, CONV_DIM), ya.reshape(rp, ATTN_DIM),
                         w_out_b, g2, w_gate_b, w_up_b, w_down_b, gf, tm=512, tf=512, name="mlp_prompt")
    y_prompt = y_prompt.reshape(nb, seq, D_MODEL)
    new_k_prompt = k[:, seq - WINDOW:].reshape(1, nb, WINDOW, N_KV_HEADS, HEAD_DIM)
    new_v_prompt = v[:, seq - WINDOW:].reshape(1, nb, WINDOW, N_KV_HEADS, HEAD_DIM)
    new_conv_prompt = u_tail[None]

    xs = x_sample.reshape(1, ns, D_MODEL)
    cos_s, sin_s = _rope_tables(jnp.full((ns,), PAST_LEN))
    prev_s = state_conv[0].reshape(ns, (CONV_W - 1) * CONV_DIM)
    yc_s, q_s, k_s, v_s, u_s = _proj_call(xs, g1, w_in_b, cw, cg, cos_s, sin_s, prev_s, tm=ns, sample=True)
    ya_s, nk_s, nv_s = _attn_sample(
        q_s.reshape(ns, 1, ATTN_DIM), k_s.reshape(ns, 1, KV_DIM), v_s.reshape(ns, 1, KV_DIM),
        cache_k[0].reshape(ns, WINDOW, KV_DIM), cache_v[0].reshape(ns, WINDOW, KV_DIM), sinks, ag, nseq=8)
    y_sample = _mlp_call(xs[0], yc_s[0], ya_s.reshape(ns, ATTN_DIM), w_out_b, g2, w_gate_b, w_up_b, w_down_b, gf,
                         tm=ns, tf=512, name="mlp_sample")
    y_sample = y_sample.reshape(ns, 1, D_MODEL)
    new_k_sample = nk_s.reshape(1, ns, WINDOW, N_KV_HEADS, HEAD_DIM)
    new_v_sample = nv_s.reshape(1, ns, WINDOW, N_KV_HEADS, HEAD_DIM)
    new_conv_sample = u_s.reshape(1, ns, CONV_W - 1, CONV_DIM)

    return (y_prompt, y_sample, new_k_prompt, new_v_prompt, new_conv_prompt,
            new_k_sample, new_v_sample, new_conv_sample)
```

```python
import functools

import jax
import jax.numpy as jnp
from jax import lax
from jax.experimental import pallas as pl
from jax.experimental.pallas import tpu as pltpu

D_MODEL = 2048
N_META = 16
CONV_DIM = 1024
CONV_W = 3
HEAD_DIM = 64
HALF_HEAD = HEAD_DIM // 2
ATTN_DIM = 1024
N_HEADS = 16
N_KV_HEADS = 4
GROUP = N_HEADS // N_KV_HEADS
KV_DIM = N_KV_HEADS * HEAD_DIM
WINDOW = 128
ROPE_THETA = 10000.0
RMS_EPS = 1e-6
NEG_INF = -1e30
PAST_LEN = 16384
LOG2_E = 1.4426950408889634

COL_BG = 0
COL_CG = CONV_DIM
COL_XC = 2 * CONV_DIM
COL_Q = 3 * CONV_DIM
COL_K = COL_Q + ATTN_DIM
COL_V = COL_K + KV_DIM
IN_COLS = COL_V + KV_DIM

LANES = 128
HEADS_PER_TILE = LANES // HEAD_DIM
VMEM_LIMIT = 56 * 1024 * 1024

F32 = jnp.float32
BF16 = jnp.bfloat16


def _rms(x, g):
    return x * lax.rsqrt(jnp.mean(x * x, axis=-1, keepdims=True) + RMS_EPS) * g


def _lane_tile(t, width):
    return jnp.concatenate([t] * (width // t.shape[-1]), axis=-1)


def _rope(t, cos, sin_signed):
    w = t.shape[-1]
    lane = lax.broadcasted_iota(jnp.int32, t.shape, 1)
    first_half = (lane & (HEAD_DIM - 1)) < HALF_HEAD
    swapped = jnp.where(first_half, pltpu.roll(t, w - HALF_HEAD, axis=1), pltpu.roll(t, HALF_HEAD, axis=1))
    return t * _lane_tile(cos, w) + swapped * _lane_tile(sin_signed, w)


def _dup_heads(pair_tile, half):
    lane = lax.broadcasted_iota(jnp.int32, pair_tile.shape, 1)
    swapped = pltpu.roll(pair_tile, HEAD_DIM, axis=1)
    if half == 0:
        return jnp.where(lane < HEAD_DIM, pair_tile, swapped)
    return jnp.where(lane < HEAD_DIM, swapped, pair_tile)


def _dup_all(t):
    tiles = []
    for g in range(N_KV_HEADS):
        pair, half = divmod(g, HEADS_PER_TILE)
        tiles.append(_dup_heads(t[:, pair * LANES:(pair + 1) * LANES], half))
    return jnp.concatenate(tiles, axis=-1).astype(BF16)


def _spread_heads(q):
    lane = lax.broadcasted_iota(jnp.int32, (q.shape[0], LANES), 1)
    tiles = []
    for h in range(N_HEADS):
        tile, half = divmod(h, HEADS_PER_TILE)
        qt = q[:, tile * LANES:(tile + 1) * LANES]
        keep = (lane < HEAD_DIM) if half == 0 else (lane >= HEAD_DIM)
        tiles.append(jnp.where(keep, qt, jnp.zeros_like(qt)))
    return jnp.concatenate(tiles, axis=-1)


def _head_block(t, h):
    return t[:, h * HEAD_DIM:(h + 1) * HEAD_DIM]


def _group_minor(t):
    return jnp.concatenate([_head_block(t, g * GROUP + i) for i in range(GROUP) for g in range(N_KV_HEADS)], axis=-1)


def _group_major(t):
    return jnp.concatenate([_head_block(t, i * N_KV_HEADS + g) for g in range(N_KV_HEADS) for i in range(GROUP)],
                           axis=-1)


def _proj_kernel(sample, x_ref, g1_ref, w_ref, cw_ref, cg_ref, cos_ref, sin_ref, prev_ref,
                 yc_ref, q_ref, kd_ref, vd_ref, kt_ref, vt_ref, u_ref, carry_ref):
    tm = x_ref.shape[1]
    hn = _rms(x_ref[0], g1_ref[...]).astype(BF16)

    def section(lo, width):
        return jnp.dot(hn, w_ref[:, lo:lo + width], preferred_element_type=F32)

    u = section(COL_CG, CONV_DIM) * section(COL_XC, CONV_DIM)
    cw = cw_ref[...]
    if sample:
        u2 = prev_ref[:, :CONV_DIM]
        u1 = prev_ref[:, CONV_DIM:]
        u_ref[0] = jnp.concatenate([u1, u], axis=-1)
    else:
        @pl.when(pl.program_id(1) == 0)
        def _():
            carry_ref[...] = prev_ref[...]

        row = lax.broadcasted_iota(jnp.int32, u.shape, 0)
        p2 = carry_ref[0:1, :]
        p1 = carry_ref[1:2, :]
        u1 = jnp.where(row == 0, p1, pltpu.roll(u, 1, axis=0))
        u2 = jnp.where(row == 0, p2, jnp.where(row == 1, p1, pltpu.roll(u, 2, axis=0)))
        tail = u[tm - (CONV_W - 1):, :]
        carry_ref[...] = tail
        u_ref[0] = tail
    cy = cw[0:1, :] * u2 + cw[1:2, :] * u1 + cw[2:3, :] * u
    yc_ref[0] = _rms(section(COL_BG, CONV_DIM) * cy, cg_ref[...]).astype(BF16)

    cos = cos_ref[...]
    sin = sin_ref[...]
    q = _rope(section(COL_Q, ATTN_DIM), cos, sin) * (HEAD_DIM ** -0.5 * LOG2_E)
    k = _rope(section(COL_K, KV_DIM), cos, sin)
    v = section(COL_V, KV_DIM)
    if sample:
        q_ref[0] = _group_minor(q).astype(BF16)
        kd_ref[0] = k.T
        vd_ref[0] = v.T
    else:
        q_ref[0] = _spread_heads(q.astype(BF16))
        kd_ref[0] = _dup_all(k)
        vd_ref[0] = _dup_all(v)
    tail_rows = kt_ref.shape[1]
    kt_ref[0] = k[tm - tail_rows:, :]
    vt_ref[0] = v[tm - tail_rows:, :]


def _proj_call(x, g1, w_in, conv_w, conv_g, cos, sin, prev, *, tm, sample):
    nb, s, _ = x.shape
    prev_rows, prev_cols = prev.shape[-2:]
    u_rows, u_cols = (tm, 2 * CONV_DIM) if sample else (CONV_W - 1, CONV_DIM)
    tail_rows = min(tm, WINDOW)
    const = lambda b, i: (0, 0)
    rows = lambda b, i: (b, i, 0)
    per_seq = lambda b, i: (b, 0, 0)
    if sample:
        prev_spec = pl.BlockSpec((tm, prev_cols), lambda b, i: (i, 0))
        u_spec = pl.BlockSpec((1, u_rows, u_cols), rows)
        u_shape = (nb, s, u_cols)
        q_cols = ATTN_DIM
        kv_spec = pl.BlockSpec((1, KV_DIM, tm), lambda b, i: (b, 0, i))
        kv_shape = jax.ShapeDtypeStruct((nb, KV_DIM, s), F32)
    else:
        prev_spec = pl.BlockSpec((prev_rows, prev_cols), const)
        u_spec = pl.BlockSpec((1, u_rows, u_cols), per_seq)
        u_shape = (nb, u_rows, u_cols)
        q_cols = N_HEADS * LANES
        kv_spec = pl.BlockSpec((1, tm, N_KV_HEADS * LANES), rows)
        kv_shape = jax.ShapeDtypeStruct((nb, s, N_KV_HEADS * LANES), BF16)
    return pl.pallas_call(
        functools.partial(_proj_kernel, sample),
        grid=(nb, s // tm),
        in_specs=[
            pl.BlockSpec((1, tm, D_MODEL), rows),
            pl.BlockSpec((1, D_MODEL), const),
            pl.BlockSpec((D_MODEL, IN_COLS), const, pipeline_mode=pl.Buffered(1)),
            pl.BlockSpec((CONV_W, CONV_DIM), const),
            pl.BlockSpec((1, CONV_DIM), const),
            pl.BlockSpec((tm, LANES), lambda b, i: (i, 0)),
            pl.BlockSpec((tm, LANES), lambda b, i: (i, 0)),
            prev_spec,
        ],
        out_specs=[
            pl.BlockSpec((1, tm, CONV_DIM), rows),
            pl.BlockSpec((1, tm, q_cols), rows),
            kv_spec,
            kv_spec,
            pl.BlockSpec((1, tail_rows, KV_DIM), per_seq),
            pl.BlockSpec((1, tail_rows, KV_DIM), per_seq),
            u_spec,
        ],
        out_shape=[
            jax.ShapeDtypeStruct((nb, s, CONV_DIM), BF16),
            jax.ShapeDtypeStruct((nb, s, q_cols), BF16),
            kv_shape,
            kv_shape,
            jax.ShapeDtypeStruct((nb, tail_rows, KV_DIM), F32),
            jax.ShapeDtypeStruct((nb, tail_rows, KV_DIM), F32),
            jax.ShapeDtypeStruct(u_shape, F32),
        ],
        scratch_shapes=[pltpu.VMEM((CONV_W - 1, CONV_DIM), F32)],
        compiler_params=pltpu.CompilerParams(
            dimension_semantics=("arbitrary", "arbitrary"), vmem_limit_bytes=VMEM_LIMIT),
        name="proj_sample" if sample else "proj_prompt",
    )(x, g1, w_in, conv_w, conv_g, cos, sin, prev)


def _attn_prompt_kernel(sinks_ref, q_ref, kc_ref, kp_ref, vc_ref, vp_ref, km_ref, vm_ref, ag_ref,
                        ya_ref, o_scr):
    first = pl.program_id(1) == 0
    blk = WINDOW
    nk = 2 * blk
    n_sub = q_ref.shape[1] // blk

    qi = lax.broadcasted_iota(jnp.int32, (blk, nk), 0)
    kj = lax.broadcasted_iota(jnp.int32, (blk, nk), 1)
    band = (kj >= qi) & (kj <= qi + WINDOW)
    band_first = band & (kj >= jnp.where(first, blk - N_META, 0))

    lane = lax.broadcasted_iota(jnp.int32, (blk, LANES), 1)
    ones = jnp.ones((nk, LANES), BF16)
    for t in range(n_sub):
        rows = slice(t * blk, (t + 1) * blk)
        valid = band_first if t == 0 else band
        for g in range(N_KV_HEADS):
            cols = slice(g * LANES, (g + 1) * LANES)
            if t == 0:
                k_prev = jnp.where(first, km_ref[:, cols], kp_ref[0, :, cols])
                v_prev = jnp.where(first, vm_ref[:, cols], vp_ref[0, :, cols])
            else:
                k_prev = kc_ref[0, (t - 1) * blk:t * blk, cols]
                v_prev = vc_ref[0, (t - 1) * blk:t * blk, cols]
            kd = jnp.concatenate([k_prev, kc_ref[0, rows, cols]], axis=0)
            vd = jnp.concatenate([v_prev, vc_ref[0, rows, cols]], axis=0)
            v_rhs = jnp.concatenate([vd, ones], axis=1)
            for tile in range(g * GROUP // HEADS_PER_TILE, (g + 1) * GROUP // HEADS_PER_TILE):
                ov, m = [], []
                for h in (HEADS_PER_TILE * tile, HEADS_PER_TILE * tile + 1):
                    qm = q_ref[0, rows, h * LANES:(h + 1) * LANES]
                    s = lax.dot_general(qm, kd, (((1,), (1,)), ((), ())), preferred_element_type=F32)
                    s = jnp.where(valid, s, NEG_INF)
                    m.append(jnp.max(s, axis=-1, keepdims=True))
                    p = jnp.exp2(s - m[-1]).astype(BF16)
                    ov.append(jnp.dot(p, v_rhs, preferred_element_type=F32))
                low = lane < HEAD_DIM
                sink = jnp.where(low[:1], sinks_ref[HEADS_PER_TILE * tile], sinks_ref[HEADS_PER_TILE * tile + 1])
                denom = (jnp.where(low, ov[0][:, LANES:], ov[1][:, LANES:])
                         + jnp.exp2(sink * LOG2_E - jnp.where(low, m[0], m[1])))
                o_scr[rows, tile * LANES:(tile + 1) * LANES] = (
                    jnp.where(low, ov[0][:, :LANES], ov[1][:, :LANES]) / denom)
        ya_ref[0, rows, :] = _rms(o_scr[rows, :], ag_ref[...]).astype(BF16)


def _attn_prompt(q, kd, vd, kd_meta, vd_meta, sinks, attn_g, *, n_sub):
    nb, s, kd_cols = kd.shape
    blk = WINDOW
    tq = n_sub * blk
    cur = lambda b, j: (b, j, 0)
    prv = lambda b, j: (b, jnp.maximum(j * n_sub - 1, 0), 0)
    const = lambda b, j: (0, 0)
    return pl.pallas_call(
        _attn_prompt_kernel,
        grid=(nb, s // tq),
        in_specs=[
            pl.BlockSpec(memory_space=pltpu.SMEM),
            pl.BlockSpec((1, tq, N_HEADS * LANES), cur),
            pl.BlockSpec((1, tq, kd_cols), cur),
            pl.BlockSpec((1, blk, kd_cols), prv),
            pl.BlockSpec((1, tq, kd_cols), cur),
            pl.BlockSpec((1, blk, kd_cols), prv),
            pl.BlockSpec((blk, kd_cols), const),
            pl.BlockSpec((blk, kd_cols), const),
            pl.BlockSpec((1, ATTN_DIM), const),
        ],
        out_specs=pl.BlockSpec((1, tq, ATTN_DIM), cur),
        out_shape=jax.ShapeDtypeStruct((nb, s, ATTN_DIM), BF16),
        scratch_shapes=[pltpu.VMEM((tq, ATTN_DIM), F32)],
        compiler_params=pltpu.CompilerParams(
            dimension_semantics=("arbitrary", "arbitrary"), vmem_limit_bytes=VMEM_LIMIT),
        name="attn_prompt",
    )(sinks, q, kd, kd, vd, vd, kd_meta, vd_meta, attn_g)


def _attn_sample_kernel(sinks_ref, q_ref, kn_ref, vn_ref, knt_ref, vnt_ref, ck_ref, cv_ref, ag_ref,
                        ya_ref, nk_ref, nv_ref, o_scr):
    nseq = q_ref.shape[0]
    base = pl.program_id(0) * nseq
    row = lax.broadcasted_iota(jnp.int32, (N_HEADS, KV_DIM), 0)
    lane = lax.broadcasted_iota(jnp.int32, (N_HEADS, KV_DIM), 1)
    own = (lane // HEAD_DIM) == (row % N_KV_HEADS)
    sink = jnp.concatenate(
        [jnp.full((1, 1), sinks_ref[(r % N_KV_HEADS) * GROUP + r // N_KV_HEADS] * LOG2_E, F32)
         for r in range(N_HEADS)], axis=0)
    key_lane = lax.broadcasted_iota(jnp.int32, (KV_DIM, WINDOW), 1)

    for n in range(nseq):
        kt = ck_ref[n]
        vt = cv_ref[n]
        kn = kn_ref[n:n + 1, :]
        vn = vn_ref[n:n + 1, :]
        qrow = q_ref[n:n + 1, :].astype(F32)
        q16 = jnp.broadcast_to(qrow[:, (GROUP - 1) * KV_DIM:], (N_HEADS, KV_DIM))
        for i in range(GROUP - 2, -1, -1):
            q16 = jnp.where(row < (i + 1) * N_KV_HEADS, qrow[:, i * KV_DIM:(i + 1) * KV_DIM], q16)
        q16 = jnp.where(own, q16, 0.0)
        s_c = jnp.dot(q16.astype(BF16), kt.astype(BF16), preferred_element_type=F32)
        s_n = jnp.sum(q16 * kn, axis=-1, keepdims=True)
        m = jnp.maximum(jnp.maximum(jnp.max(s_c, axis=-1, keepdims=True), s_n), sink)
        p_c = jnp.exp2(s_c - m)
        p_n = jnp.exp2(s_n - m)
        denom = jnp.sum(p_c, axis=-1, keepdims=True) + p_n + jnp.exp2(sink - m)
        o16 = lax.dot_general(p_c.astype(BF16), vt.astype(BF16), (((1,), (1,)), ((), ())),
                              preferred_element_type=F32)
        o16 = jnp.where(own, (o16 + p_n * vn) / denom, 0.0)
        for i in range(GROUP):
            in_group = (row >= i * N_KV_HEADS) & (row < (i + 1) * N_KV_HEADS)
            o_scr[n:n + 1, i * KV_DIM:(i + 1) * KV_DIM] = jnp.sum(jnp.where(in_group, o16, 0.0), axis=0, keepdims=True)
        is_seq = key_lane == base + n
        k_col = jnp.sum(jnp.where(is_seq, knt_ref[...], 0.0), axis=1, keepdims=True)
        v_col = jnp.sum(jnp.where(is_seq, vnt_ref[...], 0.0), axis=1, keepdims=True)
        nk_ref[n] = jnp.where(key_lane == WINDOW - 1, k_col, pltpu.roll(kt, WINDOW - 1, axis=1))
        nv_ref[n] = jnp.where(key_lane == WINDOW - 1, v_col, pltpu.roll(vt, WINDOW - 1, axis=1))
    ya_ref[...] = _rms(_group_major(o_scr[...]), ag_ref[...]).astype(BF16)


def _attn_sample(q, k_new, v_new, k_new_t, v_new_t, cache_kt, cache_vt, sinks, attn_g, *, nseq):
    n = q.shape[0]
    rows = lambda i: (i, 0)
    seqs = lambda i: (i, 0, 0)
    const = lambda i: (0, 0)
    return pl.pallas_call(
        _attn_sample_kernel,
        grid=(n // nseq,),
        in_specs=[
            pl.BlockSpec(memory_space=pltpu.SMEM),
            pl.BlockSpec((nseq, ATTN_DIM), rows),
            pl.BlockSpec((nseq, KV_DIM), rows),
            pl.BlockSpec((nseq, KV_DIM), rows),
            pl.BlockSpec((KV_DIM, n), const),
            pl.BlockSpec((KV_DIM, n), const),
            pl.BlockSpec((nseq, KV_DIM, WINDOW), seqs),
            pl.BlockSpec((nseq, KV_DIM, WINDOW), seqs),
            pl.BlockSpec((1, ATTN_DIM), const),
        ],
        out_specs=[
            pl.BlockSpec((nseq, ATTN_DIM), rows),
            pl.BlockSpec((nseq, KV_DIM, WINDOW), seqs),
            pl.BlockSpec((nseq, KV_DIM, WINDOW), seqs),
        ],
        out_shape=[
            jax.ShapeDtypeStruct((n, ATTN_DIM), BF16),
            jax.ShapeDtypeStruct((n, KV_DIM, WINDOW), F32),
            jax.ShapeDtypeStruct((n, KV_DIM, WINDOW), F32),
        ],
        scratch_shapes=[pltpu.VMEM((nseq, ATTN_DIM), F32)],
        compiler_params=pltpu.CompilerParams(
            dimension_semantics=("arbitrary",), vmem_limit_bytes=VMEM_LIMIT),
        name="attn_sample",
    )(sinks, q, k_new, v_new, k_new_t, v_new_t, cache_kt, cache_vt, attn_g)


def _mlp_kernel(x_ref, yc_ref, ya_ref, wo_ref, g2_ref, wg_ref, wu_ref, wd_ref, gf_ref,
                y_ref, hn_scr):
    c = pl.program_id(1)

    @pl.when(c == 0)
    def _():
        mix = (jnp.dot(yc_ref[...], wo_ref[:CONV_DIM, :], preferred_element_type=F32)
               + jnp.dot(ya_ref[...], wo_ref[CONV_DIM:, :], preferred_element_type=F32))
        h = x_ref[...] + mix
        y_ref[...] = h
        hn_scr[...] = _rms(h, g2_ref[...]).astype(BF16)

    hn = hn_scr[...]
    gate = jnp.dot(hn, wg_ref[...], preferred_element_type=F32)
    up = jnp.dot(hn, wu_ref[...], preferred_element_type=F32)
    act = (gate * jax.nn.sigmoid(gate) * up).astype(BF16)
    y_ref[...] += jnp.dot(act, wd_ref[...], preferred_element_type=F32)

    @pl.when(c == pl.num_programs(1) - 1)
    def _():
        y_ref[...] = _rms(y_ref[...], gf_ref[...])


def _mlp_call(x, yc, ya, w_out, g2, w_gate, w_up, w_down, gf, *, tm, tf, name):
    r = x.shape[0]
    d_ff = w_gate.shape[1]
    rows = lambda i, c: (i, 0)
    const = lambda i, c: (0, 0)
    return pl.pallas_call(
        _mlp_kernel,
        grid=(r // tm, d_ff // tf),
        in_specs=[
            pl.BlockSpec((tm, D_MODEL), rows),
            pl.BlockSpec((tm, CONV_DIM), rows),
            pl.BlockSpec((tm, ATTN_DIM), rows),
            pl.BlockSpec((D_MODEL, D_MODEL), const, pipeline_mode=pl.Buffered(1)),
            pl.BlockSpec((1, D_MODEL), const),
            pl.BlockSpec((D_MODEL, tf), lambda i, c: (0, c)),
            pl.BlockSpec((D_MODEL, tf), lambda i, c: (0, c)),
            pl.BlockSpec((tf, D_MODEL), lambda i, c: (c, 0)),
            pl.BlockSpec((1, D_MODEL), const),
        ],
        out_specs=pl.BlockSpec((tm, D_MODEL), rows),
        out_shape=jax.ShapeDtypeStruct((r, D_MODEL), F32),
        scratch_shapes=[pltpu.VMEM((tm, D_MODEL), BF16)],
        compiler_params=pltpu.CompilerParams(
            dimension_semantics=("arbitrary", "arbitrary"), vmem_limit_bytes=VMEM_LIMIT),
        name=name,
    )(x, yc, ya, w_out, g2, w_gate, w_up, w_down, gf)


def _rope_tables(pos):
    inv = ROPE_THETA ** (-jnp.arange(HALF_HEAD, dtype=F32) / HALF_HEAD)
    ang = pos.astype(F32)[:, None] * inv[None, :]
    cos = jnp.cos(ang)
    sin = jnp.sin(ang)
    reps = LANES // HEAD_DIM
    return (jnp.tile(jnp.concatenate([cos, cos], axis=-1), (1, reps)),
            jnp.tile(jnp.concatenate([-sin, sin], axis=-1), (1, reps)))


def kernel(x_prompt, x_sample, cache_k, cache_v, state_conv, meta_tokens, norm1_g, w_in, conv_w, conv_norm_g,
           attn_norm_g, attn_sinks, w_out, norm2_g, w_gate, w_up, w_down, final_norm_g):
    depth = w_in.shape[0]
    assert depth == 1, "single-layer step only"
    nb, seq, _ = x_prompt.shape
    ns, dec_seq, _ = x_sample.shape
    assert dec_seq == 1

    g1 = norm1_g[0][None]
    g2 = norm2_g[0][None]
    gf = final_norm_g[None]
    cg = conv_norm_g[0][None]
    ag = attn_norm_g[0][None]
    cw = conv_w[0]
    sinks = attn_sinks[0]
    w_in_b = w_in[0].astype(BF16)
    w_out_b = w_out[0].astype(BF16)
    w_gate_b = w_gate[0].astype(BF16)
    w_up_b = w_up[0].astype(BF16)
    w_down_b = w_down[0].astype(BF16)

    cos_m, sin_m = _rope_tables(jnp.arange(N_META))
    zero_prev = jnp.zeros((CONV_W - 1, CONV_DIM), F32)
    _, _, kd_m, vd_m, _, _, u_m = _proj_call(meta_tokens[None], g1, w_in_b, cw, cg, cos_m, sin_m, zero_prev,
                                             tm=N_META, sample=False)
    pad = ((WINDOW - N_META, 0), (0, 0))
    kd_meta = jnp.pad(kd_m[0], pad)
    vd_meta = jnp.pad(vd_m[0], pad)

    cos_p, sin_p = _rope_tables(N_META + jnp.arange(seq))
    yc, q, kd, vd, k_tail, v_tail, u_tail = _proj_call(x_prompt, g1, w_in_b, cw, cg, cos_p, sin_p, u_m[0],
                                                       tm=512, sample=False)
    ya = _attn_prompt(q, kd, vd, kd_meta, vd_meta, sinks, ag, n_sub=8)
    rp = nb * seq
    y_prompt = _mlp_call(x_prompt.reshape(rp, D_MODEL), yc.reshape(rp, CONV_DIM), ya.reshape(rp, ATTN_DIM),
                         w_out_b, g2, w_gate_b, w_up_b, w_down_b, gf, tm=512, tf=512, name="mlp_prompt")
    y_prompt = y_prompt.reshape(nb, seq, D_MODEL)
    new_k_prompt = k_tail.reshape(1, nb, WINDOW, N_KV_HEADS, HEAD_DIM)
    new_v_prompt = v_tail.reshape(1, nb, WINDOW, N_KV_HEADS, HEAD_DIM)
    new_conv_prompt = u_tail[None]

    xs = x_sample.reshape(1, ns, D_MODEL)
    cos_s, sin_s = _rope_tables(jnp.full((ns,), PAST_LEN))
    prev_s = state_conv[0].reshape(ns, (CONV_W - 1) * CONV_DIM)
    yc_s, q_s, k_st, v_st, k_s, v_s, u_s = _proj_call(xs, g1, w_in_b, cw, cg, cos_s, sin_s, prev_s, tm=ns, sample=True)
    cache_kt = jnp.transpose(cache_k[0], (0, 2, 3, 1)).reshape(ns, KV_DIM, WINDOW)
    cache_vt = jnp.transpose(cache_v[0], (0, 2, 3, 1)).reshape(ns, KV_DIM, WINDOW)
    ya_s, nk_s, nv_s = _attn_sample(q_s[0], k_s[0], v_s[0], k_st[0], v_st[0], cache_kt, cache_vt, sinks, ag, nseq=16)
    y_sample = _mlp_call(xs[0], yc_s[0], ya_s, w_out_b, g2, w_gate_b, w_up_b, w_down_b, gf,
                         tm=ns, tf=512, name="mlp_sample")
    y_sample = y_sample.reshape(ns, 1, D_MODEL)
    new_k_sample = jnp.transpose(nk_s.reshape(ns, N_KV_HEADS, HEAD_DIM, WINDOW), (0, 3, 1, 2))[None]
    new_v_sample = jnp.transpose(nv_s.reshape(ns, N_KV_HEADS, HEAD_DIM, WINDOW), (0, 3, 1, 2))[None]
    new_conv_sample = u_s.reshape(1, ns, CONV_W - 1, CONV_DIM)

    return (y_prompt, y_sample, new_k_prompt, new_v_prompt, new_conv_prompt,
            new_k_sample, new_v_sample, new_conv_sample)
```

```python
import functools

import jax
import jax.numpy as jnp
from jax import lax
from jax.experimental import pallas as pl
from jax.experimental.pallas import tpu as pltpu

D_MODEL = 2048
N_META = 16
CONV_DIM = 1024
CONV_W = 3
HEAD_DIM = 64
HALF_HEAD = HEAD_DIM // 2
ATTN_DIM = 1024
N_HEADS = 16
N_KV_HEADS = 4
GROUP = N_HEADS // N_KV_HEADS
KV_DIM = N_KV_HEADS * HEAD_DIM
WINDOW = 128
ROPE_THETA = 10000.0
RMS_EPS = 1e-6
NEG_INF = -1e30
PAST_LEN = 16384
LOG2_E = 1.4426950408889634

COL_BG = 0
COL_CG = CONV_DIM
COL_XC = 2 * CONV_DIM
COL_Q = 3 * CONV_DIM
COL_K = COL_Q + ATTN_DIM
COL_V = COL_K + KV_DIM
IN_COLS = COL_V + KV_DIM

LANES = 128
HEADS_PER_TILE = LANES // HEAD_DIM
VMEM_LIMIT = 60 * 1024 * 1024

F32 = jnp.float32
BF16 = jnp.bfloat16


def _rms(x, g):
    return x * lax.rsqrt(jnp.mean(x * x, axis=-1, keepdims=True) + RMS_EPS) * g


def _lane_tile(t, width):
    return jnp.concatenate([t] * (width // t.shape[-1]), axis=-1)


def _rope(t, cos, sin_signed):
    w = t.shape[-1]
    lane = lax.broadcasted_iota(jnp.int32, t.shape, 1)
    first_half = (lane & (HEAD_DIM - 1)) < HALF_HEAD
    swapped = jnp.where(first_half, pltpu.roll(t, w - HALF_HEAD, axis=1), pltpu.roll(t, HALF_HEAD, axis=1))
    return t * _lane_tile(cos, w) + swapped * _lane_tile(sin_signed, w)


def _dup_heads(pair_tile, half):
    lane = lax.broadcasted_iota(jnp.int32, pair_tile.shape, 1)
    swapped = pltpu.roll(pair_tile, HEAD_DIM, axis=1)
    if half == 0:
        return jnp.where(lane < HEAD_DIM, pair_tile, swapped)
    return jnp.where(lane < HEAD_DIM, swapped, pair_tile)


def _dup_all(t):
    tiles = []
    for g in range(N_KV_HEADS):
        pair, half = divmod(g, HEADS_PER_TILE)
        tiles.append(_dup_heads(t[:, pair * LANES:(pair + 1) * LANES], half))
    return jnp.concatenate(tiles, axis=-1).astype(BF16)


def _spread_heads(q):
    lane = lax.broadcasted_iota(jnp.int32, (q.shape[0], LANES), 1)
    tiles = []
    for h in range(N_HEADS):
        tile, half = divmod(h, HEADS_PER_TILE)
        qt = q[:, tile * LANES:(tile + 1) * LANES]
        keep = (lane < HEAD_DIM) if half == 0 else (lane >= HEAD_DIM)
        tiles.append(jnp.where(keep, qt, jnp.zeros_like(qt)))
    return jnp.concatenate(tiles, axis=-1)


def _head_block(t, h):
    return t[:, h * HEAD_DIM:(h + 1) * HEAD_DIM]


def _group_minor(t):
    return jnp.concatenate([_head_block(t, g * GROUP + i) for i in range(GROUP) for g in range(N_KV_HEADS)], axis=-1)


def _group_major(t):
    return jnp.concatenate([_head_block(t, i * N_KV_HEADS + g) for g in range(N_KV_HEADS) for i in range(GROUP)],
                           axis=-1)


def _proj_kernel(sample, x_ref, g1_ref, w_ref, cw_ref, cg_ref, cos_ref, sin_ref, prev_ref,
                 yc_ref, q_ref, kd_ref, vd_ref, kt_ref, vt_ref, u_ref, carry_ref):
    tm = x_ref.shape[1]
    hn = _rms(x_ref[0], g1_ref[...]).astype(BF16)

    def section(lo, width):
        return jnp.dot(hn, w_ref[:, lo:lo + width], preferred_element_type=F32)

    u = section(COL_CG, CONV_DIM) * section(COL_XC, CONV_DIM)
    cw = cw_ref[...]
    if sample:
        u2 = prev_ref[:, :CONV_DIM]
        u1 = prev_ref[:, CONV_DIM:]
        u_ref[0] = jnp.concatenate([u1, u], axis=-1)
    else:
        @pl.when(pl.program_id(1) == 0)
        def _():
            carry_ref[...] = prev_ref[...]

        row = lax.broadcasted_iota(jnp.int32, u.shape, 0)
        p2 = carry_ref[0:1, :]
        p1 = carry_ref[1:2, :]
        u1 = jnp.where(row == 0, p1, pltpu.roll(u, 1, axis=0))
        u2 = jnp.where(row == 0, p2, jnp.where(row == 1, p1, pltpu.roll(u, 2, axis=0)))
        tail = u[tm - (CONV_W - 1):, :]
        carry_ref[...] = tail
        u_ref[0] = tail
    cy = cw[0:1, :] * u2 + cw[1:2, :] * u1 + cw[2:3, :] * u
    yc_ref[0] = _rms(section(COL_BG, CONV_DIM) * cy, cg_ref[...]).astype(BF16)

    cos = cos_ref[...]
    sin = sin_ref[...]
    q = _rope(section(COL_Q, ATTN_DIM), cos, sin) * (HEAD_DIM ** -0.5 * LOG2_E)
    k = _rope(section(COL_K, KV_DIM), cos, sin)
    v = section(COL_V, KV_DIM)
    if sample:
        q_ref[0] = _group_minor(q).astype(BF16)
        kd_ref[0] = k.T
        vd_ref[0] = v.T
    else:
        q_ref[0] = _spread_heads(q.astype(BF16))
        kd_ref[0] = _dup_all(k)
        vd_ref[0] = _dup_all(v)
    tail_rows = kt_ref.shape[1]
    kt_ref[0] = k[tm - tail_rows:, :]
    vt_ref[0] = v[tm - tail_rows:, :]


def _proj_call(x, g1, w_in, conv_w, conv_g, cos, sin, prev, *, tm, sample):
    nb, s, _ = x.shape
    prev_rows, prev_cols = prev.shape[-2:]
    u_rows, u_cols = (tm, 2 * CONV_DIM) if sample else (CONV_W - 1, CONV_DIM)
    tail_rows = min(tm, WINDOW)
    const = lambda b, i: (0, 0)
    rows = lambda b, i: (b, i, 0)
    per_seq = lambda b, i: (b, 0, 0)
    if sample:
        prev_spec = pl.BlockSpec((tm, prev_cols), lambda b, i: (i, 0))
        u_spec = pl.BlockSpec((1, u_rows, u_cols), rows)
        u_shape = (nb, s, u_cols)
        q_cols = ATTN_DIM
        kv_spec = pl.BlockSpec((1, KV_DIM, tm), lambda b, i: (b, 0, i))
        kv_shape = jax.ShapeDtypeStruct((nb, KV_DIM, s), F32)
    else:
        prev_spec = pl.BlockSpec((prev_rows, prev_cols), const)
        u_spec = pl.BlockSpec((1, u_rows, u_cols), per_seq)
        u_shape = (nb, u_rows, u_cols)
        q_cols = N_HEADS * LANES
        kv_spec = pl.BlockSpec((1, tm, N_KV_HEADS * LANES), rows)
        kv_shape = jax.ShapeDtypeStruct((nb, s, N_KV_HEADS * LANES), BF16)
    return pl.pallas_call(
        functools.partial(_proj_kernel, sample),
        grid=(nb, s // tm),
        in_specs=[
            pl.BlockSpec((1, tm, D_MODEL), rows),
            pl.BlockSpec((1, D_MODEL), const),
            pl.BlockSpec((D_MODEL, IN_COLS), const, pipeline_mode=pl.Buffered(1)),
            pl.BlockSpec((CONV_W, CONV_DIM), const),
            pl.BlockSpec((1, CONV_DIM), const),
            pl.BlockSpec((tm, LANES), lambda b, i: (i, 0)),
            pl.BlockSpec((tm, LANES), lambda b, i: (i, 0)),
            prev_spec,
        ],
        out_specs=[
            pl.BlockSpec((1, tm, CONV_DIM), rows),
            pl.BlockSpec((1, tm, q_cols), rows),
            kv_spec,
            kv_spec,
            pl.BlockSpec((1, tail_rows, KV_DIM), per_seq),
            pl.BlockSpec((1, tail_rows, KV_DIM), per_seq),
            u_spec,
        ],
        out_shape=[
            jax.ShapeDtypeStruct((nb, s, CONV_DIM), BF16),
            jax.ShapeDtypeStruct((nb, s, q_cols), BF16),
            kv_shape,
            kv_shape,
            jax.ShapeDtypeStruct((nb, tail_rows, KV_DIM), F32),
            jax.ShapeDtypeStruct((nb, tail_rows, KV_DIM), F32),
            jax.ShapeDtypeStruct(u_shape, F32),
        ],
        scratch_shapes=[pltpu.VMEM((CONV_W - 1, CONV_DIM), F32)],
        compiler_params=pltpu.CompilerParams(
            dimension_semantics=("arbitrary", "arbitrary"), vmem_limit_bytes=VMEM_LIMIT),
        name="proj_sample" if sample else "proj_prompt",
    )(x, g1, w_in, conv_w, conv_g, cos, sin, prev)


def _attn_prompt_kernel(sinks_ref, q_ref, kc_ref, kp_ref, vc_ref, vp_ref, km_ref, vm_ref, ag_ref,
                        ya_ref, o_scr):
    first = pl.program_id(1) == 0
    blk = WINDOW
    nk = 2 * blk
    n_sub = q_ref.shape[1] // blk

    qi = lax.broadcasted_iota(jnp.int32, (blk, nk), 0)
    kj = lax.broadcasted_iota(jnp.int32, (blk, nk), 1)
    band = (kj >= qi) & (kj <= qi + WINDOW)
    band_first = band & (kj >= jnp.where(first, blk - N_META, 0))

    lane = lax.broadcasted_iota(jnp.int32, (blk, LANES), 1)
    ones = jnp.ones((nk, LANES), BF16)
    for t in range(n_sub):
        rows = slice(t * blk, (t + 1) * blk)
        valid = band_first if t == 0 else band
        for g in range(N_KV_HEADS):
            cols = slice(g * LANES, (g + 1) * LANES)
            if t == 0:
                k_prev = jnp.where(first, km_ref[:, cols], kp_ref[0, :, cols])
                v_prev = jnp.where(first, vm_ref[:, cols], vp_ref[0, :, cols])
            else:
                k_prev = kc_ref[0, (t - 1) * blk:t * blk, cols]
                v_prev = vc_ref[0, (t - 1) * blk:t * blk, cols]
            kd = jnp.concatenate([k_prev, kc_ref[0, rows, cols]], axis=0)
            vd = jnp.concatenate([v_prev, vc_ref[0, rows, cols]], axis=0)
            v_rhs = jnp.concatenate([vd, ones], axis=1)
            for tile in range(g * GROUP // HEADS_PER_TILE, (g + 1) * GROUP // HEADS_PER_TILE):
                ov, m = [], []
                for h in (HEADS_PER_TILE * tile, HEADS_PER_TILE * tile + 1):
                    qm = q_ref[0, rows, h * LANES:(h + 1) * LANES]
                    s = lax.dot_general(qm, kd, (((1,), (1,)), ((), ())), preferred_element_type=F32)
                    s = jnp.where(valid, s, NEG_INF)
                    m.append(jnp.max(s, axis=-1, keepdims=True))
                    p = jnp.exp2(s - m[-1]).astype(BF16)
                    ov.append(jnp.dot(p, v_rhs, preferred_element_type=F32))
                low = lane < HEAD_DIM
                sink = jnp.where(low[:1], sinks_ref[HEADS_PER_TILE * tile], sinks_ref[HEADS_PER_TILE * tile + 1])
                denom = (jnp.where(low, ov[0][:, LANES:], ov[1][:, LANES:])
                         + jnp.exp2(sink * LOG2_E - jnp.where(low, m[0], m[1])))
                o_scr[rows, tile * LANES:(tile + 1) * LANES] = (
                    jnp.where(low, ov[0][:, :LANES], ov[1][:, :LANES]) / denom)
        ya_ref[0, rows, :] = _rms(o_scr[rows, :], ag_ref[...]).astype(BF16)


def _attn_prompt(q, kd, vd, kd_meta, vd_meta, sinks, attn_g, *, n_sub):
    nb, s, kd_cols = kd.shape
    blk = WINDOW
    tq = n_sub * blk
    cur = lambda b, j: (b, j, 0)
    prv = lambda b, j: (b, jnp.maximum(j * n_sub - 1, 0), 0)
    const = lambda b, j: (0, 0)
    return pl.pallas_call(
        _attn_prompt_kernel,
        grid=(nb, s // tq),
        in_specs=[
            pl.BlockSpec(memory_space=pltpu.SMEM),
            pl.BlockSpec((1, tq, N_HEADS * LANES), cur),
            pl.BlockSpec((1, tq, kd_cols), cur),
            pl.BlockSpec((1, blk, kd_cols), prv),
            pl.BlockSpec((1, tq, kd_cols), cur),
            pl.BlockSpec((1, blk, kd_cols), prv),
            pl.BlockSpec((blk, kd_cols), const),
            pl.BlockSpec((blk, kd_cols), const),
            pl.BlockSpec((1, ATTN_DIM), const),
        ],
        out_specs=pl.BlockSpec((1, tq, ATTN_DIM), cur),
        out_shape=jax.ShapeDtypeStruct((nb, s, ATTN_DIM), BF16),
        scratch_shapes=[pltpu.VMEM((tq, ATTN_DIM), F32)],
        compiler_params=pltpu.CompilerParams(
            dimension_semantics=("arbitrary", "arbitrary"), vmem_limit_bytes=VMEM_LIMIT),
        name="attn_prompt",
    )(sinks, q, kd, kd, vd, vd, kd_meta, vd_meta, attn_g)


def _attn_sample_kernel(sinks_ref, q_ref, kn_ref, vn_ref, knt_ref, vnt_ref, ck_ref, cv_ref, ag_ref,
                        ya_ref, nk_ref, nv_ref, o_scr):
    nseq = q_ref.shape[0]
    base = pl.program_id(0) * nseq
    row = lax.broadcasted_iota(jnp.int32, (N_HEADS, KV_DIM), 0)
    lane = lax.broadcasted_iota(jnp.int32, (N_HEADS, KV_DIM), 1)
    own = (lane // HEAD_DIM) == (row % N_KV_HEADS)
    sink = jnp.concatenate(
        [jnp.full((1, 1), sinks_ref[(r % N_KV_HEADS) * GROUP + r // N_KV_HEADS] * LOG2_E, F32)
         for r in range(N_HEADS)], axis=0)
    key_lane = lax.broadcasted_iota(jnp.int32, (KV_DIM, WINDOW), 1)

    for n in range(nseq):
        kt = ck_ref[n]
        vt = cv_ref[n]
        kn = kn_ref[n:n + 1, :]
        vn = vn_ref[n:n + 1, :]
        qrow = q_ref[n:n + 1, :].astype(F32)
        q16 = jnp.broadcast_to(qrow[:, (GROUP - 1) * KV_DIM:], (N_HEADS, KV_DIM))
        for i in range(GROUP - 2, -1, -1):
            q16 = jnp.where(row < (i + 1) * N_KV_HEADS, qrow[:, i * KV_DIM:(i + 1) * KV_DIM], q16)
        q16 = jnp.where(own, q16, 0.0)
        s_c = jnp.dot(q16.astype(BF16), kt.astype(BF16), preferred_element_type=F32)
        s_n = jnp.sum(q16 * kn, axis=-1, keepdims=True)
        m = jnp.maximum(jnp.maximum(jnp.max(s_c, axis=-1, keepdims=True), s_n), sink)
        p_c = jnp.exp2(s_c - m)
        p_n = jnp.exp2(s_n - m)
        denom = jnp.sum(p_c, axis=-1, keepdims=True) + p_n + jnp.exp2(sink - m)
        o16 = lax.dot_general(p_c.astype(BF16), vt.astype(BF16), (((1,), (1,)), ((), ())),
                              preferred_element_type=F32)
        o16 = jnp.where(own, (o16 + p_n * vn) / denom, 0.0)
        for i in range(GROUP):
            in_group = (row >= i * N_KV_HEADS) & (row < (i + 1) * N_KV_HEADS)
            o_scr[n:n + 1, i * KV_DIM:(i + 1) * KV_DIM] = jnp.sum(jnp.where(in_group, o16, 0.0), axis=0, keepdims=True)
        is_seq = key_lane == base + n
        k_col = jnp.sum(jnp.where(is_seq, knt_ref[...], 0.0), axis=1, keepdims=True)
        v_col = jnp.sum(jnp.where(is_seq, vnt_ref[...], 0.0), axis=1, keepdims=True)
        nk_ref[n] = jnp.where(key_lane == WINDOW - 1, k_col, pltpu.roll(kt, WINDOW - 1, axis=1))
        nv_ref[n] = jnp.where(key_lane == WINDOW - 1, v_col, pltpu.roll(vt, WINDOW - 1, axis=1))
    ya_ref[...] = _rms(_group_major(o_scr[...]), ag_ref[...]).astype(BF16)


def _attn_sample(q, k_new, v_new, k_new_t, v_new_t, cache_kt, cache_vt, sinks, attn_g, *, nseq):
    n = q.shape[0]
    rows = lambda i: (i, 0)
    seqs = lambda i: (i, 0, 0)
    const = lambda i: (0, 0)
    return pl.pallas_call(
        _attn_sample_kernel,
        grid=(n // nseq,),
        in_specs=[
            pl.BlockSpec(memory_space=pltpu.SMEM),
            pl.BlockSpec((nseq, ATTN_DIM), rows),
            pl.BlockSpec((nseq, KV_DIM), rows),
            pl.BlockSpec((nseq, KV_DIM), rows),
            pl.BlockSpec((KV_DIM, n), const),
            pl.BlockSpec((KV_DIM, n), const),
            pl.BlockSpec((nseq, KV_DIM, WINDOW), seqs),
            pl.BlockSpec((nseq, KV_DIM, WINDOW), seqs),
            pl.BlockSpec((1, ATTN_DIM), const),
        ],
        out_specs=[
            pl.BlockSpec((nseq, ATTN_DIM), rows),
            pl.BlockSpec((nseq, KV_DIM, WINDOW), seqs),
            pl.BlockSpec((nseq, KV_DIM, WINDOW), seqs),
        ],
        out_shape=[
            jax.ShapeDtypeStruct((n, ATTN_DIM), BF16),
            jax.ShapeDtypeStruct((n, KV_DIM, WINDOW), F32),
            jax.ShapeDtypeStruct((n, KV_DIM, WINDOW), F32),
        ],
        scratch_shapes=[pltpu.VMEM((nseq, ATTN_DIM), F32)],
        compiler_params=pltpu.CompilerParams(
            dimension_semantics=("arbitrary",), vmem_limit_bytes=VMEM_LIMIT),
        name="attn_sample",
    )(sinks, q, k_new, v_new, k_new_t, v_new_t, cache_kt, cache_vt, attn_g)


def _mlp_kernel(chunks, x_ref, yc_ref, ya_ref, wo_ref, g2_ref, wg_hbm, wu_hbm, wd_hbm, gf_ref,
                y_ref, hn_scr, wg_buf, wu_buf, wd_buf, sem):
    i = pl.program_id(0)
    n_tiles = pl.num_programs(0)
    n_slots = wg_buf.shape[0]
    n_chunks = len(chunks)

    def chunk_copies(c):
        slot = c % n_slots
        off, width = chunks[c]
        cols = pl.ds(off, width)
        return (pltpu.make_async_copy(wg_hbm.at[:, cols], wg_buf.at[slot, :, :width], sem.at[0, slot]),
                pltpu.make_async_copy(wu_hbm.at[:, cols], wu_buf.at[slot, :, :width], sem.at[1, slot]),
                pltpu.make_async_copy(wd_hbm.at[cols, :], wd_buf.at[slot, :width, :], sem.at[2, slot]))

    def start(c):
        for cp in chunk_copies(c):
            cp.start()

    def wait(c):
        for cp in chunk_copies(c):
            cp.wait()

    @pl.when(i == 0)
    def _():
        start(0)

    mix = (jnp.dot(yc_ref[...], wo_ref[:CONV_DIM, :], preferred_element_type=F32)
           + jnp.dot(ya_ref[...], wo_ref[CONV_DIM:, :], preferred_element_type=F32))
    h = x_ref[...] + mix
    y_ref[...] = h
    hn_scr[...] = _rms(h, g2_ref[...]).astype(BF16)

    for c in range(n_chunks):
        if c + 1 < n_chunks:
            start(c + 1)
        else:
            @pl.when(i + 1 < n_tiles)
            def _():
                start(0)
        wait(c)
        slot = c % n_slots
        width = chunks[c][1]
        hn = hn_scr[...]
        gate = jnp.dot(hn, wg_buf[slot, :, :width], preferred_element_type=F32)
        up = jnp.dot(hn, wu_buf[slot, :, :width], preferred_element_type=F32)
        act = (gate * jax.nn.sigmoid(gate) * up).astype(BF16)
        y_ref[...] += jnp.dot(act, wd_buf[slot, :width, :], preferred_element_type=F32)

    y_ref[...] = _rms(y_ref[...], gf_ref[...])


MLP_WEIGHT_SLOTS = 2


def _mlp_chunks(d_ff, tf):
    bounds = list(range(0, d_ff, tf)) + [d_ff]
    return tuple((lo, hi - lo) for lo, hi in zip(bounds[:-1], bounds[1:]))


def _mlp_call(x, yc, ya, w_out, g2, w_gate, w_up, w_down, gf, *, tm, tf, name):
    r = x.shape[0]
    chunks = _mlp_chunks(w_gate.shape[1], tf)
    assert len(chunks) % MLP_WEIGHT_SLOTS == 0 and all(w % LANES == 0 for _, w in chunks)
    rows = lambda i: (i, 0)
    const = lambda i: (0, 0)
    return pl.pallas_call(
        functools.partial(_mlp_kernel, chunks),
        grid=(r // tm,),
        in_specs=[
            pl.BlockSpec((tm, D_MODEL), rows),
            pl.BlockSpec((tm, CONV_DIM), rows),
            pl.BlockSpec((tm, ATTN_DIM), rows),
            pl.BlockSpec((D_MODEL, D_MODEL), const, pipeline_mode=pl.Buffered(1)),
            pl.BlockSpec((1, D_MODEL), const),
            pl.BlockSpec(memory_space=pl.ANY),
            pl.BlockSpec(memory_space=pl.ANY),
            pl.BlockSpec(memory_space=pl.ANY),
            pl.BlockSpec((1, D_MODEL), const),
        ],
        out_specs=pl.BlockSpec((tm, D_MODEL), rows),
        out_shape=jax.ShapeDtypeStruct((r, D_MODEL), F32),
        scratch_shapes=[
            pltpu.VMEM((tm, D_MODEL), BF16),
            pltpu.VMEM((MLP_WEIGHT_SLOTS, D_MODEL, tf), BF16),
            pltpu.VMEM((MLP_WEIGHT_SLOTS, D_MODEL, tf), BF16),
            pltpu.VMEM((MLP_WEIGHT_SLOTS, tf, D_MODEL), BF16),
            pltpu.SemaphoreType.DMA((3, MLP_WEIGHT_SLOTS)),
        ],
        compiler_params=pltpu.CompilerParams(
            dimension_semantics=("arbitrary",), vmem_limit_bytes=VMEM_LIMIT),
        name=name,
    )(x, yc, ya, w_out, g2, w_gate, w_up, w_down, gf)


def _rope_tables(pos):
    inv = ROPE_THETA ** (-jnp.arange(HALF_HEAD, dtype=F32) / HALF_HEAD)
    ang = pos.astype(F32)[:, None] * inv[None, :]
    cos = jnp.cos(ang)
    sin = jnp.sin(ang)
    reps = LANES // HEAD_DIM
    return (jnp.tile(jnp.concatenate([cos, cos], axis=-1), (1, reps)),
            jnp.tile(jnp.concatenate([-sin, sin], axis=-1), (1, reps)))


def kernel(x_prompt, x_sample, cache_k, cache_v, state_conv, meta_tokens, norm1_g, w_in, conv_w, conv_norm_g,
           attn_norm_g, attn_sinks, w_out, norm2_g, w_gate, w_up, w_down, final_norm_g):
    depth = w_in.shape[0]
    assert depth == 1, "single-layer step only"
    nb, seq, _ = x_prompt.shape
    ns, dec_seq, _ = x_sample.shape
    assert dec_seq == 1

    g1 = norm1_g[0][None]
    g2 = norm2_g[0][None]
    gf = final_norm_g[None]
    cg = conv_norm_g[0][None]
    ag = attn_norm_g[0][None]
    cw = conv_w[0]
    sinks = attn_sinks[0]
    w_in_b = w_in[0].astype(BF16)
    w_out_b = w_out[0].astype(BF16)
    w_gate_b = w_gate[0].astype(BF16)
    w_up_b = w_up[0].astype(BF16)
    w_down_b = w_down[0].astype(BF16)

    cos_m, sin_m = _rope_tables(jnp.arange(N_META))
    zero_prev = jnp.zeros((CONV_W - 1, CONV_DIM), F32)
    _, _, kd_m, vd_m, _, _, u_m = _proj_call(meta_tokens[None], g1, w_in_b, cw, cg, cos_m, sin_m, zero_prev,
                                             tm=N_META, sample=False)
    pad = ((WINDOW - N_META, 0), (0, 0))
    kd_meta = jnp.pad(kd_m[0], pad)
    vd_meta = jnp.pad(vd_m[0], pad)

    cos_p, sin_p = _rope_tables(N_META + jnp.arange(seq))
    yc, q, kd, vd, k_tail, v_tail, u_tail = _proj_call(x_prompt, g1, w_in_b, cw, cg, cos_p, sin_p, u_m[0],
                                                       tm=512, sample=False)
    ya = _attn_prompt(q, kd, vd, kd_meta, vd_meta, sinks, ag, n_sub=8)
    rp = nb * seq
    y_prompt = _mlp_call(x_prompt.reshape(rp, D_MODEL), yc.reshape(rp, CONV_DIM), ya.reshape(rp, ATTN_DIM),
                         w_out_b, g2, w_gate_b, w_up_b, w_down_b, gf, tm=512, tf=1024, name="mlp_prompt")
    y_prompt = y_prompt.reshape(nb, seq, D_MODEL)
    new_k_prompt = k_tail.reshape(1, nb, WINDOW, N_KV_HEADS, HEAD_DIM)
    new_v_prompt = v_tail.reshape(1, nb, WINDOW, N_KV_HEADS, HEAD_DIM)
    new_conv_prompt = u_tail[None]

    xs = x_sample.reshape(1, ns, D_MODEL)
    cos_s, sin_s = _rope_tables(jnp.full((ns,), PAST_LEN))
    prev_s = state_conv[0].reshape(ns, (CONV_W - 1) * CONV_DIM)
    yc_s, q_s, k_st, v_st, k_s, v_s, u_s = _proj_call(xs, g1, w_in_b, cw, cg, cos_s, sin_s, prev_s, tm=ns, sample=True)
    cache_kt = jnp.transpose(cache_k[0], (0, 2, 3, 1)).reshape(ns, KV_DIM, WINDOW)
    cache_vt = jnp.transpose(cache_v[0], (0, 2, 3, 1)).reshape(ns, KV_DIM, WINDOW)
    ya_s, nk_s, nv_s = _attn_sample(q_s[0], k_s[0], v_s[0], k_st[0], v_st[0], cache_kt, cache_vt, sinks, ag, nseq=16)
    y_sample = _mlp_call(xs[0], yc_s[0], ya_s, w_out_b, g2, w_gate_b, w_up_b, w_down_b, gf,
                         tm=ns, tf=1024, name="mlp_sample")
    y_sample = y_sample.reshape(ns, 1, D_MODEL)
    new_k_sample = jnp.transpose(nk_s.reshape(ns, N_KV_HEADS, HEAD_DIM, WINDOW), (0, 3, 1, 2))[None]
    new_v_sample = jnp.transpose(nv_s.reshape(ns, N_KV_HEADS, HEAD_DIM, WINDOW), (0, 3, 1, 2))[None]
    new_conv_sample = u_s.reshape(1, ns, CONV_W - 1, CONV_DIM)

    return (y_prompt, y_sample, new_k_prompt, new_v_prompt, new_conv_prompt,
            new_k_sample, new_v_sample, new_conv_sample)
```

```python
import functools

import jax
import jax.numpy as jnp
from jax import lax
from jax.experimental import pallas as pl
from jax.experimental.pallas import tpu as pltpu

D_MODEL = 2048
N_META = 16
CONV_DIM = 1024
CONV_W = 3
HEAD_DIM = 64
HALF_HEAD = HEAD_DIM // 2
ATTN_DIM = 1024
N_HEADS = 16
N_KV_HEADS = 4
GROUP = N_HEADS // N_KV_HEADS
KV_DIM = N_KV_HEADS * HEAD_DIM
WINDOW = 128
ROPE_THETA = 10000.0
RMS_EPS = 1e-6
NEG_INF = -1e30
PAST_LEN = 16384
LOG2_E = 1.4426950408889634

COL_BG = 0
COL_CG = CONV_DIM
COL_XC = 2 * CONV_DIM
COL_Q = 3 * CONV_DIM
COL_K = COL_Q + ATTN_DIM
COL_V = COL_K + KV_DIM
IN_COLS = COL_V + KV_DIM

LANES = 128
HEADS_PER_TILE = LANES // HEAD_DIM
VMEM_LIMIT = 60 * 1024 * 1024

F32 = jnp.float32
BF16 = jnp.bfloat16


def _rms(x, g):
    return x * lax.rsqrt(jnp.mean(x * x, axis=-1, keepdims=True) + RMS_EPS) * g


def _lane_tile(t, width):
    return jnp.concatenate([t] * (width // t.shape[-1]), axis=-1)


def _rope(t, cos, sin_signed):
    w = t.shape[-1]
    lane = lax.broadcasted_iota(jnp.int32, t.shape, 1)
    first_half = (lane & (HEAD_DIM - 1)) < HALF_HEAD
    swapped = jnp.where(first_half, pltpu.roll(t, w - HALF_HEAD, axis=1), pltpu.roll(t, HALF_HEAD, axis=1))
    return t * _lane_tile(cos, w) + swapped * _lane_tile(sin_signed, w)


def _dup_heads(pair_tile, half):
    lane = lax.broadcasted_iota(jnp.int32, pair_tile.shape, 1)
    swapped = pltpu.roll(pair_tile, HEAD_DIM, axis=1)
    if half == 0:
        return jnp.where(lane < HEAD_DIM, pair_tile, swapped)
    return jnp.where(lane < HEAD_DIM, swapped, pair_tile)


def _dup_all(t):
    tiles = []
    for g in range(N_KV_HEADS):
        pair, half = divmod(g, HEADS_PER_TILE)
        tiles.append(_dup_heads(t[:, pair * LANES:(pair + 1) * LANES], half))
    return jnp.concatenate(tiles, axis=-1).astype(BF16)


def _spread_heads(q):
    lane = lax.broadcasted_iota(jnp.int32, (q.shape[0], LANES), 1)
    tiles = []
    for h in range(N_HEADS):
        tile, half = divmod(h, HEADS_PER_TILE)
        qt = q[:, tile * LANES:(tile + 1) * LANES]
        keep = (lane < HEAD_DIM) if half == 0 else (lane >= HEAD_DIM)
        tiles.append(jnp.where(keep, qt, jnp.zeros_like(qt)))
    return jnp.concatenate(tiles, axis=-1)


def _head_block(t, h):
    return t[:, h * HEAD_DIM:(h + 1) * HEAD_DIM]


def _group_minor(t):
    return jnp.concatenate([_head_block(t, g * GROUP + i) for i in range(GROUP) for g in range(N_KV_HEADS)], axis=-1)


def _group_major(t):
    return jnp.concatenate([_head_block(t, i * N_KV_HEADS + g) for g in range(N_KV_HEADS) for i in range(GROUP)],
                           axis=-1)


N_PROJ_IN = 8
N_PROJ_OUT = 7


def _proj_kernel(sample, *refs):
    x_ref, g1_ref, w_ref, cw_ref, cg_ref, cos_ref, sin_ref, prev_ref = refs[:N_PROJ_IN]
    n_cast = (len(refs) - 1 - N_PROJ_IN - N_PROJ_OUT) // 2
    cast_in = refs[N_PROJ_IN:N_PROJ_IN + n_cast]
    yc_ref, q_ref, kd_ref, vd_ref, kt_ref, vt_ref, u_ref = refs[N_PROJ_IN + n_cast:N_PROJ_IN + n_cast + N_PROJ_OUT]
    cast_out = refs[N_PROJ_IN + n_cast + N_PROJ_OUT:-1]
    carry_ref = refs[-1]
    for src, dst in zip(cast_in, cast_out):
        dst[...] = src[...].astype(BF16)

    tm = x_ref.shape[1]
    hn = _rms(x_ref[0], g1_ref[...]).astype(BF16)

    def section(lo, width):
        return jnp.dot(hn, w_ref[:, lo:lo + width], preferred_element_type=F32)

    u = section(COL_CG, CONV_DIM) * section(COL_XC, CONV_DIM)
    cw = cw_ref[...]
    if sample:
        u2 = prev_ref[:, :CONV_DIM]
        u1 = prev_ref[:, CONV_DIM:]
        u_ref[0] = jnp.concatenate([u1, u], axis=-1)
    else:
        @pl.when(pl.program_id(1) == 0)
        def _():
            carry_ref[...] = prev_ref[...]

        row = lax.broadcasted_iota(jnp.int32, u.shape, 0)
        p2 = carry_ref[0:1, :]
        p1 = carry_ref[1:2, :]
        u1 = jnp.where(row == 0, p1, pltpu.roll(u, 1, axis=0))
        u2 = jnp.where(row == 0, p2, jnp.where(row == 1, p1, pltpu.roll(u, 2, axis=0)))
        tail = u[tm - (CONV_W - 1):, :]
        carry_ref[...] = tail
        u_ref[0] = tail
    cy = cw[0:1, :] * u2 + cw[1:2, :] * u1 + cw[2:3, :] * u
    yc_ref[0] = _rms(section(COL_BG, CONV_DIM) * cy, cg_ref[...]).astype(BF16)

    cos = cos_ref[...]
    sin = sin_ref[...]
    q = _rope(section(COL_Q, ATTN_DIM), cos, sin) * (HEAD_DIM ** -0.5 * LOG2_E)
    k = _rope(section(COL_K, KV_DIM), cos, sin)
    v = section(COL_V, KV_DIM)
    if sample:
        q_ref[0] = _group_minor(q).astype(BF16)
        kd_ref[0] = k.T
        vd_ref[0] = v.T
    else:
        q_ref[0] = _spread_heads(q.astype(BF16))
        kd_ref[0] = _dup_all(k)
        vd_ref[0] = _dup_all(v)
    tail_rows = kt_ref.shape[1]
    kt_ref[0] = k[tm - tail_rows:, :]
    vt_ref[0] = v[tm - tail_rows:, :]


def _proj_call(x, g1, w_in, conv_w, conv_g, cos, sin, prev, *, tm, sample, cast=()):
    nb, s, _ = x.shape
    n_steps = nb * (s // tm)
    slab = lambda b, i: (b * (s // tm) + i, 0)
    cast_specs = [pl.BlockSpec((w.shape[0] // n_steps, w.shape[1]), slab) for w in cast]
    assert all(w.shape[0] % (n_steps * 16) == 0 for w in cast)
    prev_rows, prev_cols = prev.shape[-2:]
    u_rows, u_cols = (tm, 2 * CONV_DIM) if sample else (CONV_W - 1, CONV_DIM)
    tail_rows = min(tm, WINDOW)
    const = lambda b, i: (0, 0)
    rows = lambda b, i: (b, i, 0)
    per_seq = lambda b, i: (b, 0, 0)
    if sample:
        prev_spec = pl.BlockSpec((tm, prev_cols), lambda b, i: (i, 0))
        u_spec = pl.BlockSpec((1, u_rows, u_cols), rows)
        u_shape = (nb, s, u_cols)
        q_cols = ATTN_DIM
        kv_spec = pl.BlockSpec((1, KV_DIM, tm), lambda b, i: (b, 0, i))
        kv_shape = jax.ShapeDtypeStruct((nb, KV_DIM, s), F32)
    else:
        prev_spec = pl.BlockSpec((prev_rows, prev_cols), const)
        u_spec = pl.BlockSpec((1, u_rows, u_cols), per_seq)
        u_shape = (nb, u_rows, u_cols)
        q_cols = N_HEADS * LANES
        kv_spec = pl.BlockSpec((1, tm, N_KV_HEADS * LANES), rows)
        kv_shape = jax.ShapeDtypeStruct((nb, s, N_KV_HEADS * LANES), BF16)
    return pl.pallas_call(
        functools.partial(_proj_kernel, sample),
        grid=(nb, s // tm),
        in_specs=[
            pl.BlockSpec((1, tm, D_MODEL), rows),
            pl.BlockSpec((1, D_MODEL), const),
            pl.BlockSpec((D_MODEL, IN_COLS), const, pipeline_mode=pl.Buffered(1)),
            pl.BlockSpec((CONV_W, CONV_DIM), const),
            pl.BlockSpec((1, CONV_DIM), const),
            pl.BlockSpec((tm, LANES), lambda b, i: (i, 0)),
            pl.BlockSpec((tm, LANES), lambda b, i: (i, 0)),
            prev_spec,
            *cast_specs,
        ],
        out_specs=[
            pl.BlockSpec((1, tm, CONV_DIM), rows),
            pl.BlockSpec((1, tm, q_cols), rows),
            kv_spec,
            kv_spec,
            pl.BlockSpec((1, tail_rows, KV_DIM), per_seq),
            pl.BlockSpec((1, tail_rows, KV_DIM), per_seq),
            u_spec,
            *cast_specs,
        ],
        out_shape=[
            jax.ShapeDtypeStruct((nb, s, CONV_DIM), BF16),
            jax.ShapeDtypeStruct((nb, s, q_cols), BF16),
            kv_shape,
            kv_shape,
            jax.ShapeDtypeStruct((nb, tail_rows, KV_DIM), F32),
            jax.ShapeDtypeStruct((nb, tail_rows, KV_DIM), F32),
            jax.ShapeDtypeStruct(u_shape, F32),
            *[jax.ShapeDtypeStruct(w.shape, BF16) for w in cast],
        ],
        scratch_shapes=[pltpu.VMEM((CONV_W - 1, CONV_DIM), F32)],
        compiler_params=pltpu.CompilerParams(
            dimension_semantics=("arbitrary", "arbitrary"), vmem_limit_bytes=VMEM_LIMIT),
        name="proj_sample" if sample else "proj_prompt",
    )(x, g1, w_in, conv_w, conv_g, cos, sin, prev, *cast)


def _attn_prompt_kernel(sinks_ref, q_ref, kc_ref, kp_ref, vc_ref, vp_ref, km_ref, vm_ref, ag_ref,
                        ya_ref, o_scr):
    first = pl.program_id(1) == 0
    blk = WINDOW
    nk = 2 * blk
    n_sub = q_ref.shape[1] // blk

    qi = lax.broadcasted_iota(jnp.int32, (blk, nk), 0)
    kj = lax.broadcasted_iota(jnp.int32, (blk, nk), 1)
    band = (kj >= qi) & (kj <= qi + WINDOW)
    band_first = band & (kj >= jnp.where(first, blk - N_META, 0))

    lane = lax.broadcasted_iota(jnp.int32, (blk, LANES), 1)
    ones = jnp.ones((nk, LANES), BF16)
    for t in range(n_sub):
        rows = slice(t * blk, (t + 1) * blk)
        valid = band_first if t == 0 else band
        for g in range(N_KV_HEADS):
            cols = slice(g * LANES, (g + 1) * LANES)
            if t == 0:
                k_prev = jnp.where(first, km_ref[:, cols], kp_ref[0, :, cols])
                v_prev = jnp.where(first, vm_ref[:, cols], vp_ref[0, :, cols])
            else:
                k_prev = kc_ref[0, (t - 1) * blk:t * blk, cols]
                v_prev = vc_ref[0, (t - 1) * blk:t * blk, cols]
            kd = jnp.concatenate([k_prev, kc_ref[0, rows, cols]], axis=0)
            vd = jnp.concatenate([v_prev, vc_ref[0, rows, cols]], axis=0)
            v_rhs = jnp.concatenate([vd, ones], axis=1)
            for tile in range(g * GROUP // HEADS_PER_TILE, (g + 1) * GROUP // HEADS_PER_TILE):
                ov, m = [], []
                for h in (HEADS_PER_TILE * tile, HEADS_PER_TILE * tile + 1):
                    qm = q_ref[0, rows, h * LANES:(h + 1) * LANES]
                    s = lax.dot_general(qm, kd, (((1,), (1,)), ((), ())), preferred_element_type=F32)
                    s = jnp.where(valid, s, NEG_INF)
                    m.append(jnp.max(s, axis=-1, keepdims=True))
                    p = jnp.exp2(s - m[-1]).astype(BF16)
                    ov.append(jnp.dot(p, v_rhs, preferred_element_type=F32))
                low = lane < HEAD_DIM
                sink = jnp.where(low[:1], sinks_ref[HEADS_PER_TILE * tile], sinks_ref[HEADS_PER_TILE * tile + 1])
                denom = (jnp.where(low, ov[0][:, LANES:], ov[1][:, LANES:])
                         + jnp.exp2(sink * LOG2_E - jnp.where(low, m[0], m[1])))
                o_scr[rows, tile * LANES:(tile + 1) * LANES] = (
                    jnp.where(low, ov[0][:, :LANES], ov[1][:, :LANES]) / denom)
        ya_ref[0, rows, :] = _rms(o_scr[rows, :], ag_ref[...]).astype(BF16)


def _attn_prompt(q, kd, vd, kd_meta, vd_meta, sinks, attn_g, *, n_sub):
    nb, s, kd_cols = kd.shape
    blk = WINDOW
    tq = n_sub * blk
    cur = lambda b, j: (b, j, 0)
    prv = lambda b, j: (b, jnp.maximum(j * n_sub - 1, 0), 0)
    const = lambda b, j: (0, 0)
    return pl.pallas_call(
        _attn_prompt_kernel,
        grid=(nb, s // tq),
        in_specs=[
            pl.BlockSpec(memory_space=pltpu.SMEM),
            pl.BlockSpec((1, tq, N_HEADS * LANES), cur),
            pl.BlockSpec((1, tq, kd_cols), cur),
            pl.BlockSpec((1, blk, kd_cols), prv),
            pl.BlockSpec((1, tq, kd_cols), cur),
            pl.BlockSpec((1, blk, kd_cols), prv),
            pl.BlockSpec((blk, kd_cols), const),
            pl.BlockSpec((blk, kd_cols), const),
            pl.BlockSpec((1, ATTN_DIM), const),
        ],
        out_specs=pl.BlockSpec((1, tq, ATTN_DIM), cur),
        out_shape=jax.ShapeDtypeStruct((nb, s, ATTN_DIM), BF16),
        scratch_shapes=[pltpu.VMEM((tq, ATTN_DIM), F32)],
        compiler_params=pltpu.CompilerParams(
            dimension_semantics=("arbitrary", "arbitrary"), vmem_limit_bytes=VMEM_LIMIT),
        name="attn_prompt",
    )(sinks, q, kd, kd, vd, vd, kd_meta, vd_meta, attn_g)


def _attn_sample_kernel(sinks_ref, q_ref, kn_ref, vn_ref, knt_ref, vnt_ref, ck_ref, cv_ref, ag_ref,
                        ya_ref, nk_ref, nv_ref, o_scr):
    nseq = q_ref.shape[0]
    base = pl.program_id(0) * nseq
    row = lax.broadcasted_iota(jnp.int32, (N_HEADS, KV_DIM), 0)
    lane = lax.broadcasted_iota(jnp.int32, (N_HEADS, KV_DIM), 1)
    own = (lane // HEAD_DIM) == (row % N_KV_HEADS)
    sink = jnp.concatenate(
        [jnp.full((1, 1), sinks_ref[(r % N_KV_HEADS) * GROUP + r // N_KV_HEADS] * LOG2_E, F32)
         for r in range(N_HEADS)], axis=0)
    key_lane = lax.broadcasted_iota(jnp.int32, (KV_DIM, WINDOW), 1)

    for n in range(nseq):
        kt = ck_ref[n]
        vt = cv_ref[n]
        kn = kn_ref[n:n + 1, :]
        vn = vn_ref[n:n + 1, :]
        qrow = q_ref[n:n + 1, :].astype(F32)
        q16 = jnp.broadcast_to(qrow[:, (GROUP - 1) * KV_DIM:], (N_HEADS, KV_DIM))
        for i in range(GROUP - 2, -1, -1):
            q16 = jnp.where(row < (i + 1) * N_KV_HEADS, qrow[:, i * KV_DIM:(i + 1) * KV_DIM], q16)
        q16 = jnp.where(own, q16, 0.0)
        s_c = jnp.dot(q16.astype(BF16), kt.astype(BF16), preferred_element_type=F32)
        s_n = jnp.sum(q16 * kn, axis=-1, keepdims=True)
        m = jnp.maximum(jnp.maximum(jnp.max(s_c, axis=-1, keepdims=True), s_n), sink)
        p_c = jnp.exp2(s_c - m)
        p_n = jnp.exp2(s_n - m)
        denom = jnp.sum(p_c, axis=-1, keepdims=True) + p_n + jnp.exp2(sink - m)
        o16 = lax.dot_general(p_c.astype(BF16), vt.astype(BF16), (((1,), (1,)), ((), ())),
                              preferred_element_type=F32)
        o16 = jnp.where(own, (o16 + p_n * vn) / denom, 0.0)
        for i in range(GROUP):
            in_group = (row >= i * N_KV_HEADS) & (row < (i + 1) * N_KV_HEADS)
            o_scr[n:n + 1, i * KV_DIM:(i + 1) * KV_DIM] = jnp.sum(jnp.where(in_group, o16, 0.0), axis=0, keepdims=True)
        is_seq = key_lane == base + n
        k_col = jnp.sum(jnp.where(is_seq, knt_ref[...], 0.0), axis=1, keepdims=True)
        v_col = jnp.sum(jnp.where(is_seq, vnt_ref[...], 0.0), axis=1, keepdims=True)
        nk_ref[n] = jnp.where(key_lane == WINDOW - 1, k_col, pltpu.roll(kt, WINDOW - 1, axis=1))
        nv_ref[n] = jnp.where(key_lane == WINDOW - 1, v_col, pltpu.roll(vt, WINDOW - 1, axis=1))
    ya_ref[...] = _rms(_group_major(o_scr[...]), ag_ref[...]).astype(BF16)


def _attn_sample(q, k_new, v_new, k_new_t, v_new_t, cache_kt, cache_vt, sinks, attn_g, *, nseq):
    n = q.shape[0]
    rows = lambda i: (i, 0)
    seqs = lambda i: (i, 0, 0)
    const = lambda i: (0, 0)
    return pl.pallas_call(
        _attn_sample_kernel,
        grid=(n // nseq,),
        in_specs=[
            pl.BlockSpec(memory_space=pltpu.SMEM),
            pl.BlockSpec((nseq, ATTN_DIM), rows),
            pl.BlockSpec((nseq, KV_DIM), rows),
            pl.BlockSpec((nseq, KV_DIM), rows),
            pl.BlockSpec((KV_DIM, n), const),
            pl.BlockSpec((KV_DIM, n), const),
            pl.BlockSpec((nseq, KV_DIM, WINDOW), seqs),
            pl.BlockSpec((nseq, KV_DIM, WINDOW), seqs),
            pl.BlockSpec((1, ATTN_DIM), const),
        ],
        out_specs=[
            pl.BlockSpec((nseq, ATTN_DIM), rows),
            pl.BlockSpec((nseq, KV_DIM, WINDOW), seqs),
            pl.BlockSpec((nseq, KV_DIM, WINDOW), seqs),
        ],
        out_shape=[
            jax.ShapeDtypeStruct((n, ATTN_DIM), BF16),
            jax.ShapeDtypeStruct((n, KV_DIM, WINDOW), F32),
            jax.ShapeDtypeStruct((n, KV_DIM, WINDOW), F32),
        ],
        scratch_shapes=[pltpu.VMEM((nseq, ATTN_DIM), F32)],
        compiler_params=pltpu.CompilerParams(
            dimension_semantics=("arbitrary",), vmem_limit_bytes=VMEM_LIMIT),
        name="attn_sample",
    )(sinks, q, k_new, v_new, k_new_t, v_new_t, cache_kt, cache_vt, attn_g)


def _mlp_kernel(chunks, x_ref, yc_ref, ya_ref, wo_ref, g2_ref, wg_hbm, wu_hbm, wd_hbm, gf_ref,
                y_ref, hn_scr, wg_buf, wu_buf, wd_buf, sem):
    i = pl.program_id(0)
    n_tiles = pl.num_programs(0)
    n_slots = wg_buf.shape[0]
    n_chunks = len(chunks)

    def chunk_copies(c):
        slot = c % n_slots
        off, width = chunks[c]
        cols = pl.ds(off, width)
        return (pltpu.make_async_copy(wg_hbm.at[:, cols], wg_buf.at[slot, :, :width], sem.at[0, slot]),
                pltpu.make_async_copy(wu_hbm.at[:, cols], wu_buf.at[slot, :, :width], sem.at[1, slot]),
                pltpu.make_async_copy(wd_hbm.at[cols, :], wd_buf.at[slot, :width, :], sem.at[2, slot]))

    def start(c):
        for cp in chunk_copies(c):
            cp.start()

    def wait(c):
        for cp in chunk_copies(c):
            cp.wait()

    @pl.when(i == 0)
    def _():
        start(0)

    mix = (jnp.dot(yc_ref[...], wo_ref[:CONV_DIM, :], preferred_element_type=F32)
           + jnp.dot(ya_ref[...], wo_ref[CONV_DIM:, :], preferred_element_type=F32))
    h = x_ref[...] + mix
    y_ref[...] = h
    hn_scr[...] = _rms(h, g2_ref[...]).astype(BF16)

    for c in range(n_chunks):
        if c + 1 < n_chunks:
            start(c + 1)
        else:
            @pl.when(i + 1 < n_tiles)
            def _():
                start(0)
        wait(c)
        slot = c % n_slots
        width = chunks[c][1]
        hn = hn_scr[...]
        gate = jnp.dot(hn, wg_buf[slot, :, :width], preferred_element_type=F32)
        up = jnp.dot(hn, wu_buf[slot, :, :width], preferred_element_type=F32)
        act = (gate * jax.nn.sigmoid(gate) * up).astype(BF16)
        y_ref[...] += jnp.dot(act, wd_buf[slot, :width, :], preferred_element_type=F32)

    y_ref[...] = _rms(y_ref[...], gf_ref[...])


MLP_WEIGHT_SLOTS = 2


def _mlp_chunks(d_ff, tf):
    bounds = list(range(0, d_ff, tf)) + [d_ff]
    return tuple((lo, hi - lo) for lo, hi in zip(bounds[:-1], bounds[1:]))


def _mlp_call(x, yc, ya, w_out, g2, w_gate, w_up, w_down, gf, *, tm, tf, name):
    r = x.shape[0]
    chunks = _mlp_chunks(w_gate.shape[1], tf)
    assert len(chunks) % MLP_WEIGHT_SLOTS == 0 and all(w % LANES == 0 for _, w in chunks)
    rows = lambda i: (i, 0)
    const = lambda i: (0, 0)
    return pl.pallas_call(
        functools.partial(_mlp_kernel, chunks),
        grid=(r // tm,),
        in_specs=[
            pl.BlockSpec((tm, D_MODEL), rows),
            pl.BlockSpec((tm, CONV_DIM), rows),
            pl.BlockSpec((tm, ATTN_DIM), rows),
            pl.BlockSpec((D_MODEL, D_MODEL), const, pipeline_mode=pl.Buffered(1)),
            pl.BlockSpec((1, D_MODEL), const),
            pl.BlockSpec(memory_space=pl.ANY),
            pl.BlockSpec(memory_space=pl.ANY),
            pl.BlockSpec(memory_space=pl.ANY),
            pl.BlockSpec((1, D_MODEL), const),
        ],
        out_specs=pl.BlockSpec((tm, D_MODEL), rows),
        out_shape=jax.ShapeDtypeStruct((r, D_MODEL), F32),
        scratch_shapes=[
            pltpu.VMEM((tm, D_MODEL), BF16),
            pltpu.VMEM((MLP_WEIGHT_SLOTS, D_MODEL, tf), BF16),
            pltpu.VMEM((MLP_WEIGHT_SLOTS, D_MODEL, tf), BF16),
            pltpu.VMEM((MLP_WEIGHT_SLOTS, tf, D_MODEL), BF16),
            pltpu.SemaphoreType.DMA((3, MLP_WEIGHT_SLOTS)),
        ],
        compiler_params=pltpu.CompilerParams(
            dimension_semantics=("arbitrary",), vmem_limit_bytes=VMEM_LIMIT),
        name=name,
    )(x, yc, ya, w_out, g2, w_gate, w_up, w_down, gf)


def _rope_tables(pos):
    inv = ROPE_THETA ** (-jnp.arange(HALF_HEAD, dtype=F32) / HALF_HEAD)
    ang = pos.astype(F32)[:, None] * inv[None, :]
    cos = jnp.cos(ang)
    sin = jnp.sin(ang)
    reps = LANES // HEAD_DIM
    return (jnp.tile(jnp.concatenate([cos, cos], axis=-1), (1, reps)),
            jnp.tile(jnp.concatenate([-sin, sin], axis=-1), (1, reps)))


def kernel(x_prompt, x_sample, cache_k, cache_v, state_conv, meta_tokens, norm1_g, w_in, conv_w, conv_norm_g,
           attn_norm_g, attn_sinks, w_out, norm2_g, w_gate, w_up, w_down, final_norm_g):
    depth = w_in.shape[0]
    assert depth == 1, "single-layer step only"
    nb, seq, _ = x_prompt.shape
    ns, dec_seq, _ = x_sample.shape
    assert dec_seq == 1

    g1 = norm1_g[0][None]
    g2 = norm2_g[0][None]
    gf = final_norm_g[None]
    cg = conv_norm_g[0][None]
    ag = attn_norm_g[0][None]
    cw = conv_w[0]
    sinks = attn_sinks[0]
    w_in_b = w_in[0].astype(BF16)

    cos_m, sin_m = _rope_tables(jnp.arange(N_META))
    zero_prev = jnp.zeros((CONV_W - 1, CONV_DIM), F32)
    _, _, kd_m, vd_m, _, _, u_m = _proj_call(meta_tokens[None], g1, w_in_b, cw, cg, cos_m, sin_m, zero_prev,
                                             tm=N_META, sample=False)
    pad = ((WINDOW - N_META, 0), (0, 0))
    kd_meta = jnp.pad(kd_m[0], pad)
    vd_meta = jnp.pad(vd_m[0], pad)

    cos_p, sin_p = _rope_tables(N_META + jnp.arange(seq))
    yc, q, kd, vd, k_tail, v_tail, u_tail, w_out_b, w_gate_b, w_up_b, w_down_b = _proj_call(
        x_prompt, g1, w_in_b, cw, cg, cos_p, sin_p, u_m[0], tm=512, sample=False,
        cast=(w_out[0], w_gate[0], w_up[0], w_down[0]))
    ya = _attn_prompt(q, kd, vd, kd_meta, vd_meta, sinks, ag, n_sub=8)
    rp = nb * seq
    y_prompt = _mlp_call(x_prompt.reshape(rp, D_MODEL), yc.reshape(rp, CONV_DIM), ya.reshape(rp, ATTN_DIM),
                         w_out_b, g2, w_gate_b, w_up_b, w_down_b, gf, tm=512, tf=1024, name="mlp_prompt")
    y_prompt = y_prompt.reshape(nb, seq, D_MODEL)
    new_k_prompt = k_tail.reshape(1, nb, WINDOW, N_KV_HEADS, HEAD_DIM)
    new_v_prompt = v_tail.reshape(1, nb, WINDOW, N_KV_HEADS, HEAD_DIM)
    new_conv_prompt = u_tail[None]

    xs = x_sample.reshape(1, ns, D_MODEL)
    cos_s, sin_s = _rope_tables(jnp.full((ns,), PAST_LEN))
    prev_s = state_conv[0].reshape(ns, (CONV_W - 1) * CONV_DIM)
    yc_s, q_s, k_st, v_st, k_s, v_s, u_s = _proj_call(xs, g1, w_in_b, cw, cg, cos_s, sin_s, prev_s, tm=ns, sample=True)
    cache_kt = jnp.transpose(cache_k[0], (0, 2, 3, 1)).reshape(ns, KV_DIM, WINDOW)
    cache_vt = jnp.transpose(cache_v[0], (0, 2, 3, 1)).reshape(ns, KV_DIM, WINDOW)
    ya_s, nk_s, nv_s = _attn_sample(q_s[0], k_s[0], v_s[0], k_st[0], v_st[0], cache_kt, cache_vt, sinks, ag, nseq=16)
    y_sample = _mlp_call(xs[0], yc_s[0], ya_s, w_out_b, g2, w_gate_b, w_up_b, w_down_b, gf,
                         tm=ns, tf=1024, name="mlp_sample")
    y_sample = y_sample.reshape(ns, 1, D_MODEL)
    new_k_sample = jnp.transpose(nk_s.reshape(ns, N_KV_HEADS, HEAD_DIM, WINDOW), (0, 3, 1, 2))[None]
    new_v_sample = jnp.transpose(nv_s.reshape(ns, N_KV_HEADS, HEAD_DIM, WINDOW), (0, 3, 1, 2))[None]
    new_conv_sample = u_s.reshape(1, ns, CONV_W - 1, CONV_DIM)

    return (y_prompt, y_sample, new_k_prompt, new_v_prompt, new_conv_prompt,
            new_k_sample, new_v_sample, new_conv_sample)
```

```python
import functools

import jax
import jax.numpy as jnp
from jax import lax
from jax.experimental import pallas as pl
from jax.experimental.pallas import tpu as pltpu

D_MODEL = 2048
N_META = 16
CONV_DIM = 1024
CONV_W = 3
HEAD_DIM = 64
HALF_HEAD = HEAD_DIM // 2
ATTN_DIM = 1024
N_HEADS = 16
N_KV_HEADS = 4
GROUP = N_HEADS // N_KV_HEADS
KV_DIM = N_KV_HEADS * HEAD_DIM
WINDOW = 128
ROPE_THETA = 10000.0
RMS_EPS = 1e-6
NEG_INF = -1e30
PAST_LEN = 16384
LOG2_E = 1.4426950408889634

COL_BG = 0
COL_CG = CONV_DIM
COL_XC = 2 * CONV_DIM
COL_Q = 3 * CONV_DIM
COL_K = COL_Q + ATTN_DIM
COL_V = COL_K + KV_DIM
IN_COLS = COL_V + KV_DIM

LANES = 128
HEADS_PER_TILE = LANES // HEAD_DIM
VMEM_LIMIT = 60 * 1024 * 1024

F32 = jnp.float32
BF16 = jnp.bfloat16


def _rms(x, g):
    return x * lax.rsqrt(jnp.mean(x * x, axis=-1, keepdims=True) + RMS_EPS) * g


def _lane_tile(t, width):
    return jnp.concatenate([t] * (width // t.shape[-1]), axis=-1)


def _rope(t, cos, sin_signed):
    w = t.shape[-1]
    lane = lax.broadcasted_iota(jnp.int32, t.shape, 1)
    first_half = (lane & (HEAD_DIM - 1)) < HALF_HEAD
    swapped = jnp.where(first_half, pltpu.roll(t, w - HALF_HEAD, axis=1), pltpu.roll(t, HALF_HEAD, axis=1))
    return t * _lane_tile(cos, w) + swapped * _lane_tile(sin_signed, w)


def _dup_heads(pair_tile, half):
    lane = lax.broadcasted_iota(jnp.int32, pair_tile.shape, 1)
    swapped = pltpu.roll(pair_tile, HEAD_DIM, axis=1)
    if half == 0:
        return jnp.where(lane < HEAD_DIM, pair_tile, swapped)
    return jnp.where(lane < HEAD_DIM, swapped, pair_tile)


def _dup_all(t):
    tiles = []
    for g in range(N_KV_HEADS):
        pair, half = divmod(g, HEADS_PER_TILE)
        tiles.append(_dup_heads(t[:, pair * LANES:(pair + 1) * LANES], half))
    return jnp.concatenate(tiles, axis=-1).astype(BF16)


def _spread_heads(q):
    lane = lax.broadcasted_iota(jnp.int32, (q.shape[0], LANES), 1)
    tiles = []
    for h in range(N_HEADS):
        tile, half = divmod(h, HEADS_PER_TILE)
        qt = q[:, tile * LANES:(tile + 1) * LANES]
        keep = (lane < HEAD_DIM) if half == 0 else (lane >= HEAD_DIM)
        tiles.append(jnp.where(keep, qt, jnp.zeros_like(qt)))
    return jnp.concatenate(tiles, axis=-1)


def _head_block(t, h):
    return t[:, h * HEAD_DIM:(h + 1) * HEAD_DIM]


def _group_minor(t):
    return jnp.concatenate([_head_block(t, g * GROUP + i) for i in range(GROUP) for g in range(N_KV_HEADS)], axis=-1)


def _group_major(t):
    return jnp.concatenate([_head_block(t, i * N_KV_HEADS + g) for g in range(N_KV_HEADS) for i in range(GROUP)],
                           axis=-1)


N_PROJ_IN = 8
N_PROJ_OUT = 7


def _proj_kernel(sample, n_sub, *refs):
    x_ref, g1_ref, w_ref, cw_ref, cg_ref, cos_ref, sin_ref, prev_ref = refs[:N_PROJ_IN]
    n_cast = (len(refs) - 1 - N_PROJ_IN - N_PROJ_OUT) // 2
    cast_in = refs[N_PROJ_IN:N_PROJ_IN + n_cast]
    yc_ref, q_ref, kd_ref, vd_ref, kt_ref, vt_ref, u_ref = refs[N_PROJ_IN + n_cast:N_PROJ_IN + n_cast + N_PROJ_OUT]
    cast_out = refs[N_PROJ_IN + n_cast + N_PROJ_OUT:-1]
    carry_ref = refs[-1]

    ts = x_ref.shape[1] // n_sub
    cw = cw_ref[...]
    if not sample:
        @pl.when(pl.program_id(1) == 0)
        def _():
            carry_ref[...] = prev_ref[...]

        p2 = carry_ref[0:1, :]
        p1 = carry_ref[1:2, :]

    for t in range(n_sub):
        rows = slice(t * ts, (t + 1) * ts)
        hn = _rms(x_ref[0, rows, :], g1_ref[...]).astype(BF16)

        def section(lo, width):
            return jnp.dot(hn, w_ref[:, lo:lo + width], preferred_element_type=F32)

        u = section(COL_CG, CONV_DIM) * section(COL_XC, CONV_DIM)
        if sample:
            u2 = prev_ref[rows, :CONV_DIM]
            u1 = prev_ref[rows, CONV_DIM:]
            u_ref[0, rows, :] = jnp.concatenate([u1, u], axis=-1)
        else:
            row = lax.broadcasted_iota(jnp.int32, u.shape, 0)
            u1 = jnp.where(row == 0, p1, pltpu.roll(u, 1, axis=0))
            u2 = jnp.where(row == 0, p2, jnp.where(row == 1, p1, pltpu.roll(u, 2, axis=0)))
            p2 = u[ts - 2:ts - 1, :]
            p1 = u[ts - 1:, :]
        cy = cw[0:1, :] * u2 + cw[1:2, :] * u1 + cw[2:3, :] * u
        yc_ref[0, rows, :] = _rms(section(COL_BG, CONV_DIM) * cy, cg_ref[...]).astype(BF16)

        cos = cos_ref[rows, :]
        sin = sin_ref[rows, :]
        q = _rope(section(COL_Q, ATTN_DIM), cos, sin) * (HEAD_DIM ** -0.5 * LOG2_E)
        k = _rope(section(COL_K, KV_DIM), cos, sin)
        v = section(COL_V, KV_DIM)
        if sample:
            q_ref[0, rows, :] = _group_minor(q).astype(BF16)
            kd_ref[0, :, rows] = k.T
            vd_ref[0, :, rows] = v.T
        else:
            q_ref[0, rows, :] = _spread_heads(q.astype(BF16))
            kd_ref[0, rows, :] = _dup_all(k)
            vd_ref[0, rows, :] = _dup_all(v)

        if t == 0:
            for src, dst in zip(cast_in, cast_out):
                dst[...] = src[...].astype(BF16)

    if not sample:
        tail = jnp.concatenate([p2, p1], axis=0)
        carry_ref[...] = tail
        u_ref[0] = tail
    tail_rows = kt_ref.shape[1]
    kt_ref[0] = k[ts - tail_rows:, :]
    vt_ref[0] = v[ts - tail_rows:, :]


def _proj_call(x, g1, w_in, conv_w, conv_g, cos, sin, prev, *, tm, sample, n_sub=1, cast=()):
    nb, s, _ = x.shape
    n_steps = nb * (s // tm)
    slab = lambda b, i: (b * (s // tm) + i, 0)
    cast_specs = [pl.BlockSpec((w.shape[0] // n_steps, w.shape[1]), slab) for w in cast]
    assert all(w.shape[0] % (n_steps * 16) == 0 for w in cast)
    prev_rows, prev_cols = prev.shape[-2:]
    u_rows, u_cols = (tm, 2 * CONV_DIM) if sample else (CONV_W - 1, CONV_DIM)
    tail_rows = min(tm // n_sub, WINDOW)
    const = lambda b, i: (0, 0)
    rows = lambda b, i: (b, i, 0)
    per_seq = lambda b, i: (b, 0, 0)
    if sample:
        prev_spec = pl.BlockSpec((tm, prev_cols), lambda b, i: (i, 0))
        u_spec = pl.BlockSpec((1, u_rows, u_cols), rows)
        u_shape = (nb, s, u_cols)
        q_cols = ATTN_DIM
        kv_spec = pl.BlockSpec((1, KV_DIM, tm), lambda b, i: (b, 0, i))
        kv_shape = jax.ShapeDtypeStruct((nb, KV_DIM, s), F32)
    else:
        prev_spec = pl.BlockSpec((prev_rows, prev_cols), const)
        u_spec = pl.BlockSpec((1, u_rows, u_cols), per_seq)
        u_shape = (nb, u_rows, u_cols)
        q_cols = N_HEADS * LANES
        kv_spec = pl.BlockSpec((1, tm, N_KV_HEADS * LANES), rows)
        kv_shape = jax.ShapeDtypeStruct((nb, s, N_KV_HEADS * LANES), BF16)
    return pl.pallas_call(
        functools.partial(_proj_kernel, sample, n_sub),
        grid=(nb, s // tm),
        in_specs=[
            pl.BlockSpec((1, tm, D_MODEL), rows),
            pl.BlockSpec((1, D_MODEL), const),
            pl.BlockSpec((D_MODEL, IN_COLS), const, pipeline_mode=pl.Buffered(1)),
            pl.BlockSpec((CONV_W, CONV_DIM), const),
            pl.BlockSpec((1, CONV_DIM), const),
            pl.BlockSpec((tm, LANES), lambda b, i: (i, 0)),
            pl.BlockSpec((tm, LANES), lambda b, i: (i, 0)),
            prev_spec,
            *cast_specs,
        ],
        out_specs=[
            pl.BlockSpec((1, tm, CONV_DIM), rows),
            pl.BlockSpec((1, tm, q_cols), rows),
            kv_spec,
            kv_spec,
            pl.BlockSpec((1, tail_rows, KV_DIM), per_seq),
            pl.BlockSpec((1, tail_rows, KV_DIM), per_seq),
            u_spec,
            *cast_specs,
        ],
        out_shape=[
            jax.ShapeDtypeStruct((nb, s, CONV_DIM), BF16),
            jax.ShapeDtypeStruct((nb, s, q_cols), BF16),
            kv_shape,
            kv_shape,
            jax.ShapeDtypeStruct((nb, tail_rows, KV_DIM), F32),
            jax.ShapeDtypeStruct((nb, tail_rows, KV_DIM), F32),
            jax.ShapeDtypeStruct(u_shape, F32),
            *[jax.ShapeDtypeStruct(w.shape, BF16) for w in cast],
        ],
        scratch_shapes=[pltpu.VMEM((CONV_W - 1, CONV_DIM), F32)],
        compiler_params=pltpu.CompilerParams(
            dimension_semantics=("arbitrary", "arbitrary"), vmem_limit_bytes=VMEM_LIMIT),
        name="proj_sample" if sample else "proj_prompt",
    )(x, g1, w_in, conv_w, conv_g, cos, sin, prev, *cast)


def _attn_prompt_kernel(sinks_ref, q_ref, kc_ref, kp_ref, vc_ref, vp_ref, km_ref, vm_ref, ag_ref,
                        ya_ref, o_scr):
    first = pl.program_id(1) == 0
    blk = WINDOW
    nk = 2 * blk
    n_sub = q_ref.shape[1] // blk

    qi = lax.broadcasted_iota(jnp.int32, (blk, nk), 0)
    kj = lax.broadcasted_iota(jnp.int32, (blk, nk), 1)
    band = (kj >= qi) & (kj <= qi + WINDOW)
    band_first = band & (kj >= jnp.where(first, blk - N_META, 0))

    lane = lax.broadcasted_iota(jnp.int32, (blk, LANES), 1)
    ones = jnp.ones((nk, LANES), BF16)
    for t in range(n_sub):
        rows = slice(t * blk, (t + 1) * blk)
        valid = band_first if t == 0 else band
        for g in range(N_KV_HEADS):
            cols = slice(g * LANES, (g + 1) * LANES)
            if t == 0:
                k_prev = jnp.where(first, km_ref[:, cols], kp_ref[0, :, cols])
                v_prev = jnp.where(first, vm_ref[:, cols], vp_ref[0, :, cols])
            else:
                k_prev = kc_ref[0, (t - 1) * blk:t * blk, cols]
                v_prev = vc_ref[0, (t - 1) * blk:t * blk, cols]
            kd = jnp.concatenate([k_prev, kc_ref[0, rows, cols]], axis=0)
            vd = jnp.concatenate([v_prev, vc_ref[0, rows, cols]], axis=0)
            v_rhs = jnp.concatenate([vd, ones], axis=1)
            for tile in range(g * GROUP // HEADS_PER_TILE, (g + 1) * GROUP // HEADS_PER_TILE):
                ov, m = [], []
                for h in (HEADS_PER_TILE * tile, HEADS_PER_TILE * tile + 1):
                    qm = q_ref[0, rows, h * LANES:(h + 1) * LANES]
                    s = lax.dot_general(qm, kd, (((1,), (1,)), ((), ())), preferred_element_type=F32)
                    s = jnp.where(valid, s, NEG_INF)
                    m.append(jnp.max(s, axis=-1, keepdims=True))
                    p = jnp.exp2(s - m[-1]).astype(BF16)
                    ov.append(jnp.dot(p, v_rhs, preferred_element_type=F32))
                low = lane < HEAD_DIM
                sink = jnp.where(low[:1], sinks_ref[HEADS_PER_TILE * tile], sinks_ref[HEADS_PER_TILE * tile + 1])
                denom = (jnp.where(low, ov[0][:, LANES:], ov[1][:, LANES:])
                         + jnp.exp2(sink * LOG2_E - jnp.where(low, m[0], m[1])))
                o_scr[rows, tile * LANES:(tile + 1) * LANES] = (
                    jnp.where(low, ov[0][:, :LANES], ov[1][:, :LANES]) / denom)
        ya_ref[0, rows, :] = _rms(o_scr[rows, :], ag_ref[...]).astype(BF16)


def _attn_prompt(q, kd, vd, kd_meta, vd_meta, sinks, attn_g, *, n_sub):
    nb, s, kd_cols = kd.shape
    blk = WINDOW
    tq = n_sub * blk
    cur = lambda b, j: (b, j, 0)
    prv = lambda b, j: (b, jnp.maximum(j * n_sub - 1, 0), 0)
    const = lambda b, j: (0, 0)
    return pl.pallas_call(
        _attn_prompt_kernel,
        grid=(nb, s // tq),
        in_specs=[
            pl.BlockSpec(memory_space=pltpu.SMEM),
            pl.BlockSpec((1, tq, N_HEADS * LANES), cur),
            pl.BlockSpec((1, tq, kd_cols), cur),
            pl.BlockSpec((1, blk, kd_cols), prv),
            pl.BlockSpec((1, tq, kd_cols), cur),
            pl.BlockSpec((1, blk, kd_cols), prv),
            pl.BlockSpec((blk, kd_cols), const),
            pl.BlockSpec((blk, kd_cols), const),
            pl.BlockSpec((1, ATTN_DIM), const),
        ],
        out_specs=pl.BlockSpec((1, tq, ATTN_DIM), cur),
        out_shape=jax.ShapeDtypeStruct((nb, s, ATTN_DIM), BF16),
        scratch_shapes=[pltpu.VMEM((tq, ATTN_DIM), F32)],
        compiler_params=pltpu.CompilerParams(
            dimension_semantics=("arbitrary", "arbitrary"), vmem_limit_bytes=VMEM_LIMIT),
        name="attn_prompt",
    )(sinks, q, kd, kd, vd, vd, kd_meta, vd_meta, attn_g)


def _attn_sample_kernel(sinks_ref, q_ref, kn_ref, vn_ref, knt_ref, vnt_ref, ck_ref, cv_ref, ag_ref,
                        ya_ref, nk_ref, nv_ref, o_scr):
    nseq = q_ref.shape[0]
    base = pl.program_id(0) * nseq
    row = lax.broadcasted_iota(jnp.int32, (N_HEADS, KV_DIM), 0)
    lane = lax.broadcasted_iota(jnp.int32, (N_HEADS, KV_DIM), 1)
    own = (lane // HEAD_DIM) == (row % N_KV_HEADS)
    sink = jnp.concatenate(
        [jnp.full((1, 1), sinks_ref[(r % N_KV_HEADS) * GROUP + r // N_KV_HEADS] * LOG2_E, F32)
         for r in range(N_HEADS)], axis=0)
    key_lane = lax.broadcasted_iota(jnp.int32, (KV_DIM, WINDOW), 1)

    for n in range(nseq):
        kt = ck_ref[n]
        vt = cv_ref[n]
        kn = kn_ref[n:n + 1, :]
        vn = vn_ref[n:n + 1, :]
        qrow = q_ref[n:n + 1, :].astype(F32)
        q16 = jnp.broadcast_to(qrow[:, (GROUP - 1) * KV_DIM:], (N_HEADS, KV_DIM))
        for i in range(GROUP - 2, -1, -1):
            q16 = jnp.where(row < (i + 1) * N_KV_HEADS, qrow[:, i * KV_DIM:(i + 1) * KV_DIM], q16)
        q16 = jnp.where(own, q16, 0.0)
        s_c = jnp.dot(q16.astype(BF16), kt.astype(BF16), preferred_element_type=F32)
        s_n = jnp.sum(q16 * kn, axis=-1, keepdims=True)
        m = jnp.maximum(jnp.maximum(jnp.max(s_c, axis=-1, keepdims=True), s_n), sink)
        p_c = jnp.exp2(s_c - m)
        p_n = jnp.exp2(s_n - m)
        denom = jnp.sum(p_c, axis=-1, keepdims=True) + p_n + jnp.exp2(sink - m)
        o16 = lax.dot_general(p_c.astype(BF16), vt.astype(BF16), (((1,), (1,)), ((), ())),
                              preferred_element_type=F32)
        o16 = jnp.where(own, (o16 + p_n * vn) / denom, 0.0)
        for i in range(GROUP):
            in_group = (row >= i * N_KV_HEADS) & (row < (i + 1) * N_KV_HEADS)
            o_scr[n:n + 1, i * KV_DIM:(i + 1) * KV_DIM] = jnp.sum(jnp.where(in_group, o16, 0.0), axis=0, keepdims=True)
        is_seq = key_lane == base + n
        k_col = jnp.sum(jnp.where(is_seq, knt_ref[...], 0.0), axis=1, keepdims=True)
        v_col = jnp.sum(jnp.where(is_seq, vnt_ref[...], 0.0), axis=1, keepdims=True)
        nk_ref[n] = jnp.where(key_lane == WINDOW - 1, k_col, pltpu.roll(kt, WINDOW - 1, axis=1))
        nv_ref[n] = jnp.where(key_lane == WINDOW - 1, v_col, pltpu.roll(vt, WINDOW - 1, axis=1))
    ya_ref[...] = _rms(_group_major(o_scr[...]), ag_ref[...]).astype(BF16)


def _attn_sample(q, k_new, v_new, k_new_t, v_new_t, cache_kt, cache_vt, sinks, attn_g, *, nseq):
    n = q.shape[0]
    rows = lambda i: (i, 0)
    seqs = lambda i: (i, 0, 0)
    const = lambda i: (0, 0)
    return pl.pallas_call(
        _attn_sample_kernel,
        grid=(n // nseq,),
        in_specs=[
            pl.BlockSpec(memory_space=pltpu.SMEM),
            pl.BlockSpec((nseq, ATTN_DIM), rows),
            pl.BlockSpec((nseq, KV_DIM), rows),
            pl.BlockSpec((nseq, KV_DIM), rows),
            pl.BlockSpec((KV_DIM, n), const),
            pl.BlockSpec((KV_DIM, n), const),
            pl.BlockSpec((nseq, KV_DIM, WINDOW), seqs),
            pl.BlockSpec((nseq, KV_DIM, WINDOW), seqs),
            pl.BlockSpec((1, ATTN_DIM), const),
        ],
        out_specs=[
            pl.BlockSpec((nseq, ATTN_DIM), rows),
            pl.BlockSpec((nseq, KV_DIM, WINDOW), seqs),
            pl.BlockSpec((nseq, KV_DIM, WINDOW), seqs),
        ],
        out_shape=[
            jax.ShapeDtypeStruct((n, ATTN_DIM), BF16),
            jax.ShapeDtypeStruct((n, KV_DIM, WINDOW), F32),
            jax.ShapeDtypeStruct((n, KV_DIM, WINDOW), F32),
        ],
        scratch_shapes=[pltpu.VMEM((nseq, ATTN_DIM), F32)],
        compiler_params=pltpu.CompilerParams(
            dimension_semantics=("arbitrary",), vmem_limit_bytes=VMEM_LIMIT),
        name="attn_sample",
    )(sinks, q, k_new, v_new, k_new_t, v_new_t, cache_kt, cache_vt, attn_g)


def _mlp_kernel(chunks, x_ref, yc_ref, ya_ref, wo_ref, g2_ref, wg_hbm, wu_hbm, wd_hbm, gf_ref,
                y_ref, hn_scr, wg_buf, wu_buf, wd_buf, sem):
    i = pl.program_id(0)
    n_tiles = pl.num_programs(0)
    n_slots = wg_buf.shape[0]
    n_chunks = len(chunks)

    def chunk_copies(c):
        slot = c % n_slots
        off, width = chunks[c]
        cols = pl.ds(off, width)
        return (pltpu.make_async_copy(wg_hbm.at[:, cols], wg_buf.at[slot, :, :width], sem.at[0, slot]),
                pltpu.make_async_copy(wu_hbm.at[:, cols], wu_buf.at[slot, :, :width], sem.at[1, slot]),
                pltpu.make_async_copy(wd_hbm.at[cols, :], wd_buf.at[slot, :width, :], sem.at[2, slot]))

    def start(c):
        for cp in chunk_copies(c):
            cp.start()

    def wait(c):
        for cp in chunk_copies(c):
            cp.wait()

    @pl.when(i == 0)
    def _():
        start(0)

    mix = (jnp.dot(yc_ref[...], wo_ref[:CONV_DIM, :], preferred_element_type=F32)
           + jnp.dot(ya_ref[...], wo_ref[CONV_DIM:, :], preferred_element_type=F32))
    h = x_ref[...] + mix
    y_ref[...] = h
    hn_scr[...] = _rms(h, g2_ref[...]).astype(BF16)

    for c in range(n_chunks):
        if c + 1 < n_chunks:
            start(c + 1)
        else:
            @pl.when(i + 1 < n_tiles)
            def _():
                start(0)
        wait(c)
        slot = c % n_slots
        width = chunks[c][1]
        hn = hn_scr[...]
        gate = jnp.dot(hn, wg_buf[slot, :, :width], preferred_element_type=F32)
        up = jnp.dot(hn, wu_buf[slot, :, :width], preferred_element_type=F32)
        act = (gate * jax.nn.sigmoid(gate) * up).astype(BF16)
        y_ref[...] += jnp.dot(act, wd_buf[slot, :width, :], preferred_element_type=F32)

    y_ref[...] = _rms(y_ref[...], gf_ref[...])


MLP_WEIGHT_SLOTS = 2


def _mlp_chunks(d_ff, tf):
    bounds = list(range(0, d_ff, tf)) + [d_ff]
    return tuple((lo, hi - lo) for lo, hi in zip(bounds[:-1], bounds[1:]))


def _mlp_call(x, yc, ya, w_out, g2, w_gate, w_up, w_down, gf, *, tm, tf, name):
    r = x.shape[0]
    chunks = _mlp_chunks(w_gate.shape[1], tf)
    assert len(chunks) % MLP_WEIGHT_SLOTS == 0 and all(w % LANES == 0 for _, w in chunks)
    rows = lambda i: (i, 0)
    const = lambda i: (0, 0)
    return pl.pallas_call(
        functools.partial(_mlp_kernel, chunks),
        grid=(r // tm,),
        in_specs=[
            pl.BlockSpec((tm, D_MODEL), rows),
            pl.BlockSpec((tm, CONV_DIM), rows),
            pl.BlockSpec((tm, ATTN_DIM), rows),
            pl.BlockSpec((D_MODEL, D_MODEL), const, pipeline_mode=pl.Buffered(1)),
            pl.BlockSpec((1, D_MODEL), const),
            pl.BlockSpec(memory_space=pl.ANY),
            pl.BlockSpec(memory_space=pl.ANY),
            pl.BlockSpec(memory_space=pl.ANY),
            pl.BlockSpec((1, D_MODEL), const),
        ],
        out_specs=pl.BlockSpec((tm, D_MODEL), rows),
        out_shape=jax.ShapeDtypeStruct((r, D_MODEL), F32),
        scratch_shapes=[
            pltpu.VMEM((tm, D_MODEL), BF16),
            pltpu.VMEM((MLP_WEIGHT_SLOTS, D_MODEL, tf), BF16),
            pltpu.VMEM((MLP_WEIGHT_SLOTS, D_MODEL, tf), BF16),
            pltpu.VMEM((MLP_WEIGHT_SLOTS, tf, D_MODEL), BF16),
            pltpu.SemaphoreType.DMA((3, MLP_WEIGHT_SLOTS)),
        ],
        compiler_params=pltpu.CompilerParams(
            dimension_semantics=("arbitrary",), vmem_limit_bytes=VMEM_LIMIT),
        name=name,
    )(x, yc, ya, w_out, g2, w_gate, w_up, w_down, gf)


def _rope_tables(pos):
    inv = ROPE_THETA ** (-jnp.arange(HALF_HEAD, dtype=F32) / HALF_HEAD)
    ang = pos.astype(F32)[:, None] * inv[None, :]
    cos = jnp.cos(ang)
    sin = jnp.sin(ang)
    reps = LANES // HEAD_DIM
    return (jnp.tile(jnp.concatenate([cos, cos], axis=-1), (1, reps)),
            jnp.tile(jnp.concatenate([-sin, sin], axis=-1), (1, reps)))


def kernel(x_prompt, x_sample, cache_k, cache_v, state_conv, meta_tokens, norm1_g, w_in, conv_w, conv_norm_g,
           attn_norm_g, attn_sinks, w_out, norm2_g, w_gate, w_up, w_down, final_norm_g):
    depth = w_in.shape[0]
    assert depth == 1, "single-layer step only"
    nb, seq, _ = x_prompt.shape
    ns, dec_seq, _ = x_sample.shape
    assert dec_seq == 1

    g1 = norm1_g[0][None]
    g2 = norm2_g[0][None]
    gf = final_norm_g[None]
    cg = conv_norm_g[0][None]
    ag = attn_norm_g[0][None]
    cw = conv_w[0]
    sinks = attn_sinks[0]
    w_in_b = w_in[0].astype(BF16)

    cos_m, sin_m = _rope_tables(jnp.arange(N_META))
    zero_prev = jnp.zeros((CONV_W - 1, CONV_DIM), F32)
    _, _, kd_m, vd_m, _, _, u_m = _proj_call(meta_tokens[None], g1, w_in_b, cw, cg, cos_m, sin_m, zero_prev,
                                             tm=N_META, sample=False)
    pad = ((WINDOW - N_META, 0), (0, 0))
    kd_meta = jnp.pad(kd_m[0], pad)
    vd_meta = jnp.pad(vd_m[0], pad)

    cos_p, sin_p = _rope_tables(N_META + jnp.arange(seq))
    yc, q, kd, vd, k_tail, v_tail, u_tail, w_out_b, w_gate_b, w_up_b, w_down_b = _proj_call(
        x_prompt, g1, w_in_b, cw, cg, cos_p, sin_p, u_m[0], tm=512, sample=False, n_sub=2,
        cast=(w_out[0], w_gate[0], w_up[0], w_down[0]))
    ya = _attn_prompt(q, kd, vd, kd_meta, vd_meta, sinks, ag, n_sub=8)
    rp = nb * seq
    y_prompt = _mlp_call(x_prompt.reshape(rp, D_MODEL), yc.reshape(rp, CONV_DIM), ya.reshape(rp, ATTN_DIM),
                         w_out_b, g2, w_gate_b, w_up_b, w_down_b, gf, tm=512, tf=1024, name="mlp_prompt")
    y_prompt = y_prompt.reshape(nb, seq, D_MODEL)
    new_k_prompt = k_tail.reshape(1, nb, WINDOW, N_KV_HEADS, HEAD_DIM)
    new_v_prompt = v_tail.reshape(1, nb, WINDOW, N_KV_HEADS, HEAD_DIM)
    new_conv_prompt = u_tail[None]

    xs = x_sample.reshape(1, ns, D_MODEL)
    cos_s, sin_s = _rope_tables(jnp.full((ns,), PAST_LEN))
    prev_s = state_conv[0].reshape(ns, (CONV_W - 1) * CONV_DIM)
    yc_s, q_s, k_st, v_st, k_s, v_s, u_s = _proj_call(xs, g1, w_in_b, cw, cg, cos_s, sin_s, prev_s, tm=ns, sample=True)
    cache_kt = jnp.transpose(cache_k[0], (0, 2, 3, 1)).reshape(ns, KV_DIM, WINDOW)
    cache_vt = jnp.transpose(cache_v[0], (0, 2, 3, 1)).reshape(ns, KV_DIM, WINDOW)
    ya_s, nk_s, nv_s = _attn_sample(q_s[0], k_s[0], v_s[0], k_st[0], v_st[0], cache_kt, cache_vt, sinks, ag, nseq=16)
    y_sample = _mlp_call(xs[0], yc_s[0], ya_s, w_out_b, g2, w_gate_b, w_up_b, w_down_b, gf,
                         tm=ns, tf=1024, name="mlp_sample")
    y_sample = y_sample.reshape(ns, 1, D_MODEL)
    new_k_sample = jnp.transpose(nk_s.reshape(ns, N_KV_HEADS, HEAD_DIM, WINDOW), (0, 3, 1, 2))[None]
    new_v_sample = jnp.transpose(nv_s.reshape(ns, N_KV_HEADS, HEAD_DIM, WINDOW), (0, 3, 1, 2))[None]
    new_conv_sample = u_s.reshape(1, ns, CONV_W - 1, CONV_DIM)

    return (y_prompt, y_sample, new_k_prompt, new_v_prompt, new_conv_prompt,
            new_k_sample, new_v_sample, new_conv_sample)
```

```python
import functools

import jax
import jax.numpy as jnp
from jax import lax
from jax.experimental import pallas as pl
from jax.experimental.pallas import tpu as pltpu

D_MODEL = 2048
N_META = 16
CONV_DIM = 1024
CONV_W = 3
HEAD_DIM = 64
HALF_HEAD = HEAD_DIM // 2
ATTN_DIM = 1024
N_HEADS = 16
N_KV_HEADS = 4
GROUP = N_HEADS // N_KV_HEADS
KV_DIM = N_KV_HEADS * HEAD_DIM
WINDOW = 128
ROPE_THETA = 10000.0
RMS_EPS = 1e-6
NEG_INF = -1e30
PAST_LEN = 16384
LOG2_E = 1.4426950408889634

COL_BG = 0
COL_CG = CONV_DIM
COL_XC = 2 * CONV_DIM
COL_Q = 3 * CONV_DIM
COL_K = COL_Q + ATTN_DIM
COL_V = COL_K + KV_DIM
IN_COLS = COL_V + KV_DIM

LANES = 128
HEADS_PER_TILE = LANES // HEAD_DIM
VMEM_LIMIT = 60 * 1024 * 1024

F32 = jnp.float32
BF16 = jnp.bfloat16


def _rms(x, g):
    return x * lax.rsqrt(jnp.mean(x * x, axis=-1, keepdims=True) + RMS_EPS) * g


def _lane_tile(t, width):
    return jnp.concatenate([t] * (width // t.shape[-1]), axis=-1)


def _rope(t, cos, sin_signed):
    w = t.shape[-1]
    lane = lax.broadcasted_iota(jnp.int32, t.shape, 1)
    first_half = (lane & (HEAD_DIM - 1)) < HALF_HEAD
    swapped = jnp.where(first_half, pltpu.roll(t, w - HALF_HEAD, axis=1), pltpu.roll(t, HALF_HEAD, axis=1))
    return t * _lane_tile(cos, w) + swapped * _lane_tile(sin_signed, w)


def _dup_heads(pair_tile, half):
    lane = lax.broadcasted_iota(jnp.int32, pair_tile.shape, 1)
    swapped = pltpu.roll(pair_tile, HEAD_DIM, axis=1)
    if half == 0:
        return jnp.where(lane < HEAD_DIM, pair_tile, swapped)
    return jnp.where(lane < HEAD_DIM, swapped, pair_tile)


def _dup_all(t):
    tiles = []
    for g in range(N_KV_HEADS):
        pair, half = divmod(g, HEADS_PER_TILE)
        tiles.append(_dup_heads(t[:, pair * LANES:(pair + 1) * LANES], half))
    return jnp.concatenate(tiles, axis=-1).astype(BF16)


def _spread_heads(q):
    lane = lax.broadcasted_iota(jnp.int32, (q.shape[0], LANES), 1)
    tiles = []
    for h in range(N_HEADS):
        tile, half = divmod(h, HEADS_PER_TILE)
        qt = q[:, tile * LANES:(tile + 1) * LANES]
        keep = (lane < HEAD_DIM) if half == 0 else (lane >= HEAD_DIM)
        tiles.append(jnp.where(keep, qt, jnp.zeros_like(qt)))
    return jnp.concatenate(tiles, axis=-1)


def _head_block(t, h):
    return t[:, h * HEAD_DIM:(h + 1) * HEAD_DIM]


def _group_minor(t):
    return jnp.concatenate([_head_block(t, g * GROUP + i) for i in range(GROUP) for g in range(N_KV_HEADS)], axis=-1)


def _group_major(t):
    return jnp.concatenate([_head_block(t, i * N_KV_HEADS + g) for g in range(N_KV_HEADS) for i in range(GROUP)],
                           axis=-1)


N_PROJ_IN = 8
N_PROJ_OUT = 7


def _proj_kernel(sample, n_sub, *refs):
    x_ref, g1_ref, w_ref, cw_ref, cg_ref, cos_ref, sin_ref, prev_ref = refs[:N_PROJ_IN]
    n_cast = (len(refs) - 1 - N_PROJ_IN - N_PROJ_OUT) // 2
    cast_in = refs[N_PROJ_IN:N_PROJ_IN + n_cast]
    yc_ref, q_ref, kd_ref, vd_ref, kt_ref, vt_ref, u_ref = refs[N_PROJ_IN + n_cast:N_PROJ_IN + n_cast + N_PROJ_OUT]
    cast_out = refs[N_PROJ_IN + n_cast + N_PROJ_OUT:-1]
    carry_ref = refs[-1]

    ts = x_ref.shape[1] // n_sub
    cw = cw_ref[...]
    if not sample:
        @pl.when(pl.program_id(1) == 0)
        def _():
            carry_ref[...] = prev_ref[...]

        p2 = carry_ref[0:1, :]
        p1 = carry_ref[1:2, :]

    for t in range(n_sub):
        rows = slice(t * ts, (t + 1) * ts)
        hn = _rms(x_ref[0, rows, :], g1_ref[...]).astype(BF16)

        def section(lo, width):
            return jnp.dot(hn, w_ref[:, lo:lo + width], preferred_element_type=F32)

        u = section(COL_CG, CONV_DIM) * section(COL_XC, CONV_DIM)
        if sample:
            u2 = prev_ref[rows, :CONV_DIM]
            u1 = prev_ref[rows, CONV_DIM:]
            u_ref[0, rows, :] = jnp.concatenate([u1, u], axis=-1)
        else:
            row = lax.broadcasted_iota(jnp.int32, u.shape, 0)
            u1 = jnp.where(row == 0, p1, pltpu.roll(u, 1, axis=0))
            u2 = jnp.where(row == 0, p2, jnp.where(row == 1, p1, pltpu.roll(u, 2, axis=0)))
            p2 = u[ts - 2:ts - 1, :]
            p1 = u[ts - 1:, :]
        cy = cw[0:1, :] * u2 + cw[1:2, :] * u1 + cw[2:3, :] * u
        yc_ref[0, rows, :] = _rms(section(COL_BG, CONV_DIM) * cy, cg_ref[...]).astype(BF16)

        cos = cos_ref[rows, :]
        sin = sin_ref[rows, :]
        q = _rope(section(COL_Q, ATTN_DIM), cos, sin) * (HEAD_DIM ** -0.5 * LOG2_E)
        k = _rope(section(COL_K, KV_DIM), cos, sin)
        v = section(COL_V, KV_DIM)
        if sample:
            q_ref[0, rows, :] = _group_minor(q).astype(BF16)
            kd_ref[0, :, rows] = k.T
            vd_ref[0, :, rows] = v.T
        else:
            q_ref[0, rows, :] = _spread_heads(q.astype(BF16))
            kd_ref[0, rows, :] = _dup_all(k)
            vd_ref[0, rows, :] = _dup_all(v)

        if t == 0:
            for src, dst in zip(cast_in, cast_out):
                dst[...] = src[...].astype(BF16)

    if not sample:
        tail = jnp.concatenate([p2, p1], axis=0)
        carry_ref[...] = tail
        u_ref[0] = tail
    tail_rows = kt_ref.shape[1]
    kt_ref[0] = k[ts - tail_rows:, :]
    vt_ref[0] = v[ts - tail_rows:, :]


def _proj_call(x, g1, w_in, conv_w, conv_g, cos, sin, prev, *, tm, sample, n_sub=1, cast=()):
    nb, s, _ = x.shape
    n_steps = nb * (s // tm)
    slab = lambda b, i: (b * (s // tm) + i, 0)
    cast_specs = [pl.BlockSpec((w.shape[0] // n_steps, w.shape[1]), slab) for w in cast]
    assert all(w.shape[0] % (n_steps * 16) == 0 for w in cast)
    prev_rows, prev_cols = prev.shape[-2:]
    u_rows, u_cols = (tm, 2 * CONV_DIM) if sample else (CONV_W - 1, CONV_DIM)
    tail_rows = min(tm // n_sub, WINDOW)
    const = lambda b, i: (0, 0)
    rows = lambda b, i: (b, i, 0)
    per_seq = lambda b, i: (b, 0, 0)
    if sample:
        prev_spec = pl.BlockSpec((tm, prev_cols), lambda b, i: (i, 0))
        u_spec = pl.BlockSpec((1, u_rows, u_cols), rows)
        u_shape = (nb, s, u_cols)
        q_cols = ATTN_DIM
        kv_spec = pl.BlockSpec((1, KV_DIM, tm), lambda b, i: (b, 0, i))
        kv_shape = jax.ShapeDtypeStruct((nb, KV_DIM, s), F32)
    else:
        prev_spec = pl.BlockSpec((prev_rows, prev_cols), const)
        u_spec = pl.BlockSpec((1, u_rows, u_cols), per_seq)
        u_shape = (nb, u_rows, u_cols)
        q_cols = N_HEADS * LANES
        kv_spec = pl.BlockSpec((1, tm, N_KV_HEADS * LANES), rows)
        kv_shape = jax.ShapeDtypeStruct((nb, s, N_KV_HEADS * LANES), BF16)
    return pl.pallas_call(
        functools.partial(_proj_kernel, sample, n_sub),
        grid=(nb, s // tm),
        in_specs=[
            pl.BlockSpec((1, tm, D_MODEL), rows),
            pl.BlockSpec((1, D_MODEL), const),
            pl.BlockSpec((D_MODEL, IN_COLS), const, pipeline_mode=pl.Buffered(1)),
            pl.BlockSpec((CONV_W, CONV_DIM), const),
            pl.BlockSpec((1, CONV_DIM), const),
            pl.BlockSpec((tm, LANES), lambda b, i: (i, 0)),
            pl.BlockSpec((tm, LANES), lambda b, i: (i, 0)),
            prev_spec,
            *cast_specs,
        ],
        out_specs=[
            pl.BlockSpec((1, tm, CONV_DIM), rows),
            pl.BlockSpec((1, tm, q_cols), rows),
            kv_spec,
            kv_spec,
            pl.BlockSpec((1, tail_rows, KV_DIM), per_seq),
            pl.BlockSpec((1, tail_rows, KV_DIM), per_seq),
            u_spec,
            *cast_specs,
        ],
        out_shape=[
            jax.ShapeDtypeStruct((nb, s, CONV_DIM), BF16),
            jax.ShapeDtypeStruct((nb, s, q_cols), BF16),
            kv_shape,
            kv_shape,
            jax.ShapeDtypeStruct((nb, tail_rows, KV_DIM), F32),
            jax.ShapeDtypeStruct((nb, tail_rows, KV_DIM), F32),
            jax.ShapeDtypeStruct(u_shape, F32),
            *[jax.ShapeDtypeStruct(w.shape, BF16) for w in cast],
        ],
        scratch_shapes=[pltpu.VMEM((CONV_W - 1, CONV_DIM), F32)],
        compiler_params=pltpu.CompilerParams(
            dimension_semantics=("arbitrary", "arbitrary"), vmem_limit_bytes=VMEM_LIMIT),
        name="proj_sample" if sample else "proj_prompt",
    )(x, g1, w_in, conv_w, conv_g, cos, sin, prev, *cast)


def _attn_prompt_kernel(sinks_ref, q_ref, kc_ref, kp_ref, vc_ref, vp_ref, km_ref, vm_ref, ag_ref,
                        ya_ref, o_scr):
    first = pl.program_id(1) == 0
    blk = WINDOW
    nk = 2 * blk
    n_sub = q_ref.shape[1] // blk

    qi = lax.broadcasted_iota(jnp.int32, (blk, nk), 0)
    kj = lax.broadcasted_iota(jnp.int32, (blk, nk), 1)
    band = (kj >= qi) & (kj <= qi + WINDOW)
    band_first = band & (kj >= jnp.where(first, blk - N_META, 0))

    lane = lax.broadcasted_iota(jnp.int32, (blk, LANES), 1)
    ones = jnp.ones((nk, LANES), BF16)
    for t in range(n_sub):
        rows = slice(t * blk, (t + 1) * blk)
        valid = band_first if t == 0 else band
        for g in range(N_KV_HEADS):
            cols = slice(g * LANES, (g + 1) * LANES)
            if t == 0:
                k_prev = jnp.where(first, km_ref[:, cols], kp_ref[0, :, cols])
                v_prev = jnp.where(first, vm_ref[:, cols], vp_ref[0, :, cols])
            else:
                k_prev = kc_ref[0, (t - 1) * blk:t * blk, cols]
                v_prev = vc_ref[0, (t - 1) * blk:t * blk, cols]
            kd = jnp.concatenate([k_prev, kc_ref[0, rows, cols]], axis=0)
            vd = jnp.concatenate([v_prev, vc_ref[0, rows, cols]], axis=0)
            v_rhs = jnp.concatenate([vd, ones], axis=1)
            for tile in range(g * GROUP // HEADS_PER_TILE, (g + 1) * GROUP // HEADS_PER_TILE):
                ov, m = [], []
                for h in (HEADS_PER_TILE * tile, HEADS_PER_TILE * tile + 1):
                    qm = q_ref[0, rows, h * LANES:(h + 1) * LANES]
                    s = lax.dot_general(qm, kd, (((1,), (1,)), ((), ())), preferred_element_type=F32)
                    s = jnp.where(valid, s, NEG_INF)
                    m.append(jnp.max(s, axis=-1, keepdims=True))
                    p = jnp.exp2(s - m[-1]).astype(BF16)
                    ov.append(jnp.dot(p, v_rhs, preferred_element_type=F32))
                low = lane < HEAD_DIM
                sink = jnp.where(low[:1], sinks_ref[HEADS_PER_TILE * tile], sinks_ref[HEADS_PER_TILE * tile + 1])
                denom = (jnp.where(low, ov[0][:, LANES:], ov[1][:, LANES:])
                         + jnp.exp2(sink * LOG2_E - jnp.where(low, m[0], m[1])))
                o_scr[rows, tile * LANES:(tile + 1) * LANES] = (
                    jnp.where(low, ov[0][:, :LANES], ov[1][:, :LANES]) / denom)
        ya_ref[0, rows, :] = _rms(o_scr[rows, :], ag_ref[...]).astype(BF16)


def _attn_prompt(q, kd, vd, kd_meta, vd_meta, sinks, attn_g, *, n_sub):
    nb, s, kd_cols = kd.shape
    blk = WINDOW
    tq = n_sub * blk
    cur = lambda b, j: (b, j, 0)
    prv = lambda b, j: (b, jnp.maximum(j * n_sub - 1, 0), 0)
    const = lambda b, j: (0, 0)
    return pl.pallas_call(
        _attn_prompt_kernel,
        grid=(nb, s // tq),
        in_specs=[
            pl.BlockSpec(memory_space=pltpu.SMEM),
            pl.BlockSpec((1, tq, N_HEADS * LANES), cur),
            pl.BlockSpec((1, tq, kd_cols), cur),
            pl.BlockSpec((1, blk, kd_cols), prv),
            pl.BlockSpec((1, tq, kd_cols), cur),
            pl.BlockSpec((1, blk, kd_cols), prv),
            pl.BlockSpec((blk, kd_cols), const),
            pl.BlockSpec((blk, kd_cols), const),
            pl.BlockSpec((1, ATTN_DIM), const),
        ],
        out_specs=pl.BlockSpec((1, tq, ATTN_DIM), cur),
        out_shape=jax.ShapeDtypeStruct((nb, s, ATTN_DIM), BF16),
        scratch_shapes=[pltpu.VMEM((tq, ATTN_DIM), F32)],
        compiler_params=pltpu.CompilerParams(
            dimension_semantics=("arbitrary", "arbitrary"), vmem_limit_bytes=VMEM_LIMIT),
        name="attn_prompt",
    )(sinks, q, kd, kd, vd, vd, kd_meta, vd_meta, attn_g)


def _attn_sample_kernel(sinks_ref, q_ref, kn_ref, vn_ref, knt_ref, vnt_ref, ck_ref, cv_ref, ag_ref,
                        ya_ref, nk_ref, nv_ref):
    nseq = q_ref.shape[0]
    base = pl.program_id(0) * nseq
    n_rows = N_HEADS * nseq
    qf = q_ref[...].astype(F32)
    kv_of_lane = lax.broadcasted_iota(jnp.int32, (nseq, KV_DIM), 1) // HEAD_DIM
    q_rows, sink_rows = [], []
    for r in range(N_HEADS):
        i, g = divmod(r, N_KV_HEADS)
        q_rows.append(jnp.where(kv_of_lane == g, qf[:, i * KV_DIM:(i + 1) * KV_DIM], 0.0))
        sink_rows.append(jnp.full((nseq, 1), sinks_ref[g * GROUP + i] * LOG2_E, F32))
    q_all = jnp.concatenate(q_rows, axis=0)
    sink = jnp.concatenate(sink_rows, axis=0)
    q_bf = q_all.astype(BF16)

    seq_of_row = lax.broadcasted_iota(jnp.int32, (n_rows, WINDOW), 0) % nseq
    s_c = jnp.zeros((n_rows, WINDOW), F32)
    for n in range(nseq):
        s_n = jnp.dot(q_bf, ck_ref[n].astype(BF16), preferred_element_type=F32)
        s_c = jnp.where(seq_of_row == n, s_n, s_c)
    kn = jnp.concatenate([kn_ref[...]] * N_HEADS, axis=0)
    vn = jnp.concatenate([vn_ref[...]] * N_HEADS, axis=0)
    s_new = jnp.sum(q_all * kn, axis=-1, keepdims=True)
    m = jnp.maximum(jnp.maximum(jnp.max(s_c, axis=-1, keepdims=True), s_new), sink)
    p_c = jnp.exp2(s_c - m)
    p_new = jnp.exp2(s_new - m)
    denom = jnp.sum(p_c, axis=-1, keepdims=True) + p_new + jnp.exp2(sink - m)
    p_bf = p_c.astype(BF16)

    seq_of_row = lax.broadcasted_iota(jnp.int32, (n_rows, KV_DIM), 0) % nseq
    o = jnp.zeros((n_rows, KV_DIM), F32)
    for n in range(nseq):
        o_n = lax.dot_general(p_bf, cv_ref[n].astype(BF16), (((1,), (1,)), ((), ())), preferred_element_type=F32)
        o = jnp.where(seq_of_row == n, o_n, o)
    kv_of_row = (lax.broadcasted_iota(jnp.int32, (n_rows, KV_DIM), 0) // nseq) % N_KV_HEADS
    own = (lax.broadcasted_iota(jnp.int32, (n_rows, KV_DIM), 1) // HEAD_DIM) == kv_of_row
    o = jnp.where(own, (o + p_new * vn) / denom, 0.0)
    slabs = []
    for i in range(GROUP):
        lo = i * N_KV_HEADS * nseq
        slabs.append(sum(o[lo + g * nseq:lo + (g + 1) * nseq] for g in range(N_KV_HEADS)))
    ya_ref[...] = _rms(_group_major(jnp.concatenate(slabs, axis=-1)), ag_ref[...]).astype(BF16)

    key_lane = lax.broadcasted_iota(jnp.int32, (KV_DIM, WINDOW), 1)
    for n in range(nseq):
        is_seq = key_lane == base + n
        k_col = jnp.sum(jnp.where(is_seq, knt_ref[...], 0.0), axis=1, keepdims=True)
        v_col = jnp.sum(jnp.where(is_seq, vnt_ref[...], 0.0), axis=1, keepdims=True)
        nk_ref[n] = jnp.where(key_lane == WINDOW - 1, k_col, pltpu.roll(ck_ref[n], WINDOW - 1, axis=1))
        nv_ref[n] = jnp.where(key_lane == WINDOW - 1, v_col, pltpu.roll(cv_ref[n], WINDOW - 1, axis=1))


def _attn_sample(q, k_new, v_new, k_new_t, v_new_t, cache_kt, cache_vt, sinks, attn_g, *, nseq):
    n = q.shape[0]
    rows = lambda i: (i, 0)
    seqs = lambda i: (i, 0, 0)
    const = lambda i: (0, 0)
    return pl.pallas_call(
        _attn_sample_kernel,
        grid=(n // nseq,),
        in_specs=[
            pl.BlockSpec(memory_space=pltpu.SMEM),
            pl.BlockSpec((nseq, ATTN_DIM), rows),
            pl.BlockSpec((nseq, KV_DIM), rows),
            pl.BlockSpec((nseq, KV_DIM), rows),
            pl.BlockSpec((KV_DIM, n), const),
            pl.BlockSpec((KV_DIM, n), const),
            pl.BlockSpec((nseq, KV_DIM, WINDOW), seqs),
            pl.BlockSpec((nseq, KV_DIM, WINDOW), seqs),
            pl.BlockSpec((1, ATTN_DIM), const),
        ],
        out_specs=[
            pl.BlockSpec((nseq, ATTN_DIM), rows),
            pl.BlockSpec((nseq, KV_DIM, WINDOW), seqs),
            pl.BlockSpec((nseq, KV_DIM, WINDOW), seqs),
        ],
        out_shape=[
            jax.ShapeDtypeStruct((n, ATTN_DIM), BF16),
            jax.ShapeDtypeStruct((n, KV_DIM, WINDOW), F32),
            jax.ShapeDtypeStruct((n, KV_DIM, WINDOW), F32),
        ],
        compiler_params=pltpu.CompilerParams(
            dimension_semantics=("arbitrary",), vmem_limit_bytes=VMEM_LIMIT),
        name="attn_sample",
    )(sinks, q, k_new, v_new, k_new_t, v_new_t, cache_kt, cache_vt, attn_g)


def _mlp_kernel(chunks, n_sub, x_ref, yc_ref, ya_ref, wo_ref, g2_ref, wg_hbm, wu_hbm, wd_hbm, gf_ref,
                y_ref, hn_scr, wg_buf, wu_buf, wd_buf, sem):
    i = pl.program_id(0)
    n_tiles = pl.num_programs(0)
    n_slots = wg_buf.shape[0]
    n_chunks = len(chunks)

    def chunk_copies(c):
        slot = c % n_slots
        off, width = chunks[c]
        cols = pl.ds(off, width)
        return (pltpu.make_async_copy(wg_hbm.at[:, cols], wg_buf.at[slot, :, :width], sem.at[0, slot]),
                pltpu.make_async_copy(wu_hbm.at[:, cols], wu_buf.at[slot, :, :width], sem.at[1, slot]),
                pltpu.make_async_copy(wd_hbm.at[cols, :], wd_buf.at[slot, :width, :], sem.at[2, slot]))

    def start(c):
        for cp in chunk_copies(c):
            cp.start()

    def wait(c):
        for cp in chunk_copies(c):
            cp.wait()

    @pl.when(i == 0)
    def _():
        start(0)

    ts = y_ref.shape[0] // n_sub
    sub_tiles = tuple(slice(t * ts, (t + 1) * ts) for t in range(n_sub))
    for rows in sub_tiles:
        mix = (jnp.dot(yc_ref[rows, :], wo_ref[:CONV_DIM, :], preferred_element_type=F32)
               + jnp.dot(ya_ref[rows, :], wo_ref[CONV_DIM:, :], preferred_element_type=F32))
        h = x_ref[rows, :] + mix
        y_ref[rows, :] = h
        hn_scr[rows, :] = _rms(h, g2_ref[...]).astype(BF16)

    for c in range(n_chunks):
        if c + 1 < n_chunks:
            start(c + 1)
        else:
            @pl.when(i + 1 < n_tiles)
            def _():
                start(0)
        wait(c)
        slot = c % n_slots
        width = chunks[c][1]
        for rows in sub_tiles:
            hn = hn_scr[rows, :]
            gate = jnp.dot(hn, wg_buf[slot, :, :width], preferred_element_type=F32)
            up = jnp.dot(hn, wu_buf[slot, :, :width], preferred_element_type=F32)
            act = (gate * jax.nn.sigmoid(gate) * up).astype(BF16)
            y_ref[rows, :] += jnp.dot(act, wd_buf[slot, :width, :], preferred_element_type=F32)

    for rows in sub_tiles:
        y_ref[rows, :] = _rms(y_ref[rows, :], gf_ref[...])


MLP_WEIGHT_SLOTS = 2


def _mlp_chunks(d_ff, tf):
    bounds = list(range(0, d_ff, tf)) + [d_ff]
    return tuple((lo, hi - lo) for lo, hi in zip(bounds[:-1], bounds[1:]))


def _mlp_call(x, yc, ya, w_out, g2, w_gate, w_up, w_down, gf, *, tm, tf, n_sub, name):
    r = x.shape[0]
    chunks = _mlp_chunks(w_gate.shape[1], tf)
    assert len(chunks) % MLP_WEIGHT_SLOTS == 0 and all(w % LANES == 0 for _, w in chunks)
    rows = lambda i: (i, 0)
    const = lambda i: (0, 0)
    return pl.pallas_call(
        functools.partial(_mlp_kernel, chunks, n_sub),
        grid=(r // tm,),
        in_specs=[
            pl.BlockSpec((tm, D_MODEL), rows),
            pl.BlockSpec((tm, CONV_DIM), rows),
            pl.BlockSpec((tm, ATTN_DIM), rows),
            pl.BlockSpec((D_MODEL, D_MODEL), const, pipeline_mode=pl.Buffered(1)),
            pl.BlockSpec((1, D_MODEL), const),
            pl.BlockSpec(memory_space=pl.ANY),
            pl.BlockSpec(memory_space=pl.ANY),
            pl.BlockSpec(memory_space=pl.ANY),
            pl.BlockSpec((1, D_MODEL), const),
        ],
        out_specs=pl.BlockSpec((tm, D_MODEL), rows),
        out_shape=jax.ShapeDtypeStruct((r, D_MODEL), F32),
        scratch_shapes=[
            pltpu.VMEM((tm, D_MODEL), BF16),
            pltpu.VMEM((MLP_WEIGHT_SLOTS, D_MODEL, tf), BF16),
            pltpu.VMEM((MLP_WEIGHT_SLOTS, D_MODEL, tf), BF16),
            pltpu.VMEM((MLP_WEIGHT_SLOTS, tf, D_MODEL), BF16),
            pltpu.SemaphoreType.DMA((3, MLP_WEIGHT_SLOTS)),
        ],
        compiler_params=pltpu.CompilerParams(
            dimension_semantics=("arbitrary",), vmem_limit_bytes=VMEM_LIMIT),
        name=name,
    )(x, yc, ya, w_out, g2, w_gate, w_up, w_down, gf)


def _rope_tables(pos):
    inv = ROPE_THETA ** (-jnp.arange(HALF_HEAD, dtype=F32) / HALF_HEAD)
    ang = pos.astype(F32)[:, None] * inv[None, :]
    cos = jnp.cos(ang)
    sin = jnp.sin(ang)
    reps = LANES // HEAD_DIM
    return (jnp.tile(jnp.concatenate([cos, cos], axis=-1), (1, reps)),
            jnp.tile(jnp.concatenate([-sin, sin], axis=-1), (1, reps)))


def kernel(x_prompt, x_sample, cache_k, cache_v, state_conv, meta_tokens, norm1_g, w_in, conv_w, conv_norm_g,
           attn_norm_g, attn_sinks, w_out, norm2_g, w_gate, w_up, w_down, final_norm_g):
    depth = w_in.shape[0]
    assert depth == 1, "single-layer step only"
    nb, seq, _ = x_prompt.shape
    ns, dec_seq, _ = x_sample.shape
    assert dec_seq == 1

    g1 = norm1_g[0][None]
    g2 = norm2_g[0][None]
    gf = final_norm_g[None]
    cg = conv_norm_g[0][None]
    ag = attn_norm_g[0][None]
    cw = conv_w[0]
    sinks = attn_sinks[0]
    w_in_b = w_in[0].astype(BF16)

    cos_m, sin_m = _rope_tables(jnp.arange(N_META))
    zero_prev = jnp.zeros((CONV_W - 1, CONV_DIM), F32)
    _, _, kd_m, vd_m, _, _, u_m = _proj_call(meta_tokens[None], g1, w_in_b, cw, cg, cos_m, sin_m, zero_prev,
                                             tm=N_META, sample=False)
    pad = ((WINDOW - N_META, 0), (0, 0))
    kd_meta = jnp.pad(kd_m[0], pad)
    vd_meta = jnp.pad(vd_m[0], pad)

    cos_p, sin_p = _rope_tables(N_META + jnp.arange(seq))
    yc, q, kd, vd, k_tail, v_tail, u_tail, w_out_b, w_gate_b, w_up_b, w_down_b = _proj_call(
        x_prompt, g1, w_in_b, cw, cg, cos_p, sin_p, u_m[0], tm=512, sample=False, n_sub=2,
        cast=(w_out[0], w_gate[0], w_up[0], w_down[0]))
    ya = _attn_prompt(q, kd, vd, kd_meta, vd_meta, sinks, ag, n_sub=8)
    rp = nb * seq
    y_prompt = _mlp_call(x_prompt.reshape(rp, D_MODEL), yc.reshape(rp, CONV_DIM), ya.reshape(rp, ATTN_DIM),
                         w_out_b, g2, w_gate_b, w_up_b, w_down_b, gf, tm=512, tf=1024, n_sub=2, name="mlp_prompt")
    y_prompt = y_prompt.reshape(nb, seq, D_MODEL)
    new_k_prompt = k_tail.reshape(1, nb, WINDOW, N_KV_HEADS, HEAD_DIM)
    new_v_prompt = v_tail.reshape(1, nb, WINDOW, N_KV_HEADS, HEAD_DIM)
    new_conv_prompt = u_tail[None]

    xs = x_sample.reshape(1, ns, D_MODEL)
    cos_s, sin_s = _rope_tables(jnp.full((ns,), PAST_LEN))
    prev_s = state_conv[0].reshape(ns, (CONV_W - 1) * CONV_DIM)
    yc_s, q_s, k_st, v_st, k_s, v_s, u_s = _proj_call(xs, g1, w_in_b, cw, cg, cos_s, sin_s, prev_s, tm=ns, sample=True)
    cache_kt = jnp.transpose(cache_k[0], (0, 2, 3, 1)).reshape(ns, KV_DIM, WINDOW)
    cache_vt = jnp.transpose(cache_v[0], (0, 2, 3, 1)).reshape(ns, KV_DIM, WINDOW)
    ya_s, nk_s, nv_s = _attn_sample(q_s[0], k_s[0], v_s[0], k_st[0], v_st[0], cache_kt, cache_vt, sinks, ag, nseq=16)
    y_sample = _mlp_call(xs[0], yc_s[0], ya_s, w_out_b, g2, w_gate_b, w_up_b, w_down_b, gf,
                         tm=ns, tf=1024, n_sub=1, name="mlp_sample")
    y_sample = y_sample.reshape(ns, 1, D_MODEL)
    new_k_sample = jnp.transpose(nk_s.reshape(ns, N_KV_HEADS, HEAD_DIM, WINDOW), (0, 3, 1, 2))[None]
    new_v_sample = jnp.transpose(nv_s.reshape(ns, N_KV_HEADS, HEAD_DIM, WINDOW), (0, 3, 1, 2))[None]
    new_conv_sample = u_s.reshape(1, ns, CONV_W - 1, CONV_DIM)

    return (y_prompt, y_sample, new_k_prompt, new_v_prompt, new_conv_prompt,
            new_k_sample, new_v_sample, new_conv_sample)
```

```python
import functools

import jax
import jax.numpy as jnp
from jax import lax
from jax.experimental import pallas as pl
from jax.experimental.pallas import tpu as pltpu

D_MODEL = 2048
N_META = 16
CONV_DIM = 1024
CONV_W = 3
HEAD_DIM = 64
HALF_HEAD = HEAD_DIM // 2
ATTN_DIM = 1024
N_HEADS = 16
N_KV_HEADS = 4
GROUP = N_HEADS // N_KV_HEADS
KV_DIM = N_KV_HEADS * HEAD_DIM
WINDOW = 128
ROPE_THETA = 10000.0
RMS_EPS = 1e-6
NEG_INF = -1e30
PAST_LEN = 16384
LOG2_E = 1.4426950408889634

COL_BG = 0
COL_CG = CONV_DIM
COL_XC = 2 * CONV_DIM
COL_Q = 3 * CONV_DIM
COL_K = COL_Q + ATTN_DIM
COL_V = COL_K + KV_DIM
IN_COLS = COL_V + KV_DIM

LANES = 128
HEADS_PER_TILE = LANES // HEAD_DIM
VMEM_LIMIT = 60 * 1024 * 1024

F32 = jnp.float32
BF16 = jnp.bfloat16


def _rms(x, g):
    return x * lax.rsqrt(jnp.mean(x * x, axis=-1, keepdims=True) + RMS_EPS) * g


def _lane_tile(t, width):
    return jnp.concatenate([t] * (width // t.shape[-1]), axis=-1)


def _rope(t, cos, sin_signed):
    w = t.shape[-1]
    lane = lax.broadcasted_iota(jnp.int32, t.shape, 1)
    first_half = (lane & (HEAD_DIM - 1)) < HALF_HEAD
    swapped = jnp.where(first_half, pltpu.roll(t, w - HALF_HEAD, axis=1), pltpu.roll(t, HALF_HEAD, axis=1))
    return t * _lane_tile(cos, w) + swapped * _lane_tile(sin_signed, w)


def _dup_heads(pair_tile, half):
    lane = lax.broadcasted_iota(jnp.int32, pair_tile.shape, 1)
    swapped = pltpu.roll(pair_tile, HEAD_DIM, axis=1)
    if half == 0:
        return jnp.where(lane < HEAD_DIM, pair_tile, swapped)
    return jnp.where(lane < HEAD_DIM, swapped, pair_tile)


def _dup_all(t):
    tiles = []
    for g in range(N_KV_HEADS):
        pair, half = divmod(g, HEADS_PER_TILE)
        tiles.append(_dup_heads(t[:, pair * LANES:(pair + 1) * LANES], half))
    return jnp.concatenate(tiles, axis=-1).astype(BF16)


def _spread_heads(q):
    lane = lax.broadcasted_iota(jnp.int32, (q.shape[0], LANES), 1)
    tiles = []
    for h in range(N_HEADS):
        tile, half = divmod(h, HEADS_PER_TILE)
        qt = q[:, tile * LANES:(tile + 1) * LANES]
        keep = (lane < HEAD_DIM) if half == 0 else (lane >= HEAD_DIM)
        tiles.append(jnp.where(keep, qt, jnp.zeros_like(qt)))
    return jnp.concatenate(tiles, axis=-1)


def _head_block(t, h):
    return t[:, h * HEAD_DIM:(h + 1) * HEAD_DIM]


def _group_minor(t):
    return jnp.concatenate([_head_block(t, g * GROUP + i) for i in range(GROUP) for g in range(N_KV_HEADS)], axis=-1)


def _group_major(t):
    return jnp.concatenate([_head_block(t, i * N_KV_HEADS + g) for g in range(N_KV_HEADS) for i in range(GROUP)],
                           axis=-1)


def _conv_mix(u, p2, p1, cw):
    row = lax.broadcasted_iota(jnp.int32, u.shape, 0)
    u1 = jnp.where(row == 0, p1, pltpu.roll(u, 1, axis=0))
    u2 = jnp.where(row == 0, p2, jnp.where(row == 1, p1, pltpu.roll(u, 2, axis=0)))
    return cw[0:1, :] * u2 + cw[1:2, :] * u1 + cw[2:3, :] * u


Q_SCALE = HEAD_DIM ** -0.5 * LOG2_E

N_PROJ_IN = 8
N_PROJ_OUT = 7


def _proj_kernel(n_sub, *refs):
    x_ref, g1_ref, w_ref, cw_ref, cg_ref, cos_ref, sin_ref, prev_ref = refs[:N_PROJ_IN]
    n_cast = (len(refs) - 1 - N_PROJ_IN - N_PROJ_OUT) // 2
    cast_in = refs[N_PROJ_IN:N_PROJ_IN + n_cast]
    yc_ref, q_ref, kd_ref, vd_ref, kt_ref, vt_ref, u_ref = refs[N_PROJ_IN + n_cast:N_PROJ_IN + n_cast + N_PROJ_OUT]
    cast_out = refs[N_PROJ_IN + n_cast + N_PROJ_OUT:-1]
    carry_ref = refs[-1]

    ts = x_ref.shape[1] // n_sub
    cw = cw_ref[...]

    @pl.when(pl.program_id(1) == 0)
    def _():
        carry_ref[...] = prev_ref[...]

    p2 = carry_ref[0:1, :]
    p1 = carry_ref[1:2, :]

    for t in range(n_sub):
        rows = slice(t * ts, (t + 1) * ts)
        hn = _rms(x_ref[0, rows, :], g1_ref[...]).astype(BF16)

        def section(lo, width):
            return jnp.dot(hn, w_ref[:, lo:lo + width], preferred_element_type=F32)

        u = section(COL_CG, CONV_DIM) * section(COL_XC, CONV_DIM)
        cy = _conv_mix(u, p2, p1, cw)
        p2 = u[ts - 2:ts - 1, :]
        p1 = u[ts - 1:, :]
        yc_ref[0, rows, :] = _rms(section(COL_BG, CONV_DIM) * cy, cg_ref[...]).astype(BF16)

        cos = cos_ref[rows, :]
        sin = sin_ref[rows, :]
        q = _rope(section(COL_Q, ATTN_DIM), cos, sin) * Q_SCALE
        k = _rope(section(COL_K, KV_DIM), cos, sin)
        v = section(COL_V, KV_DIM)
        q_ref[0, rows, :] = _spread_heads(q.astype(BF16))
        kd_ref[0, rows, :] = _dup_all(k)
        vd_ref[0, rows, :] = _dup_all(v)

        if t == 0:
            for src, dst in zip(cast_in, cast_out):
                dst[...] = src[...].astype(BF16)

    tail = jnp.concatenate([p2, p1], axis=0)
    carry_ref[...] = tail
    u_ref[0] = tail
    tail_rows = kt_ref.shape[1]
    kt_ref[0] = k[ts - tail_rows:, :]
    vt_ref[0] = v[ts - tail_rows:, :]


def _proj_call(x, g1, w_in, conv_w, conv_g, cos, sin, prev, *, tm, n_sub, cast=()):
    nb, s, _ = x.shape
    n_steps = nb * (s // tm)
    slab = lambda b, i: (b * (s // tm) + i, 0)
    cast_specs = [pl.BlockSpec((w.shape[0] // n_steps, w.shape[1]), slab) for w in cast]
    assert all(w.shape[0] % (n_steps * 16) == 0 for w in cast)
    assert tm // n_sub >= WINDOW
    kd_cols = N_KV_HEADS * LANES
    q_cols = N_HEADS * LANES
    const = lambda b, i: (0, 0)
    rows = lambda b, i: (b, i, 0)
    per_seq = lambda b, i: (b, 0, 0)
    return pl.pallas_call(
        functools.partial(_proj_kernel, n_sub),
        grid=(nb, s // tm),
        in_specs=[
            pl.BlockSpec((1, tm, D_MODEL), rows),
            pl.BlockSpec((1, D_MODEL), const),
            pl.BlockSpec((D_MODEL, IN_COLS), const, pipeline_mode=pl.Buffered(1)),
            pl.BlockSpec((CONV_W, CONV_DIM), const),
            pl.BlockSpec((1, CONV_DIM), const),
            pl.BlockSpec((tm, LANES), lambda b, i: (i, 0)),
            pl.BlockSpec((tm, LANES), lambda b, i: (i, 0)),
            pl.BlockSpec((CONV_W - 1, CONV_DIM), const),
            *cast_specs,
        ],
        out_specs=[
            pl.BlockSpec((1, tm, CONV_DIM), rows),
            pl.BlockSpec((1, tm, q_cols), rows),
            pl.BlockSpec((1, tm, kd_cols), rows),
            pl.BlockSpec((1, tm, kd_cols), rows),
            pl.BlockSpec((1, WINDOW, KV_DIM), per_seq),
            pl.BlockSpec((1, WINDOW, KV_DIM), per_seq),
            pl.BlockSpec((1, CONV_W - 1, CONV_DIM), per_seq),
            *cast_specs,
        ],
        out_shape=[
            jax.ShapeDtypeStruct((nb, s, CONV_DIM), BF16),
            jax.ShapeDtypeStruct((nb, s, q_cols), BF16),
            jax.ShapeDtypeStruct((nb, s, kd_cols), BF16),
            jax.ShapeDtypeStruct((nb, s, kd_cols), BF16),
            jax.ShapeDtypeStruct((nb, WINDOW, KV_DIM), F32),
            jax.ShapeDtypeStruct((nb, WINDOW, KV_DIM), F32),
            jax.ShapeDtypeStruct((nb, CONV_W - 1, CONV_DIM), F32),
            *[jax.ShapeDtypeStruct(w.shape, BF16) for w in cast],
        ],
        scratch_shapes=[pltpu.VMEM((CONV_W - 1, CONV_DIM), F32)],
        compiler_params=pltpu.CompilerParams(
            dimension_semantics=("arbitrary", "arbitrary"), vmem_limit_bytes=VMEM_LIMIT),
        name="proj_prompt",
    )(x, g1, w_in, conv_w, conv_g, cos, sin, prev, *cast)


SMALL_COLS = 512


def _small_proj_kernel(xm_ref, xs_ref, g1_ref, w_ref, cw_ref, cg_ref, cosm_ref, sinm_ref, coss_ref, sins_ref, prev_ref,
                       wb_ref, kdm_ref, vdm_ref, um_ref, yc_ref, q_ref, kst_ref, vst_ref, ks_ref, vs_ref, us_ref,
                       hn_scr, z_scr):
    j = pl.program_id(0)
    n_meta = xm_ref.shape[0]
    n_all = hn_scr.shape[0]

    @pl.when(j == 0)
    def _():
        hn_scr[:n_meta, :] = _rms(xm_ref[...], g1_ref[...]).astype(BF16)
        hn_scr[n_meta:, :] = _rms(xs_ref[...], g1_ref[...]).astype(BF16)

    wb = w_ref[...].astype(BF16)
    wb_ref[...] = wb
    z_scr[j] = jnp.dot(hn_scr[...], wb, preferred_element_type=F32)

    @pl.when(j == pl.num_programs(0) - 1)
    def _():
        def section(rows, lo, width):
            parts, col = [], lo
            while col < lo + width:
                blk, off = divmod(col, SMALL_COLS)
                take = min(SMALL_COLS - off, lo + width - col)
                parts.append(z_scr[blk, rows, off:off + take])
                col += take
            return parts[0] if len(parts) == 1 else jnp.concatenate(parts, axis=-1)

        cw = cw_ref[...]
        rows = slice(0, n_meta)
        u = section(rows, COL_CG, CONV_DIM) * section(rows, COL_XC, CONV_DIM)
        um_ref[...] = u[n_meta - (CONV_W - 1):, :]
        k = _rope(section(rows, COL_K, KV_DIM), cosm_ref[...], sinm_ref[...])
        v = section(rows, COL_V, KV_DIM)
        pad = jnp.zeros((kdm_ref.shape[0] - n_meta, kdm_ref.shape[1]), BF16)
        kdm_ref[...] = jnp.concatenate([pad, _dup_all(k)], axis=0)
        vdm_ref[...] = jnp.concatenate([pad, _dup_all(v)], axis=0)

        rows = slice(n_meta, n_all)
        u = section(rows, COL_CG, CONV_DIM) * section(rows, COL_XC, CONV_DIM)
        u2 = prev_ref[:, :CONV_DIM]
        u1 = prev_ref[:, CONV_DIM:]
        us_ref[...] = jnp.concatenate([u1, u], axis=-1)
        cy = cw[0:1, :] * u2 + cw[1:2, :] * u1 + cw[2:3, :] * u
        yc_ref[...] = _rms(section(rows, COL_BG, CONV_DIM) * cy, cg_ref[...]).astype(BF16)
        cos = coss_ref[...]
        sin = sins_ref[...]
        q = _rope(section(rows, COL_Q, ATTN_DIM), cos, sin) * Q_SCALE
        k = _rope(section(rows, COL_K, KV_DIM), cos, sin)
        v = section(rows, COL_V, KV_DIM)
        q_ref[...] = _group_minor(q).astype(BF16)
        ks_ref[...] = k
        vs_ref[...] = v
        kst_ref[...] = k.T
        vst_ref[...] = v.T


def _small_proj_call(x_meta, x_sample, g1, w_in, conv_w, conv_g, cos_m, sin_m, cos_s, sin_s, prev_s):
    n_meta = x_meta.shape[0]
    ns = x_sample.shape[0]
    n_blocks = IN_COLS // SMALL_COLS
    kd_cols = N_KV_HEADS * LANES
    const = lambda j: (0, 0)
    full = lambda shape: pl.BlockSpec(shape, const)
    w_spec = pl.BlockSpec((D_MODEL, SMALL_COLS), lambda j: (0, j))
    out_shapes = [
        ((D_MODEL, IN_COLS), BF16),
        ((WINDOW, kd_cols), BF16),
        ((WINDOW, kd_cols), BF16),
        ((CONV_W - 1, CONV_DIM), F32),
        ((ns, CONV_DIM), BF16),
        ((ns, ATTN_DIM), BF16),
        ((KV_DIM, ns), F32),
        ((KV_DIM, ns), F32),
        ((ns, KV_DIM), F32),
        ((ns, KV_DIM), F32),
        ((ns, (CONV_W - 1) * CONV_DIM), F32),
    ]
    return pl.pallas_call(
        _small_proj_kernel,
        grid=(n_blocks,),
        in_specs=[
            full((n_meta, D_MODEL)), full((ns, D_MODEL)), full((1, D_MODEL)), w_spec,
            full((CONV_W, CONV_DIM)), full((1, CONV_DIM)),
            full((n_meta, LANES)), full((n_meta, LANES)), full((ns, LANES)), full((ns, LANES)),
            full((ns, (CONV_W - 1) * CONV_DIM)),
        ],
        out_specs=[w_spec] + [full(shape) for shape, _ in out_shapes[1:]],
        out_shape=[jax.ShapeDtypeStruct(shape, dtype) for shape, dtype in out_shapes],
        scratch_shapes=[pltpu.VMEM((n_meta + ns, D_MODEL), BF16),
                        pltpu.VMEM((n_blocks, n_meta + ns, SMALL_COLS), F32)],
        compiler_params=pltpu.CompilerParams(
            dimension_semantics=("arbitrary",), vmem_limit_bytes=VMEM_LIMIT),
        name="proj_small",
    )(x_meta, x_sample, g1, w_in, conv_w, conv_g, cos_m, sin_m, cos_s, sin_s, prev_s)


def _attn_prompt_kernel(sinks_ref, q_ref, kc_ref, kp_ref, vc_ref, vp_ref, km_ref, vm_ref, ag_ref,
                        ya_ref, o_scr):
    first = pl.program_id(1) == 0
    blk = WINDOW
    nk = 2 * blk
    n_sub = q_ref.shape[1] // blk

    qi = lax.broadcasted_iota(jnp.int32, (blk, nk), 0)
    kj = lax.broadcasted_iota(jnp.int32, (blk, nk), 1)
    band = (kj >= qi) & (kj <= qi + WINDOW)
    band_first = band & (kj >= jnp.where(first, blk - N_META, 0))

    lane = lax.broadcasted_iota(jnp.int32, (blk, LANES), 1)
    ones = jnp.ones((nk, LANES), BF16)
    for t in range(n_sub):
        rows = slice(t * blk, (t + 1) * blk)
        valid = band_first if t == 0 else band
        for g in range(N_KV_HEADS):
            cols = slice(g * LANES, (g + 1) * LANES)
            if t == 0:
                k_prev = jnp.where(first, km_ref[:, cols], kp_ref[0, :, cols])
                v_prev = jnp.where(first, vm_ref[:, cols], vp_ref[0, :, cols])
            else:
                k_prev = kc_ref[0, (t - 1) * blk:t * blk, cols]
                v_prev = vc_ref[0, (t - 1) * blk:t * blk, cols]
            kd = jnp.concatenate([k_prev, kc_ref[0, rows, cols]], axis=0)
            vd = jnp.concatenate([v_prev, vc_ref[0, rows, cols]], axis=0)
            v_rhs = jnp.concatenate([vd, ones], axis=1)
            for tile in range(g * GROUP // HEADS_PER_TILE, (g + 1) * GROUP // HEADS_PER_TILE):
                ov, m = [], []
                for h in (HEADS_PER_TILE * tile, HEADS_PER_TILE * tile + 1):
                    qm = q_ref[0, rows, h * LANES:(h + 1) * LANES]
                    s = lax.dot_general(qm, kd, (((1,), (1,)), ((), ())), preferred_element_type=F32)
                    s = jnp.where(valid, s, NEG_INF)
                    m.append(jnp.max(s, axis=-1, keepdims=True))
                    p = jnp.exp2(s - m[-1]).astype(BF16)
                    ov.append(jnp.dot(p, v_rhs, preferred_element_type=F32))
                low = lane < HEAD_DIM
                sink = jnp.where(low[:1], sinks_ref[HEADS_PER_TILE * tile], sinks_ref[HEADS_PER_TILE * tile + 1])
                denom = (jnp.where(low, ov[0][:, LANES:], ov[1][:, LANES:])
                         + jnp.exp2(sink * LOG2_E - jnp.where(low, m[0], m[1])))
                o_scr[rows, tile * LANES:(tile + 1) * LANES] = (
                    jnp.where(low, ov[0][:, :LANES], ov[1][:, :LANES]) / denom)
        ya_ref[0, rows, :] = _rms(o_scr[rows, :], ag_ref[...]).astype(BF16)


def _attn_prompt(q, kd, vd, kd_meta, vd_meta, sinks, attn_g, *, n_sub):
    nb, s, kd_cols = kd.shape
    blk = WINDOW
    tq = n_sub * blk
    cur = lambda b, j: (b, j, 0)
    prv = lambda b, j: (b, jnp.maximum(j * n_sub - 1, 0), 0)
    const = lambda b, j: (0, 0)
    return pl.pallas_call(
        _attn_prompt_kernel,
        grid=(nb, s // tq),
        in_specs=[
            pl.BlockSpec(memory_space=pltpu.SMEM),
            pl.BlockSpec((1, tq, N_HEADS * LANES), cur),
            pl.BlockSpec((1, tq, kd_cols), cur),
            pl.BlockSpec((1, blk, kd_cols), prv),
            pl.BlockSpec((1, tq, kd_cols), cur),
            pl.BlockSpec((1, blk, kd_cols), prv),
            pl.BlockSpec((blk, kd_cols), const),
            pl.BlockSpec((blk, kd_cols), const),
            pl.BlockSpec((1, ATTN_DIM), const),
        ],
        out_specs=pl.BlockSpec((1, tq, ATTN_DIM), cur),
        out_shape=jax.ShapeDtypeStruct((nb, s, ATTN_DIM), BF16),
        scratch_shapes=[pltpu.VMEM((tq, ATTN_DIM), F32)],
        compiler_params=pltpu.CompilerParams(
            dimension_semantics=("arbitrary", "arbitrary"), vmem_limit_bytes=VMEM_LIMIT),
        name="attn_prompt",
    )(sinks, q, kd, kd, vd, vd, kd_meta, vd_meta, attn_g)


def _attn_sample_kernel(sinks_ref, q_ref, kn_ref, vn_ref, knt_ref, vnt_ref, ck_ref, cv_ref, ag_ref,
                        ya_ref, nk_ref, nv_ref):
    nseq = q_ref.shape[0]
    base = pl.program_id(0) * nseq
    n_rows = N_HEADS * nseq
    qf = q_ref[...].astype(F32)
    kv_of_lane = lax.broadcasted_iota(jnp.int32, (nseq, KV_DIM), 1) // HEAD_DIM
    q_rows, sink_rows = [], []
    for r in range(N_HEADS):
        i, g = divmod(r, N_KV_HEADS)
        q_rows.append(jnp.where(kv_of_lane == g, qf[:, i * KV_DIM:(i + 1) * KV_DIM], 0.0))
        sink_rows.append(jnp.full((nseq, 1), sinks_ref[g * GROUP + i] * LOG2_E, F32))
    q_all = jnp.concatenate(q_rows, axis=0)
    sink = jnp.concatenate(sink_rows, axis=0)
    q_bf = q_all.astype(BF16)

    seq_of_row = lax.broadcasted_iota(jnp.int32, (n_rows, WINDOW), 0) % nseq
    s_c = jnp.zeros((n_rows, WINDOW), F32)
    for n in range(nseq):
        s_n = jnp.dot(q_bf, ck_ref[n].astype(BF16), preferred_element_type=F32)
        s_c = jnp.where(seq_of_row == n, s_n, s_c)
    kn = jnp.concatenate([kn_ref[...]] * N_HEADS, axis=0)
    vn = jnp.concatenate([vn_ref[...]] * N_HEADS, axis=0)
    s_new = jnp.sum(q_all * kn, axis=-1, keepdims=True)
    m = jnp.maximum(jnp.maximum(jnp.max(s_c, axis=-1, keepdims=True), s_new), sink)
    p_c = jnp.exp2(s_c - m)
    p_new = jnp.exp2(s_new - m)
    denom = jnp.sum(p_c, axis=-1, keepdims=True) + p_new + jnp.exp2(sink - m)
    p_bf = p_c.astype(BF16)

    seq_of_row = lax.broadcasted_iota(jnp.int32, (n_rows, KV_DIM), 0) % nseq
    o = jnp.zeros((n_rows, KV_DIM), F32)
    for n in range(nseq):
        o_n = lax.dot_general(p_bf, cv_ref[n].astype(BF16), (((1,), (1,)), ((), ())), preferred_element_type=F32)
        o = jnp.where(seq_of_row == n, o_n, o)
    kv_of_row = (lax.broadcasted_iota(jnp.int32, (n_rows, KV_DIM), 0) // nseq) % N_KV_HEADS
    own = (lax.broadcasted_iota(jnp.int32, (n_rows, KV_DIM), 1) // HEAD_DIM) == kv_of_row
    o = jnp.where(own, (o + p_new * vn) / denom, 0.0)
    slabs = []
    for i in range(GROUP):
        lo = i * N_KV_HEADS * nseq
        slabs.append(sum(o[lo + g * nseq:lo + (g + 1) * nseq] for g in range(N_KV_HEADS)))
    ya_ref[...] = _rms(_group_major(jnp.concatenate(slabs, axis=-1)), ag_ref[...]).astype(BF16)

    key_lane = lax.broadcasted_iota(jnp.int32, (KV_DIM, WINDOW), 1)
    for n in range(nseq):
        is_seq = key_lane == base + n
        k_col = jnp.sum(jnp.where(is_seq, knt_ref[...], 0.0), axis=1, keepdims=True)
        v_col = jnp.sum(jnp.where(is_seq, vnt_ref[...], 0.0), axis=1, keepdims=True)
        nk_ref[n] = jnp.where(key_lane == WINDOW - 1, k_col, pltpu.roll(ck_ref[n], WINDOW - 1, axis=1))
        nv_ref[n] = jnp.where(key_lane == WINDOW - 1, v_col, pltpu.roll(cv_ref[n], WINDOW - 1, axis=1))


def _attn_sample(q, k_new, v_new, k_new_t, v_new_t, cache_kt, cache_vt, sinks, attn_g, *, nseq):
    n = q.shape[0]
    rows = lambda i: (i, 0)
    seqs = lambda i: (i, 0, 0)
    const = lambda i: (0, 0)
    return pl.pallas_call(
        _attn_sample_kernel,
        grid=(n // nseq,),
        in_specs=[
            pl.BlockSpec(memory_space=pltpu.SMEM),
            pl.BlockSpec((nseq, ATTN_DIM), rows),
            pl.BlockSpec((nseq, KV_DIM), rows),
            pl.BlockSpec((nseq, KV_DIM), rows),
            pl.BlockSpec((KV_DIM, n), const),
            pl.BlockSpec((KV_DIM, n), const),
            pl.BlockSpec((nseq, KV_DIM, WINDOW), seqs),
            pl.BlockSpec((nseq, KV_DIM, WINDOW), seqs),
            pl.BlockSpec((1, ATTN_DIM), const),
        ],
        out_specs=[
            pl.BlockSpec((nseq, ATTN_DIM), rows),
            pl.BlockSpec((nseq, KV_DIM, WINDOW), seqs),
            pl.BlockSpec((nseq, KV_DIM, WINDOW), seqs),
        ],
        out_shape=[
            jax.ShapeDtypeStruct((n, ATTN_DIM), BF16),
            jax.ShapeDtypeStruct((n, KV_DIM, WINDOW), F32),
            jax.ShapeDtypeStruct((n, KV_DIM, WINDOW), F32),
        ],
        compiler_params=pltpu.CompilerParams(
            dimension_semantics=("arbitrary",), vmem_limit_bytes=VMEM_LIMIT),
        name="attn_sample",
    )(sinks, q, k_new, v_new, k_new_t, v_new_t, cache_kt, cache_vt, attn_g)


def _mlp_kernel(chunks, n_sub, x_ref, yc_ref, ya_ref, wo_ref, g2_ref, wg_hbm, wu_hbm, wd_hbm, gf_ref,
                y_ref, hn_scr, wg_buf, wu_buf, wd_buf, sem):
    i = pl.program_id(0)
    n_tiles = pl.num_programs(0)
    n_slots = wg_buf.shape[0]
    n_chunks = len(chunks)

    def chunk_copies(c):
        slot = c % n_slots
        off, width = chunks[c]
        cols = pl.ds(off, width)
        return (pltpu.make_async_copy(wg_hbm.at[:, cols], wg_buf.at[slot, :, :width], sem.at[0, slot]),
                pltpu.make_async_copy(wu_hbm.at[:, cols], wu_buf.at[slot, :, :width], sem.at[1, slot]),
                pltpu.make_async_copy(wd_hbm.at[cols, :], wd_buf.at[slot, :width, :], sem.at[2, slot]))

    def start(c):
        for cp in chunk_copies(c):
            cp.start()

    def wait(c):
        for cp in chunk_copies(c):
            cp.wait()

    @pl.when(i == 0)
    def _():
        start(0)

    ts = y_ref.shape[0] // n_sub
    sub_tiles = tuple(slice(t * ts, (t + 1) * ts) for t in range(n_sub))
    for rows in sub_tiles:
        mix = (jnp.dot(yc_ref[rows, :], wo_ref[:CONV_DIM, :], preferred_element_type=F32)
               + jnp.dot(ya_ref[rows, :], wo_ref[CONV_DIM:, :], preferred_element_type=F32))
        h = x_ref[rows, :] + mix
        y_ref[rows, :] = h
        hn_scr[rows, :] = _rms(h, g2_ref[...]).astype(BF16)

    for c in range(n_chunks):
        if c + 1 < n_chunks:
            start(c + 1)
        else:
            @pl.when(i + 1 < n_tiles)
            def _():
                start(0)
        wait(c)
        slot = c % n_slots
        width = chunks[c][1]
        for rows in sub_tiles:
            hn = hn_scr[rows, :]
            gate = jnp.dot(hn, wg_buf[slot, :, :width], preferred_element_type=F32)
            up = jnp.dot(hn, wu_buf[slot, :, :width], preferred_element_type=F32)
            act = (gate * jax.nn.sigmoid(gate) * up).astype(BF16)
            y_ref[rows, :] += jnp.dot(act, wd_buf[slot, :width, :], preferred_element_type=F32)

    for rows in sub_tiles:
        y_ref[rows, :] = _rms(y_ref[rows, :], gf_ref[...])


MLP_WEIGHT_SLOTS = 2


def _mlp_chunks(d_ff, tf):
    bounds = list(range(0, d_ff, tf)) + [d_ff]
    return tuple((lo, hi - lo) for lo, hi in zip(bounds[:-1], bounds[1:]))


def _mlp_call(x, yc, ya, w_out, g2, w_gate, w_up, w_down, gf, *, tm, tf, n_sub, name):
    r = x.shape[0]
    chunks = _mlp_chunks(w_gate.shape[1], tf)
    assert len(chunks) % MLP_WEIGHT_SLOTS == 0 and all(w % LANES == 0 for _, w in chunks)
    rows = lambda i: (i, 0)
    const = lambda i: (0, 0)
    return pl.pallas_call(
        functools.partial(_mlp_kernel, chunks, n_sub),
        grid=(r // tm,),
        in_specs=[
            pl.BlockSpec((tm, D_MODEL), rows),
            pl.BlockSpec((tm, CONV_DIM), rows),
            pl.BlockSpec((tm, ATTN_DIM), rows),
            pl.BlockSpec((D_MODEL, D_MODEL), const, pipeline_mode=pl.Buffered(1)),
            pl.BlockSpec((1, D_MODEL), const),
            pl.BlockSpec(memory_space=pl.ANY),
            pl.BlockSpec(memory_space=pl.ANY),
            pl.BlockSpec(memory_space=pl.ANY),
            pl.BlockSpec((1, D_MODEL), const),
        ],
        out_specs=pl.BlockSpec((tm, D_MODEL), rows),
        out_shape=jax.ShapeDtypeStruct((r, D_MODEL), F32),
        scratch_shapes=[
            pltpu.VMEM((tm, D_MODEL), BF16),
            pltpu.VMEM((MLP_WEIGHT_SLOTS, D_MODEL, tf), BF16),
            pltpu.VMEM((MLP_WEIGHT_SLOTS, D_MODEL, tf), BF16),
            pltpu.VMEM((MLP_WEIGHT_SLOTS, tf, D_MODEL), BF16),
            pltpu.SemaphoreType.DMA((3, MLP_WEIGHT_SLOTS)),
        ],
        compiler_params=pltpu.CompilerParams(
            dimension_semantics=("arbitrary",), vmem_limit_bytes=VMEM_LIMIT),
        name=name,
    )(x, yc, ya, w_out, g2, w_gate, w_up, w_down, gf)


def _rope_tables(pos):
    inv = ROPE_THETA ** (-jnp.arange(HALF_HEAD, dtype=F32) / HALF_HEAD)
    ang = pos.astype(F32)[:, None] * inv[None, :]
    cos = jnp.cos(ang)
    sin = jnp.sin(ang)
    reps = LANES // HEAD_DIM
    return (jnp.tile(jnp.concatenate([cos, cos], axis=-1), (1, reps)),
            jnp.tile(jnp.concatenate([-sin, sin], axis=-1), (1, reps)))


def kernel(x_prompt, x_sample, cache_k, cache_v, state_conv, meta_tokens, norm1_g, w_in, conv_w, conv_norm_g,
           attn_norm_g, attn_sinks, w_out, norm2_g, w_gate, w_up, w_down, final_norm_g):
    depth = w_in.shape[0]
    assert depth == 1, "single-layer step only"
    nb, seq, _ = x_prompt.shape
    ns, dec_seq, _ = x_sample.shape
    assert dec_seq == 1

    g1 = norm1_g[0][None]
    g2 = norm2_g[0][None]
    gf = final_norm_g[None]
    cg = conv_norm_g[0][None]
    ag = attn_norm_g[0][None]
    cw = conv_w[0]
    sinks = attn_sinks[0]
    xs = x_sample.reshape(ns, D_MODEL)
    cos_m, sin_m = _rope_tables(jnp.arange(N_META))
    cos_s, sin_s = _rope_tables(jnp.full((ns,), PAST_LEN))
    prev_s = state_conv[0].reshape(ns, (CONV_W - 1) * CONV_DIM)
    w_in_b, kd_meta, vd_meta, u_m, yc_s, q_s, k_st, v_st, k_s, v_s, u_s = _small_proj_call(
        meta_tokens, xs, g1, w_in[0], cw, cg, cos_m, sin_m, cos_s, sin_s, prev_s)

    cos_p, sin_p = _rope_tables(N_META + jnp.arange(seq))
    yc, q, kd, vd, k_tail, v_tail, u_tail, w_out_b, w_gate_b, w_up_b, w_down_b = _proj_call(
        x_prompt, g1, w_in_b, cw, cg, cos_p, sin_p, u_m, tm=512, n_sub=2,
        cast=(w_out[0], w_gate[0], w_up[0], w_down[0]))
    ya = _attn_prompt(q, kd, vd, kd_meta, vd_meta, sinks, ag, n_sub=8)
    rp = nb * seq
    y_prompt = _mlp_call(x_prompt.reshape(rp, D_MODEL), yc.reshape(rp, CONV_DIM), ya.reshape(rp, ATTN_DIM),
                         w_out_b, g2, w_gate_b, w_up_b, w_down_b, gf, tm=512, tf=1024, n_sub=1, name="mlp_prompt")
    y_prompt = y_prompt.reshape(nb, seq, D_MODEL)
    new_k_prompt = k_tail.reshape(1, nb, WINDOW, N_KV_HEADS, HEAD_DIM)
    new_v_prompt = v_tail.reshape(1, nb, WINDOW, N_KV_HEADS, HEAD_DIM)
    new_conv_prompt = u_tail[None]

    cache_kt = jnp.transpose(cache_k[0], (0, 2, 3, 1)).reshape(ns, KV_DIM, WINDOW)
    cache_vt = jnp.transpose(cache_v[0], (0, 2, 3, 1)).reshape(ns, KV_DIM, WINDOW)
    ya_s, nk_s, nv_s = _attn_sample(q_s, k_s, v_s, k_st, v_st, cache_kt, cache_vt, sinks, ag, nseq=16)
    y_sample = _mlp_call(xs, yc_s, ya_s, w_out_b, g2, w_gate_b, w_up_b, w_down_b, gf,
                         tm=ns, tf=1024, n_sub=1, name="mlp_sample")
    y_sample = y_sample.reshape(ns, 1, D_MODEL)
    new_k_sample = jnp.transpose(nk_s.reshape(ns, N_KV_HEADS, HEAD_DIM, WINDOW), (0, 3, 1, 2))[None]
    new_v_sample = jnp.transpose(nv_s.reshape(ns, N_KV_HEADS, HEAD_DIM, WINDOW), (0, 3, 1, 2))[None]
    new_conv_sample = u_s.reshape(1, ns, CONV_W - 1, CONV_DIM)

    return (y_prompt, y_sample, new_k_prompt, new_v_prompt, new_conv_prompt,
            new_k_sample, new_v_sample, new_conv_sample)
```

```python
import functools

import jax
import jax.numpy as jnp
from jax import lax
from jax.experimental import pallas as pl
from jax.experimental.pallas import tpu as pltpu

D_MODEL = 2048
N_META = 16
CONV_DIM = 1024
CONV_W = 3
HEAD_DIM = 64
HALF_HEAD = HEAD_DIM // 2
ATTN_DIM = 1024
N_HEADS = 16
N_KV_HEADS = 4
GROUP = N_HEADS // N_KV_HEADS
KV_DIM = N_KV_HEADS * HEAD_DIM
WINDOW = 128
ROPE_THETA = 10000.0
RMS_EPS = 1e-6
NEG_INF = -1e30
PAST_LEN = 16384
LOG2_E = 1.4426950408889634

COL_BG = 0
COL_CG = CONV_DIM
COL_XC = 2 * CONV_DIM
COL_Q = 3 * CONV_DIM
COL_K = COL_Q + ATTN_DIM
COL_V = COL_K + KV_DIM
IN_COLS = COL_V + KV_DIM

LANES = 128
HEADS_PER_TILE = LANES // HEAD_DIM
VMEM_LIMIT = 60 * 1024 * 1024

F32 = jnp.float32
BF16 = jnp.bfloat16


def _rms(x, g):
    return x * lax.rsqrt(jnp.mean(x * x, axis=-1, keepdims=True) + RMS_EPS) * g


def _lane_tile(t, width):
    return jnp.concatenate([t] * (width // t.shape[-1]), axis=-1)


def _rope(t, cos, sin_signed):
    w = t.shape[-1]
    lane = lax.broadcasted_iota(jnp.int32, t.shape, 1)
    first_half = (lane & (HEAD_DIM - 1)) < HALF_HEAD
    swapped = jnp.where(first_half, pltpu.roll(t, w - HALF_HEAD, axis=1), pltpu.roll(t, HALF_HEAD, axis=1))
    return t * _lane_tile(cos, w) + swapped * _lane_tile(sin_signed, w)


def _dup_heads(pair_tile, half):
    lane = lax.broadcasted_iota(jnp.int32, pair_tile.shape, 1)
    swapped = pltpu.roll(pair_tile, HEAD_DIM, axis=1)
    if half == 0:
        return jnp.where(lane < HEAD_DIM, pair_tile, swapped)
    return jnp.where(lane < HEAD_DIM, swapped, pair_tile)


def _dup_all(t):
    tiles = []
    for g in range(N_KV_HEADS):
        pair, half = divmod(g, HEADS_PER_TILE)
        tiles.append(_dup_heads(t[:, pair * LANES:(pair + 1) * LANES], half))
    return jnp.concatenate(tiles, axis=-1).astype(BF16)


def _spread_heads(q):
    lane = lax.broadcasted_iota(jnp.int32, (q.shape[0], LANES), 1)
    tiles = []
    for h in range(N_HEADS):
        tile, half = divmod(h, HEADS_PER_TILE)
        qt = q[:, tile * LANES:(tile + 1) * LANES]
        keep = (lane < HEAD_DIM) if half == 0 else (lane >= HEAD_DIM)
        tiles.append(jnp.where(keep, qt, jnp.zeros_like(qt)))
    return jnp.concatenate(tiles, axis=-1)


def _head_block(t, h):
    return t[:, h * HEAD_DIM:(h + 1) * HEAD_DIM]


def _group_minor(t):
    return jnp.concatenate([_head_block(t, g * GROUP + i) for i in range(GROUP) for g in range(N_KV_HEADS)], axis=-1)


def _group_major(t):
    return jnp.concatenate([_head_block(t, i * N_KV_HEADS + g) for g in range(N_KV_HEADS) for i in range(GROUP)],
                           axis=-1)


def _conv_mix(u, p2, p1, cw):
    row = lax.broadcasted_iota(jnp.int32, u.shape, 0)
    u1 = jnp.where(row == 0, p1, pltpu.roll(u, 1, axis=0))
    u2 = jnp.where(row == 0, p2, jnp.where(row == 1, p1, pltpu.roll(u, 2, axis=0)))
    return cw[0:1, :] * u2 + cw[1:2, :] * u1 + cw[2:3, :] * u


Q_SCALE = HEAD_DIM ** -0.5 * LOG2_E

N_PROJ_IN = 8
N_PROJ_OUT = 7


def _proj_kernel(n_sub, *refs):
    x_ref, g1_ref, w_ref, cw_ref, cg_ref, cos_ref, sin_ref, prev_ref = refs[:N_PROJ_IN]
    n_cast = (len(refs) - 1 - N_PROJ_IN - N_PROJ_OUT) // 2
    cast_in = refs[N_PROJ_IN:N_PROJ_IN + n_cast]
    yc_ref, q_ref, kd_ref, vd_ref, kt_ref, vt_ref, u_ref = refs[N_PROJ_IN + n_cast:N_PROJ_IN + n_cast + N_PROJ_OUT]
    cast_out = refs[N_PROJ_IN + n_cast + N_PROJ_OUT:-1]
    carry_ref = refs[-1]

    ts = x_ref.shape[1] // n_sub
    cw = cw_ref[...]

    @pl.when(pl.program_id(1) == 0)
    def _():
        carry_ref[...] = prev_ref[...]

    p2 = carry_ref[0:1, :]
    p1 = carry_ref[1:2, :]

    for t in range(n_sub):
        rows = slice(t * ts, (t + 1) * ts)
        hn = _rms(x_ref[0, rows, :], g1_ref[...]).astype(BF16)

        def section(lo, width):
            return jnp.dot(hn, w_ref[:, lo:lo + width], preferred_element_type=F32)

        u = section(COL_CG, CONV_DIM) * section(COL_XC, CONV_DIM)
        cy = _conv_mix(u, p2, p1, cw)
        p2 = u[ts - 2:ts - 1, :]
        p1 = u[ts - 1:, :]
        yc_ref[0, rows, :] = _rms(section(COL_BG, CONV_DIM) * cy, cg_ref[...]).astype(BF16)

        cos = cos_ref[rows, :]
        sin = sin_ref[rows, :]
        q = _rope(section(COL_Q, ATTN_DIM), cos, sin) * Q_SCALE
        k = _rope(section(COL_K, KV_DIM), cos, sin)
        v = section(COL_V, KV_DIM)
        q_ref[0, rows, :] = _spread_heads(q.astype(BF16))
        kd_ref[0, rows, :] = _dup_all(k)
        vd_ref[0, rows, :] = _dup_all(v)

        if t == 0:
            for src, dst in zip(cast_in, cast_out):
                dst[...] = src[...].astype(BF16)

    tail = jnp.concatenate([p2, p1], axis=0)
    carry_ref[...] = tail
    u_ref[0] = tail
    tail_rows = kt_ref.shape[1]
    kt_ref[0] = k[ts - tail_rows:, :]
    vt_ref[0] = v[ts - tail_rows:, :]


def _proj_call(x, g1, w_in, conv_w, conv_g, cos, sin, prev, *, tm, n_sub, cast=()):
    nb, s, _ = x.shape
    n_steps = nb * (s // tm)
    slab = lambda b, i: (b * (s // tm) + i, 0)
    cast_specs = [pl.BlockSpec((w.shape[0] // n_steps, w.shape[1]), slab) for w in cast]
    assert all(w.shape[0] % (n_steps * 16) == 0 for w in cast)
    assert tm // n_sub >= WINDOW
    kd_cols = N_KV_HEADS * LANES
    q_cols = N_HEADS * LANES
    const = lambda b, i: (0, 0)
    rows = lambda b, i: (b, i, 0)
    per_seq = lambda b, i: (b, 0, 0)
    return pl.pallas_call(
        functools.partial(_proj_kernel, n_sub),
        grid=(nb, s // tm),
        in_specs=[
            pl.BlockSpec((1, tm, D_MODEL), rows),
            pl.BlockSpec((1, D_MODEL), const),
            pl.BlockSpec((D_MODEL, IN_COLS), const, pipeline_mode=pl.Buffered(1)),
            pl.BlockSpec((CONV_W, CONV_DIM), const),
            pl.BlockSpec((1, CONV_DIM), const),
            pl.BlockSpec((tm, LANES), lambda b, i: (i, 0)),
            pl.BlockSpec((tm, LANES), lambda b, i: (i, 0)),
            pl.BlockSpec((CONV_W - 1, CONV_DIM), const),
            *cast_specs,
        ],
        out_specs=[
            pl.BlockSpec((1, tm, CONV_DIM), rows),
            pl.BlockSpec((1, tm, q_cols), rows),
            pl.BlockSpec((1, tm, kd_cols), rows),
            pl.BlockSpec((1, tm, kd_cols), rows),
            pl.BlockSpec((1, WINDOW, KV_DIM), per_seq),
            pl.BlockSpec((1, WINDOW, KV_DIM), per_seq),
            pl.BlockSpec((1, CONV_W - 1, CONV_DIM), per_seq),
            *cast_specs,
        ],
        out_shape=[
            jax.ShapeDtypeStruct((nb, s, CONV_DIM), BF16),
            jax.ShapeDtypeStruct((nb, s, q_cols), BF16),
            jax.ShapeDtypeStruct((nb, s, kd_cols), BF16),
            jax.ShapeDtypeStruct((nb, s, kd_cols), BF16),
            jax.ShapeDtypeStruct((nb, WINDOW, KV_DIM), F32),
            jax.ShapeDtypeStruct((nb, WINDOW, KV_DIM), F32),
            jax.ShapeDtypeStruct((nb, CONV_W - 1, CONV_DIM), F32),
            *[jax.ShapeDtypeStruct(w.shape, BF16) for w in cast],
        ],
        scratch_shapes=[pltpu.VMEM((CONV_W - 1, CONV_DIM), F32)],
        compiler_params=pltpu.CompilerParams(
            dimension_semantics=("arbitrary", "arbitrary"), vmem_limit_bytes=VMEM_LIMIT),
        name="proj_prompt",
    )(x, g1, w_in, conv_w, conv_g, cos, sin, prev, *cast)


SMALL_COLS = 512


def _small_proj_kernel(xm_ref, xs_ref, g1_ref, w_ref, cw_ref, cg_ref, cosm_ref, sinm_ref, coss_ref, sins_ref, prev_ref,
                       wb_ref, kdm_ref, vdm_ref, um_ref, yc_ref, q_ref, kst_ref, vst_ref, ks_ref, vs_ref, us_ref,
                       hn_scr, z_scr):
    j = pl.program_id(0)
    n_meta = xm_ref.shape[0]
    n_all = hn_scr.shape[0]

    @pl.when(j == 0)
    def _():
        hn_scr[:n_meta, :] = _rms(xm_ref[...], g1_ref[...]).astype(BF16)
        hn_scr[n_meta:, :] = _rms(xs_ref[...], g1_ref[...]).astype(BF16)

    wb = w_ref[...].astype(BF16)
    wb_ref[...] = wb
    z_scr[j] = jnp.dot(hn_scr[...], wb, preferred_element_type=F32)

    @pl.when(j == pl.num_programs(0) - 1)
    def _():
        def section(rows, lo, width):
            parts, col = [], lo
            while col < lo + width:
                blk, off = divmod(col, SMALL_COLS)
                take = min(SMALL_COLS - off, lo + width - col)
                parts.append(z_scr[blk, rows, off:off + take])
                col += take
            return parts[0] if len(parts) == 1 else jnp.concatenate(parts, axis=-1)

        cw = cw_ref[...]
        rows = slice(0, n_meta)
        u = section(rows, COL_CG, CONV_DIM) * section(rows, COL_XC, CONV_DIM)
        um_ref[...] = u[n_meta - (CONV_W - 1):, :]
        k = _rope(section(rows, COL_K, KV_DIM), cosm_ref[...], sinm_ref[...])
        v = section(rows, COL_V, KV_DIM)
        pad = jnp.zeros((kdm_ref.shape[0] - n_meta, kdm_ref.shape[1]), BF16)
        kdm_ref[...] = jnp.concatenate([pad, _dup_all(k)], axis=0)
        vdm_ref[...] = jnp.concatenate([pad, _dup_all(v)], axis=0)

        rows = slice(n_meta, n_all)
        u = section(rows, COL_CG, CONV_DIM) * section(rows, COL_XC, CONV_DIM)
        u2 = prev_ref[:, :CONV_DIM]
        u1 = prev_ref[:, CONV_DIM:]
        us_ref[...] = jnp.concatenate([u1, u], axis=-1)
        cy = cw[0:1, :] * u2 + cw[1:2, :] * u1 + cw[2:3, :] * u
        yc_ref[...] = _rms(section(rows, COL_BG, CONV_DIM) * cy, cg_ref[...]).astype(BF16)
        cos = coss_ref[...]
        sin = sins_ref[...]
        q = _rope(section(rows, COL_Q, ATTN_DIM), cos, sin) * Q_SCALE
        k = _rope(section(rows, COL_K, KV_DIM), cos, sin)
        v = section(rows, COL_V, KV_DIM)
        q_ref[...] = _group_minor(q).astype(BF16)
        ks_ref[...] = k
        vs_ref[...] = v
        kst_ref[...] = k.T
        vst_ref[...] = v.T


def _small_proj_call(x_meta, x_sample, g1, w_in, conv_w, conv_g, cos_m, sin_m, cos_s, sin_s, prev_s):
    n_meta = x_meta.shape[0]
    ns = x_sample.shape[0]
    n_blocks = IN_COLS // SMALL_COLS
    kd_cols = N_KV_HEADS * LANES
    const = lambda j: (0, 0)
    full = lambda shape: pl.BlockSpec(shape, const)
    w_spec = pl.BlockSpec((D_MODEL, SMALL_COLS), lambda j: (0, j))
    out_shapes = [
        ((D_MODEL, IN_COLS), BF16),
        ((WINDOW, kd_cols), BF16),
        ((WINDOW, kd_cols), BF16),
        ((CONV_W - 1, CONV_DIM), F32),
        ((ns, CONV_DIM), BF16),
        ((ns, ATTN_DIM), BF16),
        ((KV_DIM, ns), F32),
        ((KV_DIM, ns), F32),
        ((ns, KV_DIM), F32),
        ((ns, KV_DIM), F32),
        ((ns, (CONV_W - 1) * CONV_DIM), F32),
    ]
    return pl.pallas_call(
        _small_proj_kernel,
        grid=(n_blocks,),
        in_specs=[
            full((n_meta, D_MODEL)), full((ns, D_MODEL)), full((1, D_MODEL)), w_spec,
            full((CONV_W, CONV_DIM)), full((1, CONV_DIM)),
            full((n_meta, LANES)), full((n_meta, LANES)), full((ns, LANES)), full((ns, LANES)),
            full((ns, (CONV_W - 1) * CONV_DIM)),
        ],
        out_specs=[w_spec] + [full(shape) for shape, _ in out_shapes[1:]],
        out_shape=[jax.ShapeDtypeStruct(shape, dtype) for shape, dtype in out_shapes],
        scratch_shapes=[pltpu.VMEM((n_meta + ns, D_MODEL), BF16),
                        pltpu.VMEM((n_blocks, n_meta + ns, SMALL_COLS), F32)],
        compiler_params=pltpu.CompilerParams(
            dimension_semantics=("arbitrary",), vmem_limit_bytes=VMEM_LIMIT),
        name="proj_small",
    )(x_meta, x_sample, g1, w_in, conv_w, conv_g, cos_m, sin_m, cos_s, sin_s, prev_s)


def _attn_prompt_kernel(sinks_ref, q_ref, kc_ref, kp_ref, vc_ref, vp_ref, km_ref, vm_ref, ag_ref,
                        ya_ref, o_scr):
    first = pl.program_id(1) == 0
    blk = WINDOW
    nk = 2 * blk
    n_sub = q_ref.shape[1] // blk

    qi = lax.broadcasted_iota(jnp.int32, (blk, nk), 0)
    kj = lax.broadcasted_iota(jnp.int32, (blk, nk), 1)
    band = (kj >= qi) & (kj <= qi + WINDOW)
    band_first = band & (kj >= jnp.where(first, blk - N_META, 0))

    lane = lax.broadcasted_iota(jnp.int32, (blk, LANES), 1)
    ones = jnp.ones((nk, LANES), BF16)
    for t in range(n_sub):
        rows = slice(t * blk, (t + 1) * blk)
        valid = band_first if t == 0 else band
        for g in range(N_KV_HEADS):
            cols = slice(g * LANES, (g + 1) * LANES)
            if t == 0:
                k_prev = jnp.where(first, km_ref[:, cols], kp_ref[0, :, cols])
                v_prev = jnp.where(first, vm_ref[:, cols], vp_ref[0, :, cols])
            else:
                k_prev = kc_ref[0, (t - 1) * blk:t * blk, cols]
                v_prev = vc_ref[0, (t - 1) * blk:t * blk, cols]
            kd = jnp.concatenate([k_prev, kc_ref[0, rows, cols]], axis=0)
            vd = jnp.concatenate([v_prev, vc_ref[0, rows, cols]], axis=0)
            v_rhs = jnp.concatenate([vd, ones], axis=1)
            for tile in range(g * GROUP // HEADS_PER_TILE, (g + 1) * GROUP // HEADS_PER_TILE):
                ov, m = [], []
                for h in (HEADS_PER_TILE * tile, HEADS_PER_TILE * tile + 1):
                    qm = q_ref[0, rows, h * LANES:(h + 1) * LANES]
                    s = lax.dot_general(qm, kd, (((1,), (1,)), ((), ())), preferred_element_type=F32)
                    s = jnp.where(valid, s, NEG_INF)
                    m.append(jnp.max(s, axis=-1, keepdims=True))
                    p = jnp.exp2(s - m[-1]).astype(BF16)
                    ov.append(jnp.dot(p, v_rhs, preferred_element_type=F32))
                low = lane < HEAD_DIM
                sink = jnp.where(low[:1], sinks_ref[HEADS_PER_TILE * tile], sinks_ref[HEADS_PER_TILE * tile + 1])
                denom = (jnp.where(low, ov[0][:, LANES:], ov[1][:, LANES:])
                         + jnp.exp2(sink * LOG2_E - jnp.where(low, m[0], m[1])))
                o_scr[rows, tile * LANES:(tile + 1) * LANES] = (
                    jnp.where(low, ov[0][:, :LANES], ov[1][:, :LANES]) / denom)
        ya_ref[0, rows, :] = _rms(o_scr[rows, :], ag_ref[...]).astype(BF16)


def _attn_prompt(q, kd, vd, kd_meta, vd_meta, sinks, attn_g, *, n_sub):
    nb, s, kd_cols = kd.shape
    blk = WINDOW
    tq = n_sub * blk
    cur = lambda b, j: (b, j, 0)
    prv = lambda b, j: (b, jnp.maximum(j * n_sub - 1, 0), 0)
    const = lambda b, j: (0, 0)
    return pl.pallas_call(
        _attn_prompt_kernel,
        grid=(nb, s // tq),
        in_specs=[
            pl.BlockSpec(memory_space=pltpu.SMEM),
            pl.BlockSpec((1, tq, N_HEADS * LANES), cur),
            pl.BlockSpec((1, tq, kd_cols), cur),
            pl.BlockSpec((1, blk, kd_cols), prv),
            pl.BlockSpec((1, tq, kd_cols), cur),
            pl.BlockSpec((1, blk, kd_cols), prv),
            pl.BlockSpec((blk, kd_cols), const),
            pl.BlockSpec((blk, kd_cols), const),
            pl.BlockSpec((1, ATTN_DIM), const),
        ],
        out_specs=pl.BlockSpec((1, tq, ATTN_DIM), cur),
        out_shape=jax.ShapeDtypeStruct((nb, s, ATTN_DIM), BF16),
        scratch_shapes=[pltpu.VMEM((tq, ATTN_DIM), F32)],
        compiler_params=pltpu.CompilerParams(
            dimension_semantics=("arbitrary", "arbitrary"), vmem_limit_bytes=VMEM_LIMIT),
        name="attn_prompt",
    )(sinks, q, kd, kd, vd, vd, kd_meta, vd_meta, attn_g)


def _attn_sample_kernel(sinks_ref, q_ref, kn_ref, vn_ref, knt_ref, vnt_ref, ck_ref, cv_ref, ag_ref,
                        ya_ref, nk_ref, nv_ref):
    nseq = q_ref.shape[0]
    base = pl.program_id(0) * nseq
    n_rows = N_HEADS * nseq
    qf = q_ref[...].astype(F32)
    kv_of_lane = lax.broadcasted_iota(jnp.int32, (nseq, KV_DIM), 1) // HEAD_DIM
    q_rows, sink_rows = [], []
    for r in range(N_HEADS):
        i, g = divmod(r, N_KV_HEADS)
        q_rows.append(jnp.where(kv_of_lane == g, qf[:, i * KV_DIM:(i + 1) * KV_DIM], 0.0))
        sink_rows.append(jnp.full((nseq, 1), sinks_ref[g * GROUP + i] * LOG2_E, F32))
    q_all = jnp.concatenate(q_rows, axis=0)
    sink = jnp.concatenate(sink_rows, axis=0)
    q_bf = q_all.astype(BF16)

    key_lane = lax.broadcasted_iota(jnp.int32, (KV_DIM, WINDOW), 1)

    def shifted(cache, new_t, n):
        col = jnp.sum(jnp.where(key_lane == base + n, new_t, 0.0), axis=1, keepdims=True)
        return jnp.where(key_lane == WINDOW - 1, col, pltpu.roll(cache, WINDOW - 1, axis=1))

    seq_of_row = lax.broadcasted_iota(jnp.int32, (n_rows, WINDOW), 0) % nseq
    s_c = jnp.zeros((n_rows, WINDOW), F32)
    for n in range(nseq):
        kt = ck_ref[n]
        s_n = jnp.dot(q_bf, kt.astype(BF16), preferred_element_type=F32)
        s_c = jnp.where(seq_of_row == n, s_n, s_c)
        nk_ref[n] = shifted(kt, knt_ref[...], n)
    kn = jnp.concatenate([kn_ref[...]] * N_HEADS, axis=0)
    vn = jnp.concatenate([vn_ref[...]] * N_HEADS, axis=0)
    s_new = jnp.sum(q_all * kn, axis=-1, keepdims=True)
    m = jnp.maximum(jnp.maximum(jnp.max(s_c, axis=-1, keepdims=True), s_new), sink)
    p_c = jnp.exp2(s_c - m)
    p_new = jnp.exp2(s_new - m)
    denom = jnp.sum(p_c, axis=-1, keepdims=True) + p_new + jnp.exp2(sink - m)
    p_bf = p_c.astype(BF16)

    seq_of_row = lax.broadcasted_iota(jnp.int32, (n_rows, KV_DIM), 0) % nseq
    o = jnp.zeros((n_rows, KV_DIM), F32)
    for n in range(nseq):
        vt = cv_ref[n]
        o_n = lax.dot_general(p_bf, vt.astype(BF16), (((1,), (1,)), ((), ())), preferred_element_type=F32)
        o = jnp.where(seq_of_row == n, o_n, o)
        nv_ref[n] = shifted(vt, vnt_ref[...], n)
    kv_of_row = (lax.broadcasted_iota(jnp.int32, (n_rows, KV_DIM), 0) // nseq) % N_KV_HEADS
    own = (lax.broadcasted_iota(jnp.int32, (n_rows, KV_DIM), 1) // HEAD_DIM) == kv_of_row
    o = jnp.where(own, (o + p_new * vn) / denom, 0.0)
    slabs = []
    for i in range(GROUP):
        lo = i * N_KV_HEADS * nseq
        slabs.append(sum(o[lo + g * nseq:lo + (g + 1) * nseq] for g in range(N_KV_HEADS)))
    ya_ref[...] = _rms(_group_major(jnp.concatenate(slabs, axis=-1)), ag_ref[...]).astype(BF16)


def _attn_sample(q, k_new, v_new, k_new_t, v_new_t, cache_kt, cache_vt, sinks, attn_g, *, nseq):
    n = q.shape[0]
    rows = lambda i: (i, 0)
    seqs = lambda i: (i, 0, 0)
    const = lambda i: (0, 0)
    return pl.pallas_call(
        _attn_sample_kernel,
        grid=(n // nseq,),
        in_specs=[
            pl.BlockSpec(memory_space=pltpu.SMEM),
            pl.BlockSpec((nseq, ATTN_DIM), rows),
            pl.BlockSpec((nseq, KV_DIM), rows),
            pl.BlockSpec((nseq, KV_DIM), rows),
            pl.BlockSpec((KV_DIM, n), const),
            pl.BlockSpec((KV_DIM, n), const),
            pl.BlockSpec((nseq, KV_DIM, WINDOW), seqs),
            pl.BlockSpec((nseq, KV_DIM, WINDOW), seqs),
            pl.BlockSpec((1, ATTN_DIM), const),
        ],
        out_specs=[
            pl.BlockSpec((nseq, ATTN_DIM), rows),
            pl.BlockSpec((nseq, KV_DIM, WINDOW), seqs),
            pl.BlockSpec((nseq, KV_DIM, WINDOW), seqs),
        ],
        out_shape=[
            jax.ShapeDtypeStruct((n, ATTN_DIM), BF16),
            jax.ShapeDtypeStruct((n, KV_DIM, WINDOW), F32),
            jax.ShapeDtypeStruct((n, KV_DIM, WINDOW), F32),
        ],
        compiler_params=pltpu.CompilerParams(
            dimension_semantics=("arbitrary",), vmem_limit_bytes=VMEM_LIMIT),
        name="attn_sample",
    )(sinks, q, k_new, v_new, k_new_t, v_new_t, cache_kt, cache_vt, attn_g)


def _mlp_kernel(chunks, x_ref, yc_ref, ya_ref, wo_ref, g2_ref, wg_hbm, wu_hbm, wd_hbm, gf_ref,
                y_ref, hn_scr, wg_buf, wu_buf, wd_buf, sem):
    i = pl.program_id(0)
    n_tiles = pl.num_programs(0)
    n_slots = wg_buf.shape[0]
    n_chunks = len(chunks)
    ahead = n_slots - 1

    def chunk_copies(c):
        slot = c % n_slots
        off, width = chunks[c]
        cols = pl.ds(off, width)
        return (pltpu.make_async_copy(wg_hbm.at[:, cols], wg_buf.at[slot, :, :width], sem.at[0, slot]),
                pltpu.make_async_copy(wu_hbm.at[:, cols], wu_buf.at[slot, :, :width], sem.at[1, slot]),
                pltpu.make_async_copy(wd_hbm.at[cols, :], wd_buf.at[slot, :width, :], sem.at[2, slot]))

    def start(c):
        for cp in chunk_copies(c):
            cp.start()

    def wait(c):
        for cp in chunk_copies(c):
            cp.wait()

    @pl.when(i == 0)
    def _():
        for c in range(min(ahead, n_chunks)):
            start(c)

    mix = (jnp.dot(yc_ref[...], wo_ref[:CONV_DIM, :], preferred_element_type=F32)
           + jnp.dot(ya_ref[...], wo_ref[CONV_DIM:, :], preferred_element_type=F32))
    h = x_ref[...] + mix
    y_ref[...] = h
    hn_scr[...] = _rms(h, g2_ref[...]).astype(BF16)

    for c in range(n_chunks):
        nxt = c + ahead
        if nxt < n_chunks:
            start(nxt)
        else:
            @pl.when(i + 1 < n_tiles)
            def _():
                start(nxt - n_chunks)
        wait(c)
        slot = c % n_slots
        width = chunks[c][1]
        hn = hn_scr[...]
        gate = jnp.dot(hn, wg_buf[slot, :, :width], preferred_element_type=F32)
        up = jnp.dot(hn, wu_buf[slot, :, :width], preferred_element_type=F32)
        act = (gate * jax.nn.sigmoid(gate) * up).astype(BF16)
        y_ref[...] += jnp.dot(act, wd_buf[slot, :width, :], preferred_element_type=F32)

    y_ref[...] = _rms(y_ref[...], gf_ref[...])


def _mlp_chunks(d_ff, tf):
    bounds = list(range(0, d_ff, tf)) + [d_ff]
    return tuple((lo, hi - lo) for lo, hi in zip(bounds[:-1], bounds[1:]))


def _mlp_call(x, yc, ya, w_out, g2, w_gate, w_up, w_down, gf, *, tm, tf, n_slots, name):
    r = x.shape[0]
    chunks = _mlp_chunks(w_gate.shape[1], tf)
    assert r == tm or len(chunks) % n_slots == 0
    assert len(chunks) >= n_slots and all(w % LANES == 0 for _, w in chunks)
    rows = lambda i: (i, 0)
    const = lambda i: (0, 0)
    return pl.pallas_call(
        functools.partial(_mlp_kernel, chunks),
        grid=(r // tm,),
        in_specs=[
            pl.BlockSpec((tm, D_MODEL), rows),
            pl.BlockSpec((tm, CONV_DIM), rows),
            pl.BlockSpec((tm, ATTN_DIM), rows),
            pl.BlockSpec((D_MODEL, D_MODEL), const, pipeline_mode=pl.Buffered(1)),
            pl.BlockSpec((1, D_MODEL), const),
            pl.BlockSpec(memory_space=pl.ANY),
            pl.BlockSpec(memory_space=pl.ANY),
            pl.BlockSpec(memory_space=pl.ANY),
            pl.BlockSpec((1, D_MODEL), const),
        ],
        out_specs=pl.BlockSpec((tm, D_MODEL), rows),
        out_shape=jax.ShapeDtypeStruct((r, D_MODEL), F32),
        scratch_shapes=[
            pltpu.VMEM((tm, D_MODEL), BF16),
            pltpu.VMEM((n_slots, D_MODEL, tf), BF16),
            pltpu.VMEM((n_slots, D_MODEL, tf), BF16),
            pltpu.VMEM((n_slots, tf, D_MODEL), BF16),
            pltpu.SemaphoreType.DMA((3, n_slots)),
        ],
        compiler_params=pltpu.CompilerParams(
            dimension_semantics=("arbitrary",), vmem_limit_bytes=VMEM_LIMIT),
        name=name,
    )(x, yc, ya, w_out, g2, w_gate, w_up, w_down, gf)


def _rope_tables(pos):
    inv = ROPE_THETA ** (-jnp.arange(HALF_HEAD, dtype=F32) / HALF_HEAD)
    ang = pos.astype(F32)[:, None] * inv[None, :]
    cos = jnp.cos(ang)
    sin = jnp.sin(ang)
    reps = LANES // HEAD_DIM
    return (jnp.tile(jnp.concatenate([cos, cos], axis=-1), (1, reps)),
            jnp.tile(jnp.concatenate([-sin, sin], axis=-1), (1, reps)))


def kernel(x_prompt, x_sample, cache_k, cache_v, state_conv, meta_tokens, norm1_g, w_in, conv_w, conv_norm_g,
           attn_norm_g, attn_sinks, w_out, norm2_g, w_gate, w_up, w_down, final_norm_g):
    depth = w_in.shape[0]
    assert depth == 1, "single-layer step only"
    nb, seq, _ = x_prompt.shape
    ns, dec_seq, _ = x_sample.shape
    assert dec_seq == 1

    g1 = norm1_g[0][None]
    g2 = norm2_g[0][None]
    gf = final_norm_g[None]
    cg = conv_norm_g[0][None]
    ag = attn_norm_g[0][None]
    cw = conv_w[0]
    sinks = attn_sinks[0]
    xs = x_sample.reshape(ns, D_MODEL)
    cos_m, sin_m = _rope_tables(jnp.arange(N_META))
    cos_s, sin_s = _rope_tables(jnp.full((ns,), PAST_LEN))
    prev_s = state_conv[0].reshape(ns, (CONV_W - 1) * CONV_DIM)
    w_in_b, kd_meta, vd_meta, u_m, yc_s, q_s, k_st, v_st, k_s, v_s, u_s = _small_proj_call(
        meta_tokens, xs, g1, w_in[0], cw, cg, cos_m, sin_m, cos_s, sin_s, prev_s)

    cos_p, sin_p = _rope_tables(N_META + jnp.arange(seq))
    yc, q, kd, vd, k_tail, v_tail, u_tail, w_out_b, w_gate_b, w_up_b, w_down_b = _proj_call(
        x_prompt, g1, w_in_b, cw, cg, cos_p, sin_p, u_m, tm=512, n_sub=2,
        cast=(w_out[0], w_gate[0], w_up[0], w_down[0]))
    ya = _attn_prompt(q, kd, vd, kd_meta, vd_meta, sinks, ag, n_sub=8)
    rp = nb * seq
    y_prompt = _mlp_call(x_prompt.reshape(rp, D_MODEL), yc.reshape(rp, CONV_DIM), ya.reshape(rp, ATTN_DIM),
                         w_out_b, g2, w_gate_b, w_up_b, w_down_b, gf, tm=512, tf=1024, n_slots=2, name="mlp_prompt")
    y_prompt = y_prompt.reshape(nb, seq, D_MODEL)
    new_k_prompt = k_tail.reshape(1, nb, WINDOW, N_KV_HEADS, HEAD_DIM)
    new_v_prompt = v_tail.reshape(1, nb, WINDOW, N_KV_HEADS, HEAD_DIM)
    new_conv_prompt = u_tail[None]

    cache_kt = jnp.transpose(cache_k[0], (0, 2, 3, 1)).reshape(ns, KV_DIM, WINDOW)
    cache_vt = jnp.transpose(cache_v[0], (0, 2, 3, 1)).reshape(ns, KV_DIM, WINDOW)
    ya_s, nk_s, nv_s = _attn_sample(q_s, k_s, v_s, k_st, v_st, cache_kt, cache_vt, sinks, ag, nseq=16)
    y_sample = _mlp_call(xs, yc_s, ya_s, w_out_b, g2, w_gate_b, w_up_b, w_down_b, gf,
                         tm=ns, tf=512, n_slots=4, name="mlp_sample")
    y_sample = y_sample.reshape(ns, 1, D_MODEL)
    new_k_sample = jnp.transpose(nk_s.reshape(ns, N_KV_HEADS, HEAD_DIM, WINDOW), (0, 3, 1, 2))[None]
    new_v_sample = jnp.transpose(nv_s.reshape(ns, N_KV_HEADS, HEAD_DIM, WINDOW), (0, 3, 1, 2))[None]
    new_conv_sample = u_s.reshape(1, ns, CONV_W - 1, CONV_DIM)

    return (y_prompt, y_sample, new_k_prompt, new_v_prompt, new_conv_prompt,
            new_k_sample, new_v_sample, new_conv_sample)
```

```python
import functools

import jax
import jax.numpy as jnp
from jax import lax
from jax.experimental import pallas as pl
from jax.experimental.pallas import tpu as pltpu

D_MODEL = 2048
N_META = 16
CONV_DIM = 1024
CONV_W = 3
HEAD_DIM = 64
HALF_HEAD = HEAD_DIM // 2
ATTN_DIM = 1024
N_HEADS = 16
N_KV_HEADS = 4
GROUP = N_HEADS // N_KV_HEADS
KV_DIM = N_KV_HEADS * HEAD_DIM
WINDOW = 128
ROPE_THETA = 10000.0
RMS_EPS = 1e-6
NEG_INF = -1e30
PAST_LEN = 16384
LOG2_E = 1.4426950408889634

COL_BG = 0
COL_CG = CONV_DIM
COL_XC = 2 * CONV_DIM
COL_Q = 3 * CONV_DIM
COL_K = COL_Q + ATTN_DIM
COL_V = COL_K + KV_DIM
IN_COLS = COL_V + KV_DIM

LANES = 128
HEADS_PER_TILE = LANES // HEAD_DIM
VMEM_LIMIT = 62 * 1024 * 1024

F32 = jnp.float32
BF16 = jnp.bfloat16


def _rms(x, g):
    return x * lax.rsqrt(jnp.mean(x * x, axis=-1, keepdims=True) + RMS_EPS) * g


def _lane_tile(t, width):
    return jnp.concatenate([t] * (width // t.shape[-1]), axis=-1)


def _rope(t, cos, sin_signed):
    w = t.shape[-1]
    lane = lax.broadcasted_iota(jnp.int32, t.shape, 1)
    first_half = (lane & (HEAD_DIM - 1)) < HALF_HEAD
    swapped = jnp.where(first_half, pltpu.roll(t, w - HALF_HEAD, axis=1), pltpu.roll(t, HALF_HEAD, axis=1))
    return t * _lane_tile(cos, w) + swapped * _lane_tile(sin_signed, w)


def _dup_heads(pair_tile, half):
    lane = lax.broadcasted_iota(jnp.int32, pair_tile.shape, 1)
    swapped = pltpu.roll(pair_tile, HEAD_DIM, axis=1)
    if half == 0:
        return jnp.where(lane < HEAD_DIM, pair_tile, swapped)
    return jnp.where(lane < HEAD_DIM, swapped, pair_tile)


def _dup_all(t):
    tiles = []
    for g in range(N_KV_HEADS):
        pair, half = divmod(g, HEADS_PER_TILE)
        tiles.append(_dup_heads(t[:, pair * LANES:(pair + 1) * LANES], half))
    return jnp.concatenate(tiles, axis=-1).astype(BF16)


def _spread_heads(q):
    lane = lax.broadcasted_iota(jnp.int32, (q.shape[0], LANES), 1)
    tiles = []
    for h in range(N_HEADS):
        tile, half = divmod(h, HEADS_PER_TILE)
        qt = q[:, tile * LANES:(tile + 1) * LANES]
        keep = (lane < HEAD_DIM) if half == 0 else (lane >= HEAD_DIM)
        tiles.append(jnp.where(keep, qt, jnp.zeros_like(qt)))
    return jnp.concatenate(tiles, axis=-1)


def _head_block(t, h):
    return t[:, h * HEAD_DIM:(h + 1) * HEAD_DIM]


def _group_minor(t):
    return jnp.concatenate([_head_block(t, g * GROUP + i) for i in range(GROUP) for g in range(N_KV_HEADS)], axis=-1)


def _group_major(t):
    return jnp.concatenate([_head_block(t, i * N_KV_HEADS + g) for g in range(N_KV_HEADS) for i in range(GROUP)],
                           axis=-1)


def _conv_mix(u, p2, p1, cw):
    row = lax.broadcasted_iota(jnp.int32, u.shape, 0)
    u1 = jnp.where(row == 0, p1, pltpu.roll(u, 1, axis=0))
    u2 = jnp.where(row == 0, p2, jnp.where(row == 1, p1, pltpu.roll(u, 2, axis=0)))
    return cw[0:1, :] * u2 + cw[1:2, :] * u1 + cw[2:3, :] * u


Q_SCALE = HEAD_DIM ** -0.5 * LOG2_E

N_PROJ_IN = 8
N_PROJ_OUT = 7


def _proj_kernel(n_sub, *refs):
    x_ref, g1_ref, w_ref, cw_ref, cg_ref, cos_ref, sin_ref, prev_ref = refs[:N_PROJ_IN]
    n_cast = (len(refs) - 1 - N_PROJ_IN - N_PROJ_OUT) // 2
    cast_in = refs[N_PROJ_IN:N_PROJ_IN + n_cast]
    yc_ref, q_ref, kd_ref, vd_ref, kt_ref, vt_ref, u_ref = refs[N_PROJ_IN + n_cast:N_PROJ_IN + n_cast + N_PROJ_OUT]
    cast_out = refs[N_PROJ_IN + n_cast + N_PROJ_OUT:-1]
    carry_ref = refs[-1]

    ts = x_ref.shape[1] // n_sub
    cw = cw_ref[...]

    @pl.when(pl.program_id(1) == 0)
    def _():
        carry_ref[...] = prev_ref[...]

    p2 = carry_ref[0:1, :]
    p1 = carry_ref[1:2, :]

    for t in range(n_sub):
        rows = slice(t * ts, (t + 1) * ts)
        hn = _rms(x_ref[0, rows, :], g1_ref[...]).astype(BF16)

        def section(lo, width):
            return jnp.dot(hn, w_ref[:, lo:lo + width], preferred_element_type=F32)

        u = section(COL_CG, CONV_DIM) * section(COL_XC, CONV_DIM)
        cy = _conv_mix(u, p2, p1, cw)
        p2 = u[ts - 2:ts - 1, :]
        p1 = u[ts - 1:, :]
        yc_ref[0, rows, :] = _rms(section(COL_BG, CONV_DIM) * cy, cg_ref[...]).astype(BF16)

        cos = cos_ref[rows, :]
        sin = sin_ref[rows, :]
        q = _rope(section(COL_Q, ATTN_DIM), cos, sin) * Q_SCALE
        k = _rope(section(COL_K, KV_DIM), cos, sin)
        v = section(COL_V, KV_DIM)
        q_ref[0, rows, :] = _spread_heads(q.astype(BF16))
        kd_ref[0, rows, :] = _dup_all(k)
        vd_ref[0, rows, :] = _dup_all(v)

        if t == 0:
            for src, dst in zip(cast_in, cast_out):
                dst[...] = src[...].astype(BF16)

    tail = jnp.concatenate([p2, p1], axis=0)
    carry_ref[...] = tail
    u_ref[0] = tail
    tail_rows = kt_ref.shape[1]
    kt_ref[0] = k[ts - tail_rows:, :]
    vt_ref[0] = v[ts - tail_rows:, :]


def _proj_call(x, g1, w_in, conv_w, conv_g, cos, sin, prev, *, tm, n_sub, cast=()):
    nb, s, _ = x.shape
    n_steps = nb * (s // tm)
    slab = lambda b, i: (b * (s // tm) + i, 0)
    cast_specs = [pl.BlockSpec((w.shape[0] // n_steps, w.shape[1]), slab) for w in cast]
    assert all(w.shape[0] % (n_steps * 16) == 0 for w in cast)
    assert tm // n_sub >= WINDOW
    kd_cols = N_KV_HEADS * LANES
    q_cols = N_HEADS * LANES
    const = lambda b, i: (0, 0)
    rows = lambda b, i: (b, i, 0)
    per_seq = lambda b, i: (b, 0, 0)
    return pl.pallas_call(
        functools.partial(_proj_kernel, n_sub),
        grid=(nb, s // tm),
        in_specs=[
            pl.BlockSpec((1, tm, D_MODEL), rows),
            pl.BlockSpec((1, D_MODEL), const),
            pl.BlockSpec((D_MODEL, IN_COLS), const, pipeline_mode=pl.Buffered(1)),
            pl.BlockSpec((CONV_W, CONV_DIM), const),
            pl.BlockSpec((1, CONV_DIM), const),
            pl.BlockSpec((tm, LANES), lambda b, i: (i, 0)),
            pl.BlockSpec((tm, LANES), lambda b, i: (i, 0)),
            pl.BlockSpec((CONV_W - 1, CONV_DIM), const),
            *cast_specs,
        ],
        out_specs=[
            pl.BlockSpec((1, tm, CONV_DIM), rows),
            pl.BlockSpec((1, tm, q_cols), rows),
            pl.BlockSpec((1, tm, kd_cols), rows),
            pl.BlockSpec((1, tm, kd_cols), rows),
            pl.BlockSpec((1, WINDOW, KV_DIM), per_seq),
            pl.BlockSpec((1, WINDOW, KV_DIM), per_seq),
            pl.BlockSpec((1, CONV_W - 1, CONV_DIM), per_seq),
            *cast_specs,
        ],
        out_shape=[
            jax.ShapeDtypeStruct((nb, s, CONV_DIM), BF16),
            jax.ShapeDtypeStruct((nb, s, q_cols), BF16),
            jax.ShapeDtypeStruct((nb, s, kd_cols), BF16),
            jax.ShapeDtypeStruct((nb, s, kd_cols), BF16),
            jax.ShapeDtypeStruct((nb, WINDOW, KV_DIM), F32),
            jax.ShapeDtypeStruct((nb, WINDOW, KV_DIM), F32),
            jax.ShapeDtypeStruct((nb, CONV_W - 1, CONV_DIM), F32),
            *[jax.ShapeDtypeStruct(w.shape, BF16) for w in cast],
        ],
        scratch_shapes=[pltpu.VMEM((CONV_W - 1, CONV_DIM), F32)],
        compiler_params=pltpu.CompilerParams(
            dimension_semantics=("arbitrary", "arbitrary"), vmem_limit_bytes=VMEM_LIMIT),
        name="proj_prompt",
    )(x, g1, w_in, conv_w, conv_g, cos, sin, prev, *cast)


SMALL_COLS = 512


def _small_proj_kernel(xm_ref, xs_ref, g1_ref, w_ref, cw_ref, cg_ref, cosm_ref, sinm_ref, coss_ref, sins_ref, prev_ref,
                       wb_ref, kdm_ref, vdm_ref, um_ref, yc_ref, q_ref, kst_ref, vst_ref, ks_ref, vs_ref, us_ref,
                       hn_scr, z_scr):
    j = pl.program_id(0)
    n_meta = xm_ref.shape[0]
    n_all = hn_scr.shape[0]

    @pl.when(j == 0)
    def _():
        hn_scr[:n_meta, :] = _rms(xm_ref[...], g1_ref[...]).astype(BF16)
        hn_scr[n_meta:, :] = _rms(xs_ref[...], g1_ref[...]).astype(BF16)

    wb = w_ref[...].astype(BF16)
    wb_ref[...] = wb
    z_scr[j] = jnp.dot(hn_scr[...], wb, preferred_element_type=F32)

    @pl.when(j == pl.num_programs(0) - 1)
    def _():
        def section(rows, lo, width):
            parts, col = [], lo
            while col < lo + width:
                blk, off = divmod(col, SMALL_COLS)
                take = min(SMALL_COLS - off, lo + width - col)
                parts.append(z_scr[blk, rows, off:off + take])
                col += take
            return parts[0] if len(parts) == 1 else jnp.concatenate(parts, axis=-1)

        cw = cw_ref[...]
        rows = slice(0, n_meta)
        u = section(rows, COL_CG, CONV_DIM) * section(rows, COL_XC, CONV_DIM)
        um_ref[...] = u[n_meta - (CONV_W - 1):, :]
        k = _rope(section(rows, COL_K, KV_DIM), cosm_ref[...], sinm_ref[...])
        v = section(rows, COL_V, KV_DIM)
        pad = jnp.zeros((kdm_ref.shape[0] - n_meta, kdm_ref.shape[1]), BF16)
        kdm_ref[...] = jnp.concatenate([pad, _dup_all(k)], axis=0)
        vdm_ref[...] = jnp.concatenate([pad, _dup_all(v)], axis=0)

        rows = slice(n_meta, n_all)
        u = section(rows, COL_CG, CONV_DIM) * section(rows, COL_XC, CONV_DIM)
        u2 = prev_ref[:, :CONV_DIM]
        u1 = prev_ref[:, CONV_DIM:]
        us_ref[...] = jnp.concatenate([u1, u], axis=-1)
        cy = cw[0:1, :] * u2 + cw[1:2, :] * u1 + cw[2:3, :] * u
        yc_ref[...] = _rms(section(rows, COL_BG, CONV_DIM) * cy, cg_ref[...]).astype(BF16)
        cos = coss_ref[...]
        sin = sins_ref[...]
        q = _rope(section(rows, COL_Q, ATTN_DIM), cos, sin) * Q_SCALE
        k = _rope(section(rows, COL_K, KV_DIM), cos, sin)
        v = section(rows, COL_V, KV_DIM)
        q_ref[...] = _group_minor(q).astype(BF16)
        ks_ref[...] = k
        vs_ref[...] = v
        kst_ref[...] = k.T
        vst_ref[...] = v.T


def _small_proj_call(x_meta, x_sample, g1, w_in, conv_w, conv_g, cos_m, sin_m, cos_s, sin_s, prev_s):
    n_meta = x_meta.shape[0]
    ns = x_sample.shape[0]
    n_blocks = IN_COLS // SMALL_COLS
    kd_cols = N_KV_HEADS * LANES
    const = lambda j: (0, 0)
    full = lambda shape: pl.BlockSpec(shape, const)
    w_spec = pl.BlockSpec((D_MODEL, SMALL_COLS), lambda j: (0, j))
    out_shapes = [
        ((D_MODEL, IN_COLS), BF16),
        ((WINDOW, kd_cols), BF16),
        ((WINDOW, kd_cols), BF16),
        ((CONV_W - 1, CONV_DIM), F32),
        ((ns, CONV_DIM), BF16),
        ((ns, ATTN_DIM), BF16),
        ((KV_DIM, ns), F32),
        ((KV_DIM, ns), F32),
        ((ns, KV_DIM), F32),
        ((ns, KV_DIM), F32),
        ((ns, (CONV_W - 1) * CONV_DIM), F32),
    ]
    return pl.pallas_call(
        _small_proj_kernel,
        grid=(n_blocks,),
        in_specs=[
            full((n_meta, D_MODEL)), full((ns, D_MODEL)), full((1, D_MODEL)), w_spec,
            full((CONV_W, CONV_DIM)), full((1, CONV_DIM)),
            full((n_meta, LANES)), full((n_meta, LANES)), full((ns, LANES)), full((ns, LANES)),
            full((ns, (CONV_W - 1) * CONV_DIM)),
        ],
        out_specs=[w_spec] + [full(shape) for shape, _ in out_shapes[1:]],
        out_shape=[jax.ShapeDtypeStruct(shape, dtype) for shape, dtype in out_shapes],
        scratch_shapes=[pltpu.VMEM((n_meta + ns, D_MODEL), BF16),
                        pltpu.VMEM((n_blocks, n_meta + ns, SMALL_COLS), F32)],
        compiler_params=pltpu.CompilerParams(
            dimension_semantics=("arbitrary",), vmem_limit_bytes=VMEM_LIMIT),
        name="proj_small",
    )(x_meta, x_sample, g1, w_in, conv_w, conv_g, cos_m, sin_m, cos_s, sin_s, prev_s)


def _attn_prompt_kernel(sinks_ref, q_ref, kc_ref, kp_ref, vc_ref, vp_ref, km_ref, vm_ref, ag_ref,
                        ya_ref, o_scr):
    first = pl.program_id(1) == 0
    blk = WINDOW
    nk = 2 * blk
    n_sub = q_ref.shape[1] // blk

    qi = lax.broadcasted_iota(jnp.int32, (blk, nk), 0)
    kj = lax.broadcasted_iota(jnp.int32, (blk, nk), 1)
    band = (kj >= qi) & (kj <= qi + WINDOW)
    band_first = band & (kj >= jnp.where(first, blk - N_META, 0))

    lane = lax.broadcasted_iota(jnp.int32, (blk, LANES), 1)
    ones = jnp.ones((nk, LANES), BF16)
    for t in range(n_sub):
        rows = slice(t * blk, (t + 1) * blk)
        valid = band_first if t == 0 else band
        for g in range(N_KV_HEADS):
            cols = slice(g * LANES, (g + 1) * LANES)
            if t == 0:
                k_prev = jnp.where(first, km_ref[:, cols], kp_ref[0, :, cols])
                v_prev = jnp.where(first, vm_ref[:, cols], vp_ref[0, :, cols])
            else:
                k_prev = kc_ref[0, (t - 1) * blk:t * blk, cols]
                v_prev = vc_ref[0, (t - 1) * blk:t * blk, cols]
            kd = jnp.concatenate([k_prev, kc_ref[0, rows, cols]], axis=0)
            vd = jnp.concatenate([v_prev, vc_ref[0, rows, cols]], axis=0)
            v_rhs = jnp.concatenate([vd, ones], axis=1)
            for tile in range(g * GROUP // HEADS_PER_TILE, (g + 1) * GROUP // HEADS_PER_TILE):
                ov, m = [], []
                for h in (HEADS_PER_TILE * tile, HEADS_PER_TILE * tile + 1):
                    qm = q_ref[0, rows, h * LANES:(h + 1) * LANES]
                    s = lax.dot_general(qm, kd, (((1,), (1,)), ((), ())), preferred_element_type=F32)
                    s = jnp.where(valid, s, NEG_INF)
                    m.append(jnp.max(s, axis=-1, keepdims=True))
                    p = jnp.exp2(s - m[-1]).astype(BF16)
                    ov.append(jnp.dot(p, v_rhs, preferred_element_type=F32))
                low = lane < HEAD_DIM
                sink = jnp.where(low[:1], sinks_ref[HEADS_PER_TILE * tile], sinks_ref[HEADS_PER_TILE * tile + 1])
                denom = (jnp.where(low, ov[0][:, LANES:], ov[1][:, LANES:])
                         + jnp.exp2(sink * LOG2_E - jnp.where(low, m[0], m[1])))
                o_scr[rows, tile * LANES:(tile + 1) * LANES] = (
                    jnp.where(low, ov[0][:, :LANES], ov[1][:, :LANES]) / denom)
        ya_ref[0, rows, :] = _rms(o_scr[rows, :], ag_ref[...]).astype(BF16)


def _attn_prompt(q, kd, vd, kd_meta, vd_meta, sinks, attn_g, *, n_sub):
    nb, s, kd_cols = kd.shape
    blk = WINDOW
    tq = n_sub * blk
    cur = lambda b, j: (b, j, 0)
    prv = lambda b, j: (b, jnp.maximum(j * n_sub - 1, 0), 0)
    const = lambda b, j: (0, 0)
    return pl.pallas_call(
        _attn_prompt_kernel,
        grid=(nb, s // tq),
        in_specs=[
            pl.BlockSpec(memory_space=pltpu.SMEM),
            pl.BlockSpec((1, tq, N_HEADS * LANES), cur),
            pl.BlockSpec((1, tq, kd_cols), cur),
            pl.BlockSpec((1, blk, kd_cols), prv),
            pl.BlockSpec((1, tq, kd_cols), cur),
            pl.BlockSpec((1, blk, kd_cols), prv),
            pl.BlockSpec((blk, kd_cols), const),
            pl.BlockSpec((blk, kd_cols), const),
            pl.BlockSpec((1, ATTN_DIM), const),
        ],
        out_specs=pl.BlockSpec((1, tq, ATTN_DIM), cur),
        out_shape=jax.ShapeDtypeStruct((nb, s, ATTN_DIM), BF16),
        scratch_shapes=[pltpu.VMEM((tq, ATTN_DIM), F32)],
        compiler_params=pltpu.CompilerParams(
            dimension_semantics=("arbitrary", "arbitrary"), vmem_limit_bytes=VMEM_LIMIT),
        name="attn_prompt",
    )(sinks, q, kd, kd, vd, vd, kd_meta, vd_meta, attn_g)


def _attn_sample_kernel(sinks_ref, q_ref, kn_ref, vn_ref, knt_ref, vnt_ref, ck_ref, cv_ref, ag_ref,
                        ya_ref, nk_ref, nv_ref):
    nseq = q_ref.shape[0]
    base = pl.program_id(0) * nseq
    n_rows = N_HEADS * nseq
    qf = q_ref[...].astype(F32)
    kv_of_lane = lax.broadcasted_iota(jnp.int32, (nseq, KV_DIM), 1) // HEAD_DIM
    q_rows, sink_rows = [], []
    for r in range(N_HEADS):
        i, g = divmod(r, N_KV_HEADS)
        q_rows.append(jnp.where(kv_of_lane == g, qf[:, i * KV_DIM:(i + 1) * KV_DIM], 0.0))
        sink_rows.append(jnp.full((nseq, 1), sinks_ref[g * GROUP + i] * LOG2_E, F32))
    q_all = jnp.concatenate(q_rows, axis=0)
    sink = jnp.concatenate(sink_rows, axis=0)
    q_bf = q_all.astype(BF16)

    key_lane = lax.broadcasted_iota(jnp.int32, (KV_DIM, WINDOW), 1)

    def shifted(cache, new_t, n):
        col = jnp.sum(jnp.where(key_lane == base + n, new_t, 0.0), axis=1, keepdims=True)
        return jnp.where(key_lane == WINDOW - 1, col, pltpu.roll(cache, WINDOW - 1, axis=1))

    seq_of_row = lax.broadcasted_iota(jnp.int32, (n_rows, WINDOW), 0) % nseq
    s_c = jnp.zeros((n_rows, WINDOW), F32)
    for n in range(nseq):
        kt = ck_ref[n]
        s_n = jnp.dot(q_bf, kt.astype(BF16), preferred_element_type=F32)
        s_c = jnp.where(seq_of_row == n, s_n, s_c)
        nk_ref[n] = shifted(kt, knt_ref[...], n)
    kn = jnp.concatenate([kn_ref[...]] * N_HEADS, axis=0)
    vn = jnp.concatenate([vn_ref[...]] * N_HEADS, axis=0)
    s_new = jnp.sum(q_all * kn, axis=-1, keepdims=True)
    m = jnp.maximum(jnp.maximum(jnp.max(s_c, axis=-1, keepdims=True), s_new), sink)
    p_c = jnp.exp2(s_c - m)
    p_new = jnp.exp2(s_new - m)
    denom = jnp.sum(p_c, axis=-1, keepdims=True) + p_new + jnp.exp2(sink - m)
    p_bf = p_c.astype(BF16)

    seq_of_row = lax.broadcasted_iota(jnp.int32, (n_rows, KV_DIM), 0) % nseq
    o = jnp.zeros((n_rows, KV_DIM), F32)
    for n in range(nseq):
        vt = cv_ref[n]
        o_n = lax.dot_general(p_bf, vt.astype(BF16), (((1,), (1,)), ((), ())), preferred_element_type=F32)
        o = jnp.where(seq_of_row == n, o_n, o)
        nv_ref[n] = shifted(vt, vnt_ref[...], n)
    kv_of_row = (lax.broadcasted_iota(jnp.int32, (n_rows, KV_DIM), 0) // nseq) % N_KV_HEADS
    own = (lax.broadcasted_iota(jnp.int32, (n_rows, KV_DIM), 1) // HEAD_DIM) == kv_of_row
    o = jnp.where(own, (o + p_new * vn) / denom, 0.0)
    slabs = []
    for i in range(GROUP):
        lo = i * N_KV_HEADS * nseq
        slabs.append(sum(o[lo + g * nseq:lo + (g + 1) * nseq] for g in range(N_KV_HEADS)))
    ya_ref[...] = _rms(_group_major(jnp.concatenate(slabs, axis=-1)), ag_ref[...]).astype(BF16)


def _attn_sample(q, k_new, v_new, k_new_t, v_new_t, cache_kt, cache_vt, sinks, attn_g, *, nseq):
    n = q.shape[0]
    rows = lambda i: (i, 0)
    seqs = lambda i: (i, 0, 0)
    const = lambda i: (0, 0)
    return pl.pallas_call(
        _attn_sample_kernel,
        grid=(n // nseq,),
        in_specs=[
            pl.BlockSpec(memory_space=pltpu.SMEM),
            pl.BlockSpec((nseq, ATTN_DIM), rows),
            pl.BlockSpec((nseq, KV_DIM), rows),
            pl.BlockSpec((nseq, KV_DIM), rows),
            pl.BlockSpec((KV_DIM, n), const),
            pl.BlockSpec((KV_DIM, n), const),
            pl.BlockSpec((nseq, KV_DIM, WINDOW), seqs),
            pl.BlockSpec((nseq, KV_DIM, WINDOW), seqs),
            pl.BlockSpec((1, ATTN_DIM), const),
        ],
        out_specs=[
            pl.BlockSpec((nseq, ATTN_DIM), rows),
            pl.BlockSpec((nseq, KV_DIM, WINDOW), seqs),
            pl.BlockSpec((nseq, KV_DIM, WINDOW), seqs),
        ],
        out_shape=[
            jax.ShapeDtypeStruct((n, ATTN_DIM), BF16),
            jax.ShapeDtypeStruct((n, KV_DIM, WINDOW), F32),
            jax.ShapeDtypeStruct((n, KV_DIM, WINDOW), F32),
        ],
        compiler_params=pltpu.CompilerParams(
            dimension_semantics=("arbitrary",), vmem_limit_bytes=VMEM_LIMIT),
        name="attn_sample",
    )(sinks, q, k_new, v_new, k_new_t, v_new_t, cache_kt, cache_vt, attn_g)


N_MLP_IN = 9


def _mlp_kernel(chunks, has_rider, *refs):
    x_ref, yc_ref, ya_ref, wo_ref, g2_ref, wg_hbm, wu_hbm, wd_hbm, gf_ref = refs[:N_MLP_IN]
    if has_rider:
        xr_ref, ycr_ref, yar_ref, y_ref, yr_ref, hn_scr, hnr_scr, wg_buf, wu_buf, wd_buf, sem = refs[N_MLP_IN:]
    else:
        y_ref, hn_scr, wg_buf, wu_buf, wd_buf, sem = refs[N_MLP_IN:]
    i = pl.program_id(0)
    n_tiles = pl.num_programs(0)
    n_slots = wg_buf.shape[0]
    n_chunks = len(chunks)
    ahead = n_slots - 1

    def chunk_copies(c):
        slot = c % n_slots
        off, width = chunks[c]
        cols = pl.ds(off, width)
        return (pltpu.make_async_copy(wg_hbm.at[:, cols], wg_buf.at[slot, :, :width], sem.at[0, slot]),
                pltpu.make_async_copy(wu_hbm.at[:, cols], wu_buf.at[slot, :, :width], sem.at[1, slot]),
                pltpu.make_async_copy(wd_hbm.at[cols, :], wd_buf.at[slot, :width, :], sem.at[2, slot]))

    def start(c):
        for cp in chunk_copies(c):
            cp.start()

    def wait(c):
        for cp in chunk_copies(c):
            cp.wait()

    @pl.when(i == 0)
    def _():
        for c in range(min(ahead, n_chunks)):
            start(c)

    def tile(row_sets, prefetch_next_tile):
        for x, yc, ya, y, hn in row_sets:
            mix = (jnp.dot(yc[...], wo_ref[:CONV_DIM, :], preferred_element_type=F32)
                   + jnp.dot(ya[...], wo_ref[CONV_DIM:, :], preferred_element_type=F32))
            h = x[...] + mix
            y[...] = h
            hn[...] = _rms(h, g2_ref[...]).astype(BF16)

        for c in range(n_chunks):
            nxt = c + ahead
            if nxt < n_chunks:
                start(nxt)
            elif prefetch_next_tile:
                @pl.when(i + 1 < n_tiles)
                def _():
                    start(nxt - n_chunks)
            wait(c)
            slot = c % n_slots
            width = chunks[c][1]
            for x, yc, ya, y, hn in row_sets:
                hnv = hn[...]
                gate = jnp.dot(hnv, wg_buf[slot, :, :width], preferred_element_type=F32)
                up = jnp.dot(hnv, wu_buf[slot, :, :width], preferred_element_type=F32)
                act = (gate * jax.nn.sigmoid(gate) * up).astype(BF16)
                y[...] += jnp.dot(act, wd_buf[slot, :width, :], preferred_element_type=F32)

        for x, yc, ya, y, hn in row_sets:
            y[...] = _rms(y[...], gf_ref[...])

    main = (x_ref, yc_ref, ya_ref, y_ref, hn_scr)
    if has_rider:
        rider = (xr_ref, ycr_ref, yar_ref, yr_ref, hnr_scr)

        @pl.when(i + 1 < n_tiles)
        def _():
            tile([main], True)

        @pl.when(i + 1 == n_tiles)
        def _():
            tile([main, rider], False)
    else:
        tile([main], True)


def _mlp_chunks(d_ff, tf):
    bounds = list(range(0, d_ff, tf)) + [d_ff]
    return tuple((lo, hi - lo) for lo, hi in zip(bounds[:-1], bounds[1:]))


def _mlp_call(x, yc, ya, w_out, g2, w_gate, w_up, w_down, gf, *, tm, tf, n_slots, rider=None):
    r = x.shape[0]
    chunks = _mlp_chunks(w_gate.shape[1], tf)
    assert r == tm or len(chunks) % n_slots == 0
    assert len(chunks) >= n_slots and all(w % LANES == 0 for _, w in chunks)
    rows = lambda i: (i, 0)
    const = lambda i: (0, 0)
    once = lambda shape: pl.BlockSpec(shape, const, pipeline_mode=pl.Buffered(1))
    rider = tuple(rider) if rider is not None else ()
    nr = rider[0].shape[0] if rider else 0
    out_specs = [pl.BlockSpec((tm, D_MODEL), rows)]
    out_shape = [jax.ShapeDtypeStruct((r, D_MODEL), F32)]
    scratch = [pltpu.VMEM((tm, D_MODEL), BF16)]
    if rider:
        out_specs.append(pl.BlockSpec((nr, D_MODEL), const))
        out_shape.append(jax.ShapeDtypeStruct((nr, D_MODEL), F32))
        scratch.append(pltpu.VMEM((nr, D_MODEL), BF16))
    return pl.pallas_call(
        functools.partial(_mlp_kernel, chunks, bool(rider)),
        grid=(r // tm,),
        in_specs=[
            pl.BlockSpec((tm, D_MODEL), rows),
            pl.BlockSpec((tm, CONV_DIM), rows),
            pl.BlockSpec((tm, ATTN_DIM), rows),
            once((D_MODEL, D_MODEL)),
            pl.BlockSpec((1, D_MODEL), const),
            pl.BlockSpec(memory_space=pl.ANY),
            pl.BlockSpec(memory_space=pl.ANY),
            pl.BlockSpec(memory_space=pl.ANY),
            pl.BlockSpec((1, D_MODEL), const),
            *[once(a.shape) for a in rider],
        ],
        out_specs=out_specs,
        out_shape=out_shape,
        scratch_shapes=scratch + [
            pltpu.VMEM((n_slots, D_MODEL, tf), BF16),
            pltpu.VMEM((n_slots, D_MODEL, tf), BF16),
            pltpu.VMEM((n_slots, tf, D_MODEL), BF16),
            pltpu.SemaphoreType.DMA((3, n_slots)),
        ],
        compiler_params=pltpu.CompilerParams(
            dimension_semantics=("arbitrary",), vmem_limit_bytes=VMEM_LIMIT),
        name="mlp",
    )(x, yc, ya, w_out, g2, w_gate, w_up, w_down, gf, *rider)


def _rope_tables(pos):
    lane = jnp.arange(LANES)
    inv = ROPE_THETA ** (-(lane % HALF_HEAD).astype(F32) / HALF_HEAD)
    sign = jnp.where(lane % HEAD_DIM < HALF_HEAD, -1.0, 1.0).astype(F32)
    ang = pos.astype(F32)[:, None] * inv[None, :]
    return jnp.cos(ang), jnp.sin(ang) * sign[None, :]


def kernel(x_prompt, x_sample, cache_k, cache_v, state_conv, meta_tokens, norm1_g, w_in, conv_w, conv_norm_g,
           attn_norm_g, attn_sinks, w_out, norm2_g, w_gate, w_up, w_down, final_norm_g):
    depth = w_in.shape[0]
    assert depth == 1, "single-layer step only"
    nb, seq, _ = x_prompt.shape
    ns, dec_seq, _ = x_sample.shape
    assert dec_seq == 1

    g1 = norm1_g[0][None]
    g2 = norm2_g[0][None]
    gf = final_norm_g[None]
    cg = conv_norm_g[0][None]
    ag = attn_norm_g[0][None]
    cw = conv_w[0]
    sinks = attn_sinks[0]
    xs = x_sample.reshape(ns, D_MODEL)
    cos_m, sin_m = _rope_tables(jnp.arange(N_META))
    cos_s, sin_s = _rope_tables(jnp.full((ns,), PAST_LEN))
    prev_s = state_conv[0].reshape(ns, (CONV_W - 1) * CONV_DIM)
    w_in_b, kd_meta, vd_meta, u_m, yc_s, q_s, k_st, v_st, k_s, v_s, u_s = _small_proj_call(
        meta_tokens, xs, g1, w_in[0], cw, cg, cos_m, sin_m, cos_s, sin_s, prev_s)

    cos_p, sin_p = _rope_tables(N_META + jnp.arange(seq))
    yc, q, kd, vd, k_tail, v_tail, u_tail, w_out_b, w_gate_b, w_up_b, w_down_b = _proj_call(
        x_prompt, g1, w_in_b, cw, cg, cos_p, sin_p, u_m, tm=512, n_sub=2,
        cast=(w_out[0], w_gate[0], w_up[0], w_down[0]))
    ya = _attn_prompt(q, kd, vd, kd_meta, vd_meta, sinks, ag, n_sub=8)
    new_k_prompt = k_tail.reshape(1, nb, WINDOW, N_KV_HEADS, HEAD_DIM)
    new_v_prompt = v_tail.reshape(1, nb, WINDOW, N_KV_HEADS, HEAD_DIM)
    new_conv_prompt = u_tail[None]

    cache_kt = jnp.transpose(cache_k[0], (0, 2, 3, 1)).reshape(ns, KV_DIM, WINDOW)
    cache_vt = jnp.transpose(cache_v[0], (0, 2, 3, 1)).reshape(ns, KV_DIM, WINDOW)
    ya_s, nk_s, nv_s = _attn_sample(q_s, k_s, v_s, k_st, v_st, cache_kt, cache_vt, sinks, ag, nseq=16)
    new_k_sample = jnp.transpose(nk_s.reshape(ns, N_KV_HEADS, HEAD_DIM, WINDOW), (0, 3, 1, 2))[None]
    new_v_sample = jnp.transpose(nv_s.reshape(ns, N_KV_HEADS, HEAD_DIM, WINDOW), (0, 3, 1, 2))[None]
    new_conv_sample = u_s.reshape(1, ns, CONV_W - 1, CONV_DIM)

    rp = nb * seq
    y_prompt, y_sample = _mlp_call(
        x_prompt.reshape(rp, D_MODEL), yc.reshape(rp, CONV_DIM), ya.reshape(rp, ATTN_DIM),
        w_out_b, g2, w_gate_b, w_up_b, w_down_b, gf, tm=512, tf=1024, n_slots=2, rider=(xs, yc_s, ya_s))
    y_prompt = y_prompt.reshape(nb, seq, D_MODEL)
    y_sample = y_sample.reshape(ns, 1, D_MODEL)

    return (y_prompt, y_sample, new_k_prompt, new_v_prompt, new_conv_prompt,
            new_k_sample, new_v_sample, new_conv_sample)
```

```python
import functools

import jax
import jax.numpy as jnp
from jax import lax
from jax.experimental import pallas as pl
from jax.experimental.pallas import tpu as pltpu

D_MODEL = 2048
N_META = 16
CONV_DIM = 1024
CONV_W = 3
HEAD_DIM = 64
HALF_HEAD = HEAD_DIM // 2
ATTN_DIM = 1024
N_HEADS = 16
N_KV_HEADS = 4
GROUP = N_HEADS // N_KV_HEADS
KV_DIM = N_KV_HEADS * HEAD_DIM
WINDOW = 128
ROPE_THETA = 10000.0
RMS_EPS = 1e-6
NEG_INF = -1e30
PAST_LEN = 16384
LOG2_E = 1.4426950408889634

COL_BG = 0
COL_CG = CONV_DIM
COL_XC = 2 * CONV_DIM
COL_Q = 3 * CONV_DIM
COL_K = COL_Q + ATTN_DIM
COL_V = COL_K + KV_DIM
IN_COLS = COL_V + KV_DIM

LANES = 128
HEADS_PER_TILE = LANES // HEAD_DIM
VMEM_LIMIT = 60 * 1024 * 1024

F32 = jnp.float32
BF16 = jnp.bfloat16


def _rms(x, g):
    return x * lax.rsqrt(jnp.mean(x * x, axis=-1, keepdims=True) + RMS_EPS) * g


def _lane_tile(t, width):
    return jnp.concatenate([t] * (width // t.shape[-1]), axis=-1)


def _rope(t, cos, sin_signed):
    w = t.shape[-1]
    lane = lax.broadcasted_iota(jnp.int32, t.shape, 1)
    first_half = (lane & (HEAD_DIM - 1)) < HALF_HEAD
    swapped = jnp.where(first_half, pltpu.roll(t, w - HALF_HEAD, axis=1), pltpu.roll(t, HALF_HEAD, axis=1))
    return t * _lane_tile(cos, w) + swapped * _lane_tile(sin_signed, w)


def _dup_heads(pair_tile, half):
    lane = lax.broadcasted_iota(jnp.int32, pair_tile.shape, 1)
    swapped = pltpu.roll(pair_tile, HEAD_DIM, axis=1)
    if half == 0:
        return jnp.where(lane < HEAD_DIM, pair_tile, swapped)
    return jnp.where(lane < HEAD_DIM, swapped, pair_tile)


def _dup_all(t):
    tiles = []
    for g in range(N_KV_HEADS):
        pair, half = divmod(g, HEADS_PER_TILE)
        tiles.append(_dup_heads(t[:, pair * LANES:(pair + 1) * LANES], half))
    return jnp.concatenate(tiles, axis=-1).astype(BF16)


def _spread_heads(q):
    lane = lax.broadcasted_iota(jnp.int32, (q.shape[0], LANES), 1)
    tiles = []
    for h in range(N_HEADS):
        tile, half = divmod(h, HEADS_PER_TILE)
        qt = q[:, tile * LANES:(tile + 1) * LANES]
        keep = (lane < HEAD_DIM) if half == 0 else (lane >= HEAD_DIM)
        tiles.append(jnp.where(keep, qt, jnp.zeros_like(qt)))
    return jnp.concatenate(tiles, axis=-1)


def _head_block(t, h):
    return t[:, h * HEAD_DIM:(h + 1) * HEAD_DIM]


def _group_minor(t):
    return jnp.concatenate([_head_block(t, g * GROUP + i) for i in range(GROUP) for g in range(N_KV_HEADS)], axis=-1)


def _group_major(t):
    return jnp.concatenate([_head_block(t, i * N_KV_HEADS + g) for g in range(N_KV_HEADS) for i in range(GROUP)],
                           axis=-1)


def _conv_mix(u, p2, p1, cw):
    row = lax.broadcasted_iota(jnp.int32, u.shape, 0)
    u1 = jnp.where(row == 0, p1, pltpu.roll(u, 1, axis=0))
    u2 = jnp.where(row == 0, p2, jnp.where(row == 1, p1, pltpu.roll(u, 2, axis=0)))
    return cw[0:1, :] * u2 + cw[1:2, :] * u1 + cw[2:3, :] * u


Q_SCALE = HEAD_DIM ** -0.5 * LOG2_E

N_PROJ_IN = 8
N_PROJ_OUT = 7


def _proj_kernel(n_sub, *refs):
    x_ref, g1_ref, w_ref, cw_ref, cg_ref, cos_ref, sin_ref, prev_ref = refs[:N_PROJ_IN]
    n_cast = (len(refs) - 1 - N_PROJ_IN - N_PROJ_OUT) // 2
    cast_in = refs[N_PROJ_IN:N_PROJ_IN + n_cast]
    yc_ref, q_ref, kd_ref, vd_ref, kt_ref, vt_ref, u_ref = refs[N_PROJ_IN + n_cast:N_PROJ_IN + n_cast + N_PROJ_OUT]
    cast_out = refs[N_PROJ_IN + n_cast + N_PROJ_OUT:-1]
    carry_ref = refs[-1]

    ts = x_ref.shape[1] // n_sub
    cw = cw_ref[...]

    @pl.when(pl.program_id(1) == 0)
    def _():
        carry_ref[...] = prev_ref[...]

    p2 = carry_ref[0:1, :]
    p1 = carry_ref[1:2, :]

    for t in range(n_sub):
        rows = slice(t * ts, (t + 1) * ts)
        hn = _rms(x_ref[0, rows, :], g1_ref[...]).astype(BF16)

        def section(lo, width):
            return jnp.dot(hn, w_ref[:, lo:lo + width], preferred_element_type=F32)

        u = section(COL_CG, CONV_DIM) * section(COL_XC, CONV_DIM)
        cy = _conv_mix(u, p2, p1, cw)
        p2 = u[ts - 2:ts - 1, :]
        p1 = u[ts - 1:, :]
        yc_ref[0, rows, :] = _rms(section(COL_BG, CONV_DIM) * cy, cg_ref[...]).astype(BF16)

        cos = cos_ref[rows, :]
        sin = sin_ref[rows, :]
        q = _rope(section(COL_Q, ATTN_DIM), cos, sin) * Q_SCALE
        k = _rope(section(COL_K, KV_DIM), cos, sin)
        v = section(COL_V, KV_DIM)
        q_ref[0, rows, :] = _spread_heads(q.astype(BF16))
        kd_ref[0, rows, :] = _dup_all(k)
        vd_ref[0, rows, :] = _dup_all(v)

        if t == 0:
            for src, dst in zip(cast_in, cast_out):
                dst[...] = src[...].astype(BF16)

    tail = jnp.concatenate([p2, p1], axis=0)
    carry_ref[...] = tail
    u_ref[0] = tail
    tail_rows = kt_ref.shape[1]
    kt_ref[0] = k[ts - tail_rows:, :]
    vt_ref[0] = v[ts - tail_rows:, :]


def _proj_call(x, g1, w_in, conv_w, conv_g, cos, sin, prev, *, tm, n_sub, cast=()):
    nb, s, _ = x.shape
    n_steps = nb * (s // tm)
    slab = lambda b, i: (b * (s // tm) + i, 0)
    cast_specs = [pl.BlockSpec((w.shape[0] // n_steps, w.shape[1]), slab) for w in cast]
    assert all(w.shape[0] % (n_steps * 16) == 0 for w in cast)
    assert tm // n_sub >= WINDOW
    kd_cols = N_KV_HEADS * LANES
    q_cols = N_HEADS * LANES
    const = lambda b, i: (0, 0)
    rows = lambda b, i: (b, i, 0)
    per_seq = lambda b, i: (b, 0, 0)
    return pl.pallas_call(
        functools.partial(_proj_kernel, n_sub),
        grid=(nb, s // tm),
        in_specs=[
            pl.BlockSpec((1, tm, D_MODEL), rows),
            pl.BlockSpec((1, D_MODEL), const),
            pl.BlockSpec((D_MODEL, IN_COLS), const, pipeline_mode=pl.Buffered(1)),
            pl.BlockSpec((CONV_W, CONV_DIM), const),
            pl.BlockSpec((1, CONV_DIM), const),
            pl.BlockSpec((tm, LANES), lambda b, i: (i, 0)),
            pl.BlockSpec((tm, LANES), lambda b, i: (i, 0)),
            pl.BlockSpec((CONV_W - 1, CONV_DIM), const),
            *cast_specs,
        ],
        out_specs=[
            pl.BlockSpec((1, tm, CONV_DIM), rows),
            pl.BlockSpec((1, tm, q_cols), rows),
            pl.BlockSpec((1, tm, kd_cols), rows),
            pl.BlockSpec((1, tm, kd_cols), rows),
            pl.BlockSpec((1, WINDOW, KV_DIM), per_seq),
            pl.BlockSpec((1, WINDOW, KV_DIM), per_seq),
            pl.BlockSpec((1, CONV_W - 1, CONV_DIM), per_seq),
            *cast_specs,
        ],
        out_shape=[
            jax.ShapeDtypeStruct((nb, s, CONV_DIM), BF16),
            jax.ShapeDtypeStruct((nb, s, q_cols), BF16),
            jax.ShapeDtypeStruct((nb, s, kd_cols), BF16),
            jax.ShapeDtypeStruct((nb, s, kd_cols), BF16),
            jax.ShapeDtypeStruct((nb, WINDOW, KV_DIM), F32),
            jax.ShapeDtypeStruct((nb, WINDOW, KV_DIM), F32),
            jax.ShapeDtypeStruct((nb, CONV_W - 1, CONV_DIM), F32),
            *[jax.ShapeDtypeStruct(w.shape, BF16) for w in cast],
        ],
        scratch_shapes=[pltpu.VMEM((CONV_W - 1, CONV_DIM), F32)],
        compiler_params=pltpu.CompilerParams(
            dimension_semantics=("arbitrary", "arbitrary"), vmem_limit_bytes=VMEM_LIMIT),
        name="proj_prompt",
    )(x, g1, w_in, conv_w, conv_g, cos, sin, prev, *cast)


SMALL_ROWS = 256


def _small_proj_kernel(xm_ref, xs_ref, g1_ref, w_ref, cw_ref, cg_ref, cosm_ref, sinm_ref, coss_ref, sins_ref, prev_ref,
                       wb_ref, kdm_ref, vdm_ref, um_ref, yc_ref, q_ref, kst_ref, vst_ref, ks_ref, vs_ref, us_ref,
                       hn_scr, z_scr):
    j = pl.program_id(0)
    n_meta = xm_ref.shape[0]
    n_all = z_scr.shape[0]

    @pl.when(j == 0)
    def _():
        hn = jnp.concatenate([_rms(xm_ref[...], g1_ref[...]), _rms(xs_ref[...], g1_ref[...])], axis=0).astype(BF16)
        for b in range(hn_scr.shape[0]):
            hn_scr[b] = hn[:, b * SMALL_ROWS:(b + 1) * SMALL_ROWS]
        z_scr[...] = jnp.zeros_like(z_scr)

    wb = w_ref[...].astype(BF16)
    wb_ref[...] = wb
    z_scr[...] += jnp.dot(hn_scr[j], wb, preferred_element_type=F32)

    @pl.when(j == pl.num_programs(0) - 1)
    def _():
        def section(rows, lo, width):
            return z_scr[rows, lo:lo + width]

        cw = cw_ref[...]
        rows = slice(0, n_meta)
        u = section(rows, COL_CG, CONV_DIM) * section(rows, COL_XC, CONV_DIM)
        um_ref[...] = u[n_meta - (CONV_W - 1):, :]
        k = _rope(section(rows, COL_K, KV_DIM), cosm_ref[...], sinm_ref[...])
        v = section(rows, COL_V, KV_DIM)
        pad = jnp.zeros((kdm_ref.shape[0] - n_meta, kdm_ref.shape[1]), BF16)
        kdm_ref[...] = jnp.concatenate([pad, _dup_all(k)], axis=0)
        vdm_ref[...] = jnp.concatenate([pad, _dup_all(v)], axis=0)

        rows = slice(n_meta, n_all)
        u = section(rows, COL_CG, CONV_DIM) * section(rows, COL_XC, CONV_DIM)
        u2 = prev_ref[:, :CONV_DIM]
        u1 = prev_ref[:, CONV_DIM:]
        us_ref[...] = jnp.concatenate([u1, u], axis=-1)
        cy = cw[0:1, :] * u2 + cw[1:2, :] * u1 + cw[2:3, :] * u
        yc_ref[...] = _rms(section(rows, COL_BG, CONV_DIM) * cy, cg_ref[...]).astype(BF16)
        cos = coss_ref[...]
        sin = sins_ref[...]
        q = _rope(section(rows, COL_Q, ATTN_DIM), cos, sin) * Q_SCALE
        k = _rope(section(rows, COL_K, KV_DIM), cos, sin)
        v = section(rows, COL_V, KV_DIM)
        q_ref[...] = _group_minor(q).astype(BF16)
        ks_ref[...] = k
        vs_ref[...] = v
        kst_ref[...] = k.T
        vst_ref[...] = v.T


def _small_proj_call(x_meta, x_sample, g1, w_in, conv_w, conv_g, cos_m, sin_m, cos_s, sin_s, prev_s):
    n_meta = x_meta.shape[0]
    ns = x_sample.shape[0]
    n_blocks = D_MODEL // SMALL_ROWS
    kd_cols = N_KV_HEADS * LANES
    const = lambda j: (0, 0)
    full = lambda shape: pl.BlockSpec(shape, const)
    w_spec = pl.BlockSpec((SMALL_ROWS, IN_COLS), lambda j: (j, 0))
    out_shapes = [
        ((D_MODEL, IN_COLS), BF16),
        ((WINDOW, kd_cols), BF16),
        ((WINDOW, kd_cols), BF16),
        ((CONV_W - 1, CONV_DIM), F32),
        ((ns, CONV_DIM), BF16),
        ((ns, ATTN_DIM), BF16),
        ((KV_DIM, ns), F32),
        ((KV_DIM, ns), F32),
        ((ns, KV_DIM), F32),
        ((ns, KV_DIM), F32),
        ((ns, (CONV_W - 1) * CONV_DIM), F32),
    ]
    return pl.pallas_call(
        _small_proj_kernel,
        grid=(n_blocks,),
        in_specs=[
            full((n_meta, D_MODEL)), full((ns, D_MODEL)), full((1, D_MODEL)), w_spec,
            full((CONV_W, CONV_DIM)), full((1, CONV_DIM)),
            full((n_meta, LANES)), full((n_meta, LANES)), full((ns, LANES)), full((ns, LANES)),
            full((ns, (CONV_W - 1) * CONV_DIM)),
        ],
        out_specs=[w_spec] + [full(shape) for shape, _ in out_shapes[1:]],
        out_shape=[jax.ShapeDtypeStruct(shape, dtype) for shape, dtype in out_shapes],
        scratch_shapes=[pltpu.VMEM((n_blocks, n_meta + ns, SMALL_ROWS), BF16),
                        pltpu.VMEM((n_meta + ns, IN_COLS), F32)],
        compiler_params=pltpu.CompilerParams(
            dimension_semantics=("arbitrary",), vmem_limit_bytes=VMEM_LIMIT),
        name="proj_small",
    )(x_meta, x_sample, g1, w_in, conv_w, conv_g, cos_m, sin_m, cos_s, sin_s, prev_s)


def _attn_prompt_kernel(sinks_ref, q_ref, kc_ref, kp_ref, vc_ref, vp_ref, km_ref, vm_ref, ag_ref,
                        ya_ref, o_scr):
    first = pl.program_id(1) == 0
    blk = WINDOW
    nk = 2 * blk
    n_sub = q_ref.shape[1] // blk

    qi = lax.broadcasted_iota(jnp.int32, (blk, nk), 0)
    kj = lax.broadcasted_iota(jnp.int32, (blk, nk), 1)
    band = (kj >= qi) & (kj <= qi + WINDOW)
    band_first = band & (kj >= jnp.where(first, blk - N_META, 0))

    lane = lax.broadcasted_iota(jnp.int32, (blk, LANES), 1)
    ones = jnp.ones((nk, LANES), BF16)
    for t in range(n_sub):
        rows = slice(t * blk, (t + 1) * blk)
        valid = band_first if t == 0 else band
        for g in range(N_KV_HEADS):
            cols = slice(g * LANES, (g + 1) * LANES)
            if t == 0:
                k_prev = jnp.where(first, km_ref[:, cols], kp_ref[0, :, cols])
                v_prev = jnp.where(first, vm_ref[:, cols], vp_ref[0, :, cols])
            else:
                k_prev = kc_ref[0, (t - 1) * blk:t * blk, cols]
                v_prev = vc_ref[0, (t - 1) * blk:t * blk, cols]
            kd = jnp.concatenate([k_prev, kc_ref[0, rows, cols]], axis=0)
            vd = jnp.concatenate([v_prev, vc_ref[0, rows, cols]], axis=0)
            v_rhs = jnp.concatenate([vd, ones], axis=1)
            for tile in range(g * GROUP // HEADS_PER_TILE, (g + 1) * GROUP // HEADS_PER_TILE):
                ov, m = [], []
                for h in (HEADS_PER_TILE * tile, HEADS_PER_TILE * tile + 1):
                    qm = q_ref[0, rows, h * LANES:(h + 1) * LANES]
                    s = lax.dot_general(qm, kd, (((1,), (1,)), ((), ())), preferred_element_type=F32)
                    s = jnp.where(valid, s, NEG_INF)
                    m.append(jnp.max(s, axis=-1, keepdims=True))
                    p = jnp.exp2(s - m[-1]).astype(BF16)
                    ov.append(jnp.dot(p, v_rhs, preferred_element_type=F32))
                low = lane < HEAD_DIM
                sink = jnp.where(low[:1], sinks_ref[HEADS_PER_TILE * tile], sinks_ref[HEADS_PER_TILE * tile + 1])
                denom = (jnp.where(low, ov[0][:, LANES:], ov[1][:, LANES:])
                         + jnp.exp2(sink * LOG2_E - jnp.where(low, m[0], m[1])))
                o_scr[rows, tile * LANES:(tile + 1) * LANES] = (
                    jnp.where(low, ov[0][:, :LANES], ov[1][:, :LANES]) / denom)
        ya_ref[0, rows, :] = _rms(o_scr[rows, :], ag_ref[...]).astype(BF16)


def _attn_prompt(q, kd, vd, kd_meta, vd_meta, sinks, attn_g, *, n_sub):
    nb, s, kd_cols = kd.shape
    blk = WINDOW
    tq = n_sub * blk
    cur = lambda b, j: (b, j, 0)
    prv = lambda b, j: (b, jnp.maximum(j * n_sub - 1, 0), 0)
    const = lambda b, j: (0, 0)
    return pl.pallas_call(
        _attn_prompt_kernel,
        grid=(nb, s // tq),
        in_specs=[
            pl.BlockSpec(memory_space=pltpu.SMEM),
            pl.BlockSpec((1, tq, N_HEADS * LANES), cur),
            pl.BlockSpec((1, tq, kd_cols), cur),
            pl.BlockSpec((1, blk, kd_cols), prv),
            pl.BlockSpec((1, tq, kd_cols), cur),
            pl.BlockSpec((1, blk, kd_cols), prv),
            pl.BlockSpec((blk, kd_cols), const),
            pl.BlockSpec((blk, kd_cols), const),
            pl.BlockSpec((1, ATTN_DIM), const),
        ],
        out_specs=pl.BlockSpec((1, tq, ATTN_DIM), cur),
        out_shape=jax.ShapeDtypeStruct((nb, s, ATTN_DIM), BF16),
        scratch_shapes=[pltpu.VMEM((tq, ATTN_DIM), F32)],
        compiler_params=pltpu.CompilerParams(
            dimension_semantics=("arbitrary", "arbitrary"), vmem_limit_bytes=VMEM_LIMIT),
        name="attn_prompt",
    )(sinks, q, kd, kd, vd, vd, kd_meta, vd_meta, attn_g)


def _attn_sample_kernel(sinks_ref, q_ref, kn_ref, vn_ref, knt_ref, vnt_ref, ck_ref, cv_ref, ag_ref,
                        ya_ref, nk_ref, nv_ref):
    nseq = q_ref.shape[0]
    base = pl.program_id(0) * nseq
    n_rows = N_HEADS * nseq
    qf = q_ref[...].astype(F32)
    kv_of_lane = lax.broadcasted_iota(jnp.int32, (nseq, KV_DIM), 1) // HEAD_DIM
    q_rows, sink_rows = [], []
    for r in range(N_HEADS):
        i, g = divmod(r, N_KV_HEADS)
        q_rows.append(jnp.where(kv_of_lane == g, qf[:, i * KV_DIM:(i + 1) * KV_DIM], 0.0))
        sink_rows.append(jnp.full((nseq, 1), sinks_ref[g * GROUP + i] * LOG2_E, F32))
    q_all = jnp.concatenate(q_rows, axis=0)
    sink = jnp.concatenate(sink_rows, axis=0)
    q_bf = q_all.astype(BF16)

    key_lane = lax.broadcasted_iota(jnp.int32, (KV_DIM, WINDOW), 1)

    def shifted(cache, new_t, n):
        col = jnp.sum(jnp.where(key_lane == base + n, new_t, 0.0), axis=1, keepdims=True)
        return jnp.where(key_lane == WINDOW - 1, col, pltpu.roll(cache, WINDOW - 1, axis=1))

    seq_of_row = lax.broadcasted_iota(jnp.int32, (n_rows, WINDOW), 0) % nseq
    s_c = jnp.zeros((n_rows, WINDOW), F32)
    for n in range(nseq):
        kt = ck_ref[n]
        s_n = jnp.dot(q_bf, kt.astype(BF16), preferred_element_type=F32)
        s_c = jnp.where(seq_of_row == n, s_n, s_c)
        nk_ref[n] = shifted(kt, knt_ref[...], n)
    kn = jnp.concatenate([kn_ref[...]] * N_HEADS, axis=0)
    vn = jnp.concatenate([vn_ref[...]] * N_HEADS, axis=0)
    s_new = jnp.sum(q_all * kn, axis=-1, keepdims=True)
    m = jnp.maximum(jnp.maximum(jnp.max(s_c, axis=-1, keepdims=True), s_new), sink)
    p_c = jnp.exp2(s_c - m)
    p_new = jnp.exp2(s_new - m)
    denom = jnp.sum(p_c, axis=-1, keepdims=True) + p_new + jnp.exp2(sink - m)
    p_bf = p_c.astype(BF16)

    seq_of_row = lax.broadcasted_iota(jnp.int32, (n_rows, KV_DIM), 0) % nseq
    o = jnp.zeros((n_rows, KV_DIM), F32)
    for n in range(nseq):
        vt = cv_ref[n]
        o_n = lax.dot_general(p_bf, vt.astype(BF16), (((1,), (1,)), ((), ())), preferred_element_type=F32)
        o = jnp.where(seq_of_row == n, o_n, o)
        nv_ref[n] = shifted(vt, vnt_ref[...], n)
    kv_of_row = (lax.broadcasted_iota(jnp.int32, (n_rows, KV_DIM), 0) // nseq) % N_KV_HEADS
    own = (lax.broadcasted_iota(jnp.int32, (n_rows, KV_DIM), 1) // HEAD_DIM) == kv_of_row
    o = jnp.where(own, (o + p_new * vn) / denom, 0.0)
    slabs = []
    for i in range(GROUP):
        lo = i * N_KV_HEADS * nseq
        slabs.append(sum(o[lo + g * nseq:lo + (g + 1) * nseq] for g in range(N_KV_HEADS)))
    ya_ref[...] = _rms(_group_major(jnp.concatenate(slabs, axis=-1)), ag_ref[...]).astype(BF16)


def _attn_sample(q, k_new, v_new, k_new_t, v_new_t, cache_kt, cache_vt, sinks, attn_g, *, nseq):
    n = q.shape[0]
    rows = lambda i: (i, 0)
    seqs = lambda i: (i, 0, 0)
    const = lambda i: (0, 0)
    return pl.pallas_call(
        _attn_sample_kernel,
        grid=(n // nseq,),
        in_specs=[
            pl.BlockSpec(memory_space=pltpu.SMEM),
            pl.BlockSpec((nseq, ATTN_DIM), rows),
            pl.BlockSpec((nseq, KV_DIM), rows),
            pl.BlockSpec((nseq, KV_DIM), rows),
            pl.BlockSpec((KV_DIM, n), const),
            pl.BlockSpec((KV_DIM, n), const),
            pl.BlockSpec((nseq, KV_DIM, WINDOW), seqs),
            pl.BlockSpec((nseq, KV_DIM, WINDOW), seqs),
            pl.BlockSpec((1, ATTN_DIM), const),
        ],
        out_specs=[
            pl.BlockSpec((nseq, ATTN_DIM), rows),
            pl.BlockSpec((nseq, KV_DIM, WINDOW), seqs),
            pl.BlockSpec((nseq, KV_DIM, WINDOW), seqs),
        ],
        out_shape=[
            jax.ShapeDtypeStruct((n, ATTN_DIM), BF16),
            jax.ShapeDtypeStruct((n, KV_DIM, WINDOW), F32),
            jax.ShapeDtypeStruct((n, KV_DIM, WINDOW), F32),
        ],
        compiler_params=pltpu.CompilerParams(
            dimension_semantics=("arbitrary",), vmem_limit_bytes=VMEM_LIMIT),
        name="attn_sample",
    )(sinks, q, k_new, v_new, k_new_t, v_new_t, cache_kt, cache_vt, attn_g)


def _mlp_kernel(chunks, x_ref, yc_ref, ya_ref, wo_ref, g2_ref, wg_hbm, wu_hbm, wd_hbm, gf_ref,
                y_ref, hn_scr, wg_buf, wu_buf, wd_buf, sem):
    i = pl.program_id(0)
    n_tiles = pl.num_programs(0)
    n_slots = wg_buf.shape[0]
    n_chunks = len(chunks)
    ahead = n_slots - 1

    def chunk_copies(c):
        slot = c % n_slots
        off, width = chunks[c]
        cols = pl.ds(off, width)
        return (pltpu.make_async_copy(wg_hbm.at[:, cols], wg_buf.at[slot, :, :width], sem.at[0, slot]),
                pltpu.make_async_copy(wu_hbm.at[:, cols], wu_buf.at[slot, :, :width], sem.at[1, slot]),
                pltpu.make_async_copy(wd_hbm.at[cols, :], wd_buf.at[slot, :width, :], sem.at[2, slot]))

    def start(c):
        for cp in chunk_copies(c):
            cp.start()

    def wait(c):
        for cp in chunk_copies(c):
            cp.wait()

    @pl.when(i == 0)
    def _():
        for c in range(min(ahead, n_chunks)):
            start(c)

    mix = (jnp.dot(yc_ref[...], wo_ref[:CONV_DIM, :], preferred_element_type=F32)
           + jnp.dot(ya_ref[...], wo_ref[CONV_DIM:, :], preferred_element_type=F32))
    h = x_ref[...] + mix
    y_ref[...] = h
    hn_scr[...] = _rms(h, g2_ref[...]).astype(BF16)

    for c in range(n_chunks):
        nxt = c + ahead
        if nxt < n_chunks:
            start(nxt)
        else:
            @pl.when(i + 1 < n_tiles)
            def _():
                start(nxt - n_chunks)
        wait(c)
        slot = c % n_slots
        width = chunks[c][1]
        hn = hn_scr[...]
        gate = jnp.dot(hn, wg_buf[slot, :, :width], preferred_element_type=F32)
        up = jnp.dot(hn, wu_buf[slot, :, :width], preferred_element_type=F32)
        act = (gate * jax.nn.sigmoid(gate) * up).astype(BF16)
        y_ref[...] += jnp.dot(act, wd_buf[slot, :width, :], preferred_element_type=F32)

    y_ref[...] = _rms(y_ref[...], gf_ref[...])


def _mlp_chunks(d_ff, tf):
    bounds = list(range(0, d_ff, tf)) + [d_ff]
    return tuple((lo, hi - lo) for lo, hi in zip(bounds[:-1], bounds[1:]))


def _mlp_call(x, yc, ya, w_out, g2, w_gate, w_up, w_down, gf, *, tm, tf, n_slots, name):
    r = x.shape[0]
    chunks = _mlp_chunks(w_gate.shape[1], tf)
    assert r == tm or len(chunks) % n_slots == 0
    assert len(chunks) >= n_slots and all(w % LANES == 0 for _, w in chunks)
    rows = lambda i: (i, 0)
    const = lambda i: (0, 0)
    return pl.pallas_call(
        functools.partial(_mlp_kernel, chunks),
        grid=(r // tm,),
        in_specs=[
            pl.BlockSpec((tm, D_MODEL), rows),
            pl.BlockSpec((tm, CONV_DIM), rows),
            pl.BlockSpec((tm, ATTN_DIM), rows),
            pl.BlockSpec((D_MODEL, D_MODEL), const, pipeline_mode=pl.Buffered(1)),
            pl.BlockSpec((1, D_MODEL), const),
            pl.BlockSpec(memory_space=pl.ANY),
            pl.BlockSpec(memory_space=pl.ANY),
            pl.BlockSpec(memory_space=pl.ANY),
            pl.BlockSpec((1, D_MODEL), const),
        ],
        out_specs=pl.BlockSpec((tm, D_MODEL), rows),
        out_shape=jax.ShapeDtypeStruct((r, D_MODEL), F32),
        scratch_shapes=[
            pltpu.VMEM((tm, D_MODEL), BF16),
            pltpu.VMEM((n_slots, D_MODEL, tf), BF16),
            pltpu.VMEM((n_slots, D_MODEL, tf), BF16),
            pltpu.VMEM((n_slots, tf, D_MODEL), BF16),
            pltpu.SemaphoreType.DMA((3, n_slots)),
        ],
        compiler_params=pltpu.CompilerParams(
            dimension_semantics=("arbitrary",), vmem_limit_bytes=VMEM_LIMIT),
        name=name,
    )(x, yc, ya, w_out, g2, w_gate, w_up, w_down, gf)


def _rope_tables(pos):
    inv = ROPE_THETA ** (-jnp.arange(HALF_HEAD, dtype=F32) / HALF_HEAD)
    ang = pos.astype(F32)[:, None] * inv[None, :]
    cos = jnp.cos(ang)
    sin = jnp.sin(ang)
    reps = LANES // HEAD_DIM
    return (jnp.tile(jnp.concatenate([cos, cos], axis=-1), (1, reps)),
            jnp.tile(jnp.concatenate([-sin, sin], axis=-1), (1, reps)))


def kernel(x_prompt, x_sample, cache_k, cache_v, state_conv, meta_tokens, norm1_g, w_in, conv_w, conv_norm_g,
           attn_norm_g, attn_sinks, w_out, norm2_g, w_gate, w_up, w_down, final_norm_g):
    depth = w_in.shape[0]
    assert depth == 1, "single-layer step only"
    nb, seq, _ = x_prompt.shape
    ns, dec_seq, _ = x_sample.shape
    assert dec_seq == 1

    g1 = norm1_g[0][None]
    g2 = norm2_g[0][None]
    gf = final_norm_g[None]
    cg = conv_norm_g[0][None]
    ag = attn_norm_g[0][None]
    cw = conv_w[0]
    sinks = attn_sinks[0]
    xs = x_sample.reshape(ns, D_MODEL)
    cos_m, sin_m = _rope_tables(jnp.arange(N_META))
    cos_s, sin_s = _rope_tables(jnp.full((ns,), PAST_LEN))
    prev_s = state_conv[0].reshape(ns, (CONV_W - 1) * CONV_DIM)
    w_in_b, kd_meta, vd_meta, u_m, yc_s, q_s, k_st, v_st, k_s, v_s, u_s = _small_proj_call(
        meta_tokens, xs, g1, w_in[0], cw, cg, cos_m, sin_m, cos_s, sin_s, prev_s)

    cos_p, sin_p = _rope_tables(N_META + jnp.arange(seq))
    yc, q, kd, vd, k_tail, v_tail, u_tail, w_out_b, w_gate_b, w_up_b, w_down_b = _proj_call(
        x_prompt, g1, w_in_b, cw, cg, cos_p, sin_p, u_m, tm=512, n_sub=2,
        cast=(w_out[0], w_gate[0], w_up[0], w_down[0]))
    ya = _attn_prompt(q, kd, vd, kd_meta, vd_meta, sinks, ag, n_sub=16)
    rp = nb * seq
    y_prompt = _mlp_call(x_prompt.reshape(rp, D_MODEL), yc.reshape(rp, CONV_DIM), ya.reshape(rp, ATTN_DIM),
                         w_out_b, g2, w_gate_b, w_up_b, w_down_b, gf, tm=512, tf=1024, n_slots=2, name="mlp_prompt")
    y_prompt = y_prompt.reshape(nb, seq, D_MODEL)
    new_k_prompt = k_tail.reshape(1, nb, WINDOW, N_KV_HEADS, HEAD_DIM)
    new_v_prompt = v_tail.reshape(1, nb, WINDOW, N_KV_HEADS, HEAD_DIM)
    new_conv_prompt = u_tail[None]

    cache_kt = jnp.transpose(cache_k[0], (0, 2, 3, 1)).reshape(ns, KV_DIM, WINDOW)
    cache_vt = jnp.transpose(cache_v[0], (0, 2, 3, 1)).reshape(ns, KV_DIM, WINDOW)
    ya_s, nk_s, nv_s = _attn_sample(q_s, k_s, v_s, k_st, v_st, cache_kt, cache_vt, sinks, ag, nseq=16)
    y_sample = _mlp_call(xs, yc_s, ya_s, w_out_b, g2, w_gate_b, w_up_b, w_down_b, gf,
                         tm=ns, tf=512, n_slots=4, name="mlp_sample")
    y_sample = y_sample.reshape(ns, 1, D_MODEL)
    new_k_sample = jnp.transpose(nk_s.reshape(ns, N_KV_HEADS, HEAD_DIM, WINDOW), (0, 3, 1, 2))[None]
    new_v_sample = jnp.transpose(nv_s.reshape(ns, N_KV_HEADS, HEAD_DIM, WINDOW), (0, 3, 1, 2))[None]
    new_conv_sample = u_s.reshape(1, ns, CONV_W - 1, CONV_DIM)

    return (y_prompt, y_sample, new_k_prompt, new_v_prompt, new_conv_prompt,
            new_k_sample, new_v_sample, new_conv_sample)
```

```python
import functools

import jax
import jax.numpy as jnp
from jax import lax
from jax.experimental import pallas as pl
from jax.experimental.pallas import tpu as pltpu

D_MODEL = 2048
N_META = 16
CONV_DIM = 1024
CONV_W = 3
HEAD_DIM = 64
HALF_HEAD = HEAD_DIM // 2
ATTN_DIM = 1024
N_HEADS = 16
N_KV_HEADS = 4
GROUP = N_HEADS // N_KV_HEADS
KV_DIM = N_KV_HEADS * HEAD_DIM
WINDOW = 128
ROPE_THETA = 10000.0
RMS_EPS = 1e-6
NEG_INF = -1e30
PAST_LEN = 16384
LOG2_E = 1.4426950408889634

COL_BG = 0
COL_CG = CONV_DIM
COL_XC = 2 * CONV_DIM
COL_Q = 3 * CONV_DIM
COL_K = COL_Q + ATTN_DIM
COL_V = COL_K + KV_DIM
IN_COLS = COL_V + KV_DIM

LANES = 128
HEADS_PER_TILE = LANES // HEAD_DIM
VMEM_LIMIT = 60 * 1024 * 1024

F32 = jnp.float32
BF16 = jnp.bfloat16


def _rms(x, g):
    return x * lax.rsqrt(jnp.mean(x * x, axis=-1, keepdims=True) + RMS_EPS) * g


def _lane_tile(t, width):
    return jnp.concatenate([t] * (width // t.shape[-1]), axis=-1)


def _rope(t, cos, sin_signed):
    w = t.shape[-1]
    lane = lax.broadcasted_iota(jnp.int32, t.shape, 1)
    first_half = (lane & (HEAD_DIM - 1)) < HALF_HEAD
    swapped = jnp.where(first_half, pltpu.roll(t, w - HALF_HEAD, axis=1), pltpu.roll(t, HALF_HEAD, axis=1))
    return t * _lane_tile(cos, w) + swapped * _lane_tile(sin_signed, w)


def _dup_heads(pair_tile, half):
    lane = lax.broadcasted_iota(jnp.int32, pair_tile.shape, 1)
    swapped = pltpu.roll(pair_tile, HEAD_DIM, axis=1)
    if half == 0:
        return jnp.where(lane < HEAD_DIM, pair_tile, swapped)
    return jnp.where(lane < HEAD_DIM, swapped, pair_tile)


def _dup_all(t):
    tiles = []
    for g in range(N_KV_HEADS):
        pair, half = divmod(g, HEADS_PER_TILE)
        tiles.append(_dup_heads(t[:, pair * LANES:(pair + 1) * LANES], half))
    return jnp.concatenate(tiles, axis=-1).astype(BF16)


def _spread_heads(q):
    lane = lax.broadcasted_iota(jnp.int32, (q.shape[0], LANES), 1)
    tiles = []
    for h in range(N_HEADS):
        tile, half = divmod(h, HEADS_PER_TILE)
        qt = q[:, tile * LANES:(tile + 1) * LANES]
        keep = (lane < HEAD_DIM) if half == 0 else (lane >= HEAD_DIM)
        tiles.append(jnp.where(keep, qt, jnp.zeros_like(qt)))
    return jnp.concatenate(tiles, axis=-1)


def _head_block(t, h):
    return t[:, h * HEAD_DIM:(h + 1) * HEAD_DIM]


def _group_minor(t):
    return jnp.concatenate([_head_block(t, g * GROUP + i) for i in range(GROUP) for g in range(N_KV_HEADS)], axis=-1)


def _group_major(t):
    return jnp.concatenate([_head_block(t, i * N_KV_HEADS + g) for g in range(N_KV_HEADS) for i in range(GROUP)],
                           axis=-1)


def _conv_mix(u, p2, p1, cw):
    row = lax.broadcasted_iota(jnp.int32, u.shape, 0)
    u1 = jnp.where(row == 0, p1, pltpu.roll(u, 1, axis=0))
    u2 = jnp.where(row == 0, p2, jnp.where(row == 1, p1, pltpu.roll(u, 2, axis=0)))
    return cw[0:1, :] * u2 + cw[1:2, :] * u1 + cw[2:3, :] * u


Q_SCALE = HEAD_DIM ** -0.5 * LOG2_E

N_PROJ_IN = 8
N_PROJ_OUT = 7


def _proj_kernel(n_sub, *refs):
    x_ref, g1_ref, w_ref, cw_ref, cg_ref, cos_ref, sin_ref, prev_ref = refs[:N_PROJ_IN]
    n_cast = (len(refs) - 1 - N_PROJ_IN - N_PROJ_OUT) // 2
    cast_in = refs[N_PROJ_IN:N_PROJ_IN + n_cast]
    yc_ref, q_ref, kd_ref, vd_ref, kt_ref, vt_ref, u_ref = refs[N_PROJ_IN + n_cast:N_PROJ_IN + n_cast + N_PROJ_OUT]
    cast_out = refs[N_PROJ_IN + n_cast + N_PROJ_OUT:-1]
    carry_ref = refs[-1]

    ts = x_ref.shape[1] // n_sub
    cw = cw_ref[...]

    @pl.when(pl.program_id(1) == 0)
    def _():
        carry_ref[...] = prev_ref[...]

    p2 = carry_ref[0:1, :]
    p1 = carry_ref[1:2, :]

    for t in range(n_sub):
        rows = slice(t * ts, (t + 1) * ts)
        hn = _rms(x_ref[0, rows, :], g1_ref[...]).astype(BF16)

        def section(lo, width):
            return jnp.dot(hn, w_ref[:, lo:lo + width], preferred_element_type=F32)

        u = section(COL_CG, CONV_DIM) * section(COL_XC, CONV_DIM)
        cy = _conv_mix(u, p2, p1, cw)
        p2 = u[ts - 2:ts - 1, :]
        p1 = u[ts - 1:, :]
        yc_ref[0, rows, :] = _rms(section(COL_BG, CONV_DIM) * cy, cg_ref[...]).astype(BF16)

        cos = cos_ref[rows, :]
        sin = sin_ref[rows, :]
        q = _rope(section(COL_Q, ATTN_DIM), cos, sin) * Q_SCALE
        k = _rope(section(COL_K, KV_DIM), cos, sin)
        v = section(COL_V, KV_DIM)
        q_ref[0, rows, :] = _spread_heads(q.astype(BF16))
        kd_ref[0, rows, :] = _dup_all(k)
        vd_ref[0, rows, :] = _dup_all(v)

        if t == 0:
            for src, dst in zip(cast_in, cast_out):
                dst[...] = src[...].astype(BF16)

    tail = jnp.concatenate([p2, p1], axis=0)
    carry_ref[...] = tail
    u_ref[0] = tail
    tail_rows = kt_ref.shape[1]
    kt_ref[0] = k[ts - tail_rows:, :]
    vt_ref[0] = v[ts - tail_rows:, :]


def _proj_call(x, g1, w_in, conv_w, conv_g, cos, sin, prev, *, tm, n_sub, cast=()):
    nb, s, _ = x.shape
    n_steps = nb * (s // tm)
    slab = lambda b, i: (b * (s // tm) + i, 0)
    cast_specs = [pl.BlockSpec((w.shape[0] // n_steps, w.shape[1]), slab) for w in cast]
    assert all(w.shape[0] % (n_steps * 16) == 0 for w in cast)
    assert tm // n_sub >= WINDOW
    kd_cols = N_KV_HEADS * LANES
    q_cols = N_HEADS * LANES
    const = lambda b, i: (0, 0)
    rows = lambda b, i: (b, i, 0)
    per_seq = lambda b, i: (b, 0, 0)
    return pl.pallas_call(
        functools.partial(_proj_kernel, n_sub),
        grid=(nb, s // tm),
        in_specs=[
            pl.BlockSpec((1, tm, D_MODEL), rows),
            pl.BlockSpec((1, D_MODEL), const),
            pl.BlockSpec((D_MODEL, IN_COLS), const, pipeline_mode=pl.Buffered(1)),
            pl.BlockSpec((CONV_W, CONV_DIM), const),
            pl.BlockSpec((1, CONV_DIM), const),
            pl.BlockSpec((tm, LANES), lambda b, i: (i, 0)),
            pl.BlockSpec((tm, LANES), lambda b, i: (i, 0)),
            pl.BlockSpec((CONV_W - 1, CONV_DIM), const),
            *cast_specs,
        ],
        out_specs=[
            pl.BlockSpec((1, tm, CONV_DIM), rows),
            pl.BlockSpec((1, tm, q_cols), rows),
            pl.BlockSpec((1, tm, kd_cols), rows),
            pl.BlockSpec((1, tm, kd_cols), rows),
            pl.BlockSpec((1, WINDOW, KV_DIM), per_seq),
            pl.BlockSpec((1, WINDOW, KV_DIM), per_seq),
            pl.BlockSpec((1, CONV_W - 1, CONV_DIM), per_seq),
            *cast_specs,
        ],
        out_shape=[
            jax.ShapeDtypeStruct((nb, s, CONV_DIM), BF16),
            jax.ShapeDtypeStruct((nb, s, q_cols), BF16),
            jax.ShapeDtypeStruct((nb, s, kd_cols), BF16),
            jax.ShapeDtypeStruct((nb, s, kd_cols), BF16),
            jax.ShapeDtypeStruct((nb, WINDOW, KV_DIM), F32),
            jax.ShapeDtypeStruct((nb, WINDOW, KV_DIM), F32),
            jax.ShapeDtypeStruct((nb, CONV_W - 1, CONV_DIM), F32),
            *[jax.ShapeDtypeStruct(w.shape, BF16) for w in cast],
        ],
        scratch_shapes=[pltpu.VMEM((CONV_W - 1, CONV_DIM), F32)],
        compiler_params=pltpu.CompilerParams(
            dimension_semantics=("arbitrary", "arbitrary"), vmem_limit_bytes=VMEM_LIMIT),
        name="proj_prompt",
    )(x, g1, w_in, conv_w, conv_g, cos, sin, prev, *cast)


SMALL_ROWS = 256


def _small_proj_kernel(xm_ref, xs_ref, g1_ref, w_ref, cw_ref, cg_ref, cosm_ref, sinm_ref, coss_ref, sins_ref, prev_ref,
                       wb_ref, kdm_ref, vdm_ref, um_ref, yc_ref, q_ref, kst_ref, vst_ref, ks_ref, vs_ref, us_ref,
                       hn_scr, z_scr):
    j = pl.program_id(0)
    n_meta = xm_ref.shape[0]
    n_all = z_scr.shape[0]

    @pl.when(j == 0)
    def _():
        hn = jnp.concatenate([_rms(xm_ref[...], g1_ref[...]), _rms(xs_ref[...], g1_ref[...])], axis=0).astype(BF16)
        for b in range(hn_scr.shape[0]):
            hn_scr[b] = hn[:, b * SMALL_ROWS:(b + 1) * SMALL_ROWS]
        z_scr[...] = jnp.zeros_like(z_scr)

    wb = w_ref[...].astype(BF16)
    wb_ref[...] = wb
    z_scr[...] += jnp.dot(hn_scr[j], wb, preferred_element_type=F32)

    @pl.when(j == pl.num_programs(0) - 1)
    def _():
        def section(rows, lo, width):
            return z_scr[rows, lo:lo + width]

        cw = cw_ref[...]
        rows = slice(0, n_meta)
        u = section(rows, COL_CG, CONV_DIM) * section(rows, COL_XC, CONV_DIM)
        um_ref[...] = u[n_meta - (CONV_W - 1):, :]
        k = _rope(section(rows, COL_K, KV_DIM), cosm_ref[...], sinm_ref[...])
        v = section(rows, COL_V, KV_DIM)
        pad = jnp.zeros((kdm_ref.shape[0] - n_meta, kdm_ref.shape[1]), BF16)
        kdm_ref[...] = jnp.concatenate([pad, _dup_all(k)], axis=0)
        vdm_ref[...] = jnp.concatenate([pad, _dup_all(v)], axis=0)

        rows = slice(n_meta, n_all)
        u = section(rows, COL_CG, CONV_DIM) * section(rows, COL_XC, CONV_DIM)
        u2 = prev_ref[:, :CONV_DIM]
        u1 = prev_ref[:, CONV_DIM:]
        us_ref[...] = jnp.concatenate([u1, u], axis=-1)
        cy = cw[0:1, :] * u2 + cw[1:2, :] * u1 + cw[2:3, :] * u
        yc_ref[...] = _rms(section(rows, COL_BG, CONV_DIM) * cy, cg_ref[...]).astype(BF16)
        cos = coss_ref[...]
        sin = sins_ref[...]
        q = _rope(section(rows, COL_Q, ATTN_DIM), cos, sin) * Q_SCALE
        k = _rope(section(rows, COL_K, KV_DIM), cos, sin)
        v = section(rows, COL_V, KV_DIM)
        q_ref[...] = _group_minor(q).astype(BF16)
        ks_ref[...] = k
        vs_ref[...] = v
        kst_ref[...] = k.T
        vst_ref[...] = v.T


def _small_proj_call(x_meta, x_sample, g1, w_in, conv_w, conv_g, cos_m, sin_m, cos_s, sin_s, prev_s):
    n_meta = x_meta.shape[0]
    ns = x_sample.shape[0]
    n_blocks = D_MODEL // SMALL_ROWS
    kd_cols = N_KV_HEADS * LANES
    const = lambda j: (0, 0)
    full = lambda shape: pl.BlockSpec(shape, const)
    w_spec = pl.BlockSpec((SMALL_ROWS, IN_COLS), lambda j: (j, 0))
    out_shapes = [
        ((D_MODEL, IN_COLS), BF16),
        ((WINDOW, kd_cols), BF16),
        ((WINDOW, kd_cols), BF16),
        ((CONV_W - 1, CONV_DIM), F32),
        ((ns, CONV_DIM), BF16),
        ((ns, ATTN_DIM), BF16),
        ((KV_DIM, ns), F32),
        ((KV_DIM, ns), F32),
        ((ns, KV_DIM), F32),
        ((ns, KV_DIM), F32),
        ((ns, (CONV_W - 1) * CONV_DIM), F32),
    ]
    return pl.pallas_call(
        _small_proj_kernel,
        grid=(n_blocks,),
        in_specs=[
            full((n_meta, D_MODEL)), full((ns, D_MODEL)), full((1, D_MODEL)), w_spec,
            full((CONV_W, CONV_DIM)), full((1, CONV_DIM)),
            full((n_meta, LANES)), full((n_meta, LANES)), full((ns, LANES)), full((ns, LANES)),
            full((ns, (CONV_W - 1) * CONV_DIM)),
        ],
        out_specs=[w_spec] + [full(shape) for shape, _ in out_shapes[1:]],
        out_shape=[jax.ShapeDtypeStruct(shape, dtype) for shape, dtype in out_shapes],
        scratch_shapes=[pltpu.VMEM((n_blocks, n_meta + ns, SMALL_ROWS), BF16),
                        pltpu.VMEM((n_meta + ns, IN_COLS), F32)],
        compiler_params=pltpu.CompilerParams(
            dimension_semantics=("arbitrary",), vmem_limit_bytes=VMEM_LIMIT),
        name="proj_small",
    )(x_meta, x_sample, g1, w_in, conv_w, conv_g, cos_m, sin_m, cos_s, sin_s, prev_s)


def _attn_prompt_kernel(sinks_ref, q_ref, kc_ref, kp_ref, vc_ref, vp_ref, km_ref, vm_ref, ag_ref,
                        ya_ref, o_scr):
    first = pl.program_id(1) == 0
    blk = WINDOW
    nk = 2 * blk
    n_sub = q_ref.shape[1] // blk

    qi = lax.broadcasted_iota(jnp.int32, (blk, nk), 0)
    kj = lax.broadcasted_iota(jnp.int32, (blk, nk), 1)
    band = (kj >= qi) & (kj <= qi + WINDOW)
    band_first = band & (kj >= jnp.where(first, blk - N_META, 0))

    lane = lax.broadcasted_iota(jnp.int32, (blk, LANES), 1)
    ones = jnp.ones((nk, LANES), BF16)
    for t in range(n_sub):
        rows = slice(t * blk, (t + 1) * blk)
        valid = band_first if t == 0 else band
        for g in range(N_KV_HEADS):
            cols = slice(g * LANES, (g + 1) * LANES)
            if t == 0:
                k_prev = jnp.where(first, km_ref[:, cols], kp_ref[0, :, cols])
                v_prev = jnp.where(first, vm_ref[:, cols], vp_ref[0, :, cols])
            else:
                k_prev = kc_ref[0, (t - 1) * blk:t * blk, cols]
                v_prev = vc_ref[0, (t - 1) * blk:t * blk, cols]
            kd = jnp.concatenate([k_prev, kc_ref[0, rows, cols]], axis=0)
            vd = jnp.concatenate([v_prev, vc_ref[0, rows, cols]], axis=0)
            v_rhs = jnp.concatenate([vd, ones], axis=1)
            for tile in range(g * GROUP // HEADS_PER_TILE, (g + 1) * GROUP // HEADS_PER_TILE):
                ov, m = [], []
                for h in (HEADS_PER_TILE * tile, HEADS_PER_TILE * tile + 1):
                    qm = q_ref[0, rows, h * LANES:(h + 1) * LANES]
                    s = lax.dot_general(qm, kd, (((1,), (1,)), ((), ())), preferred_element_type=F32)
                    s = jnp.where(valid, s, NEG_INF)
                    m.append(jnp.max(s, axis=-1, keepdims=True))
                    p = jnp.exp2(s - m[-1]).astype(BF16)
                    ov.append(jnp.dot(p, v_rhs, preferred_element_type=F32))
                low = lane < HEAD_DIM
                sink = jnp.where(low[:1], sinks_ref[HEADS_PER_TILE * tile], sinks_ref[HEADS_PER_TILE * tile + 1])
                denom = (jnp.where(low, ov[0][:, LANES:], ov[1][:, LANES:])
                         + jnp.exp2(sink * LOG2_E - jnp.where(low, m[0], m[1])))
                o_scr[rows, tile * LANES:(tile + 1) * LANES] = (
                    jnp.where(low, ov[0][:, :LANES], ov[1][:, :LANES]) / denom)
        ya_ref[0, rows, :] = _rms(o_scr[rows, :], ag_ref[...]).astype(BF16)


def _attn_prompt(q, kd, vd, kd_meta, vd_meta, sinks, attn_g, *, n_sub):
    nb, s, kd_cols = kd.shape
    blk = WINDOW
    tq = n_sub * blk
    cur = lambda b, j: (b, j, 0)
    prv = lambda b, j: (b, jnp.maximum(j * n_sub - 1, 0), 0)
    const = lambda b, j: (0, 0)
    return pl.pallas_call(
        _attn_prompt_kernel,
        grid=(nb, s // tq),
        in_specs=[
            pl.BlockSpec(memory_space=pltpu.SMEM),
            pl.BlockSpec((1, tq, N_HEADS * LANES), cur),
            pl.BlockSpec((1, tq, kd_cols), cur),
            pl.BlockSpec((1, blk, kd_cols), prv),
            pl.BlockSpec((1, tq, kd_cols), cur),
            pl.BlockSpec((1, blk, kd_cols), prv),
            pl.BlockSpec((blk, kd_cols), const),
            pl.BlockSpec((blk, kd_cols), const),
            pl.BlockSpec((1, ATTN_DIM), const),
        ],
        out_specs=pl.BlockSpec((1, tq, ATTN_DIM), cur),
        out_shape=jax.ShapeDtypeStruct((nb, s, ATTN_DIM), BF16),
        scratch_shapes=[pltpu.VMEM((tq, ATTN_DIM), F32)],
        compiler_params=pltpu.CompilerParams(
            dimension_semantics=("arbitrary", "arbitrary"), vmem_limit_bytes=VMEM_LIMIT),
        name="attn_prompt",
    )(sinks, q, kd, kd, vd, vd, kd_meta, vd_meta, attn_g)


def _attn_sample_kernel(sinks_ref, q_ref, kn_ref, vn_ref, knt_ref, vnt_ref, ck_ref, cv_ref, ag_ref,
                        ya_ref, nk_ref, nv_ref):
    nseq = q_ref.shape[0]
    base = pl.program_id(0) * nseq
    n_rows = N_HEADS * nseq
    qf = q_ref[...].astype(F32)
    kv_of_lane = lax.broadcasted_iota(jnp.int32, (nseq, KV_DIM), 1) // HEAD_DIM
    q_rows, sink_rows = [], []
    for r in range(N_HEADS):
        i, g = divmod(r, N_KV_HEADS)
        q_rows.append(jnp.where(kv_of_lane == g, qf[:, i * KV_DIM:(i + 1) * KV_DIM], 0.0))
        sink_rows.append(jnp.full((nseq, 1), sinks_ref[g * GROUP + i] * LOG2_E, F32))
    q_all = jnp.concatenate(q_rows, axis=0)
    sink = jnp.concatenate(sink_rows, axis=0)
    q_bf = q_all.astype(BF16)

    key_lane = lax.broadcasted_iota(jnp.int32, (KV_DIM, WINDOW), 1)

    def shifted(cache, new_t, n):
        col = jnp.sum(jnp.where(key_lane == base + n, new_t, 0.0), axis=1, keepdims=True)
        return jnp.where(key_lane == WINDOW - 1, col, pltpu.roll(cache, WINDOW - 1, axis=1))

    seq_of_row = lax.broadcasted_iota(jnp.int32, (n_rows, WINDOW), 0) % nseq
    s_c = jnp.zeros((n_rows, WINDOW), F32)
    for n in range(nseq):
        kt = ck_ref[n]
        s_n = jnp.dot(q_bf, kt.astype(BF16), preferred_element_type=F32)
        s_c = jnp.where(seq_of_row == n, s_n, s_c)
        nk_ref[n] = shifted(kt, knt_ref[...], n)
    kn = jnp.concatenate([kn_ref[...]] * N_HEADS, axis=0)
    vn = jnp.concatenate([vn_ref[...]] * N_HEADS, axis=0)
    s_new = jnp.sum(q_all * kn, axis=-1, keepdims=True)
    m = jnp.maximum(jnp.maximum(jnp.max(s_c, axis=-1, keepdims=True), s_new), sink)
    p_c = jnp.exp2(s_c - m)
    p_new = jnp.exp2(s_new - m)
    denom = jnp.sum(p_c, axis=-1, keepdims=True) + p_new + jnp.exp2(sink - m)
    p_bf = p_c.astype(BF16)

    seq_of_row = lax.broadcasted_iota(jnp.int32, (n_rows, KV_DIM), 0) % nseq
    o = jnp.zeros((n_rows, KV_DIM), F32)
    for n in range(nseq):
        vt = cv_ref[n]
        o_n = lax.dot_general(p_bf, vt.astype(BF16), (((1,), (1,)), ((), ())), preferred_element_type=F32)
        o = jnp.where(seq_of_row == n, o_n, o)
        nv_ref[n] = shifted(vt, vnt_ref[...], n)
    kv_of_row = (lax.broadcasted_iota(jnp.int32, (n_rows, KV_DIM), 0) // nseq) % N_KV_HEADS
    own = (lax.broadcasted_iota(jnp.int32, (n_rows, KV_DIM), 1) // HEAD_DIM) == kv_of_row
    o = jnp.where(own, (o + p_new * vn) / denom, 0.0)
    slabs = []
    for i in range(GROUP):
        lo = i * N_KV_HEADS * nseq
        slabs.append(sum(o[lo + g * nseq:lo + (g + 1) * nseq] for g in range(N_KV_HEADS)))
    ya_ref[...] = _rms(_group_major(jnp.concatenate(slabs, axis=-1)), ag_ref[...]).astype(BF16)


def _attn_sample(q, k_new, v_new, k_new_t, v_new_t, cache_kt, cache_vt, sinks, attn_g, *, nseq):
    n = q.shape[0]
    rows = lambda i: (i, 0)
    seqs = lambda i: (i, 0, 0)
    const = lambda i: (0, 0)
    return pl.pallas_call(
        _attn_sample_kernel,
        grid=(n // nseq,),
        in_specs=[
            pl.BlockSpec(memory_space=pltpu.SMEM),
            pl.BlockSpec((nseq, ATTN_DIM), rows),
            pl.BlockSpec((nseq, KV_DIM), rows),
            pl.BlockSpec((nseq, KV_DIM), rows),
            pl.BlockSpec((KV_DIM, n), const),
            pl.BlockSpec((KV_DIM, n), const),
            pl.BlockSpec((nseq, KV_DIM, WINDOW), seqs),
            pl.BlockSpec((nseq, KV_DIM, WINDOW), seqs),
            pl.BlockSpec((1, ATTN_DIM), const),
        ],
        out_specs=[
            pl.BlockSpec((nseq, ATTN_DIM), rows),
            pl.BlockSpec((nseq, KV_DIM, WINDOW), seqs),
            pl.BlockSpec((nseq, KV_DIM, WINDOW), seqs),
        ],
        out_shape=[
            jax.ShapeDtypeStruct((n, ATTN_DIM), BF16),
            jax.ShapeDtypeStruct((n, KV_DIM, WINDOW), F32),
            jax.ShapeDtypeStruct((n, KV_DIM, WINDOW), F32),
        ],
        compiler_params=pltpu.CompilerParams(
            dimension_semantics=("arbitrary",), vmem_limit_bytes=VMEM_LIMIT),
        name="attn_sample",
    )(sinks, q, k_new, v_new, k_new_t, v_new_t, cache_kt, cache_vt, attn_g)


def _mlp_kernel(chunks, x_ref, yc_ref, ya_ref, wo_ref, g2_ref, wg_hbm, wu_hbm, wd_hbm, gf_ref,
                y_ref, hn_scr, wg_buf, wu_buf, wd_buf, sem):
    i = pl.program_id(0)
    n_tiles = pl.num_programs(0)
    n_slots = wg_buf.shape[0]
    n_chunks = len(chunks)
    ahead = n_slots - 1

    def chunk_copies(c):
        slot = c % n_slots
        off, width = chunks[c]
        cols = pl.ds(off, width)
        return (pltpu.make_async_copy(wg_hbm.at[:, cols], wg_buf.at[slot, :, :width], sem.at[0, slot]),
                pltpu.make_async_copy(wu_hbm.at[:, cols], wu_buf.at[slot, :, :width], sem.at[1, slot]),
                pltpu.make_async_copy(wd_hbm.at[cols, :], wd_buf.at[slot, :width, :], sem.at[2, slot]))

    def start(c):
        for cp in chunk_copies(c):
            cp.start()

    def wait(c):
        for cp in chunk_copies(c):
            cp.wait()

    @pl.when(i == 0)
    def _():
        for c in range(min(ahead, n_chunks)):
            start(c)

    mix = (jnp.dot(yc_ref[...], wo_ref[:CONV_DIM, :], preferred_element_type=F32)
           + jnp.dot(ya_ref[...], wo_ref[CONV_DIM:, :], preferred_element_type=F32))
    h = x_ref[...] + mix
    y_ref[...] = h
    hn_scr[...] = _rms(h, g2_ref[...]).astype(BF16)

    for c in range(n_chunks):
        nxt = c + ahead
        if nxt < n_chunks:
            start(nxt)
        else:
            @pl.when(i + 1 < n_tiles)
            def _():
                start(nxt - n_chunks)
        wait(c)
        slot = c % n_slots
        width = chunks[c][1]
        hn = hn_scr[...]
        gate = jnp.dot(hn, wg_buf[slot, :, :width], preferred_element_type=F32)
        up = jnp.dot(hn, wu_buf[slot, :, :width], preferred_element_type=F32)
        act = (gate * jax.nn.sigmoid(gate) * up).astype(BF16)
        y_ref[...] += jnp.dot(act, wd_buf[slot, :width, :], preferred_element_type=F32)

    y_ref[...] = _rms(y_ref[...], gf_ref[...])


def _mlp_chunks(d_ff, tf):
    bounds = list(range(0, d_ff, tf)) + [d_ff]
    return tuple((lo, hi - lo) for lo, hi in zip(bounds[:-1], bounds[1:]))


def _mlp_call(x, yc, ya, w_out, g2, w_gate, w_up, w_down, gf, *, tm, tf, n_slots, name):
    r = x.shape[0]
    chunks = _mlp_chunks(w_gate.shape[1], tf)
    assert r == tm or len(chunks) % n_slots == 0
    assert len(chunks) >= n_slots and all(w % LANES == 0 for _, w in chunks)
    rows = lambda i: (i, 0)
    const = lambda i: (0, 0)
    return pl.pallas_call(
        functools.partial(_mlp_kernel, chunks),
        grid=(r // tm,),
        in_specs=[
            pl.BlockSpec((tm, D_MODEL), rows),
            pl.BlockSpec((tm, CONV_DIM), rows),
            pl.BlockSpec((tm, ATTN_DIM), rows),
            pl.BlockSpec((D_MODEL, D_MODEL), const, pipeline_mode=pl.Buffered(1)),
            pl.BlockSpec((1, D_MODEL), const),
            pl.BlockSpec(memory_space=pl.ANY),
            pl.BlockSpec(memory_space=pl.ANY),
            pl.BlockSpec(memory_space=pl.ANY),
            pl.BlockSpec((1, D_MODEL), const),
        ],
        out_specs=pl.BlockSpec((tm, D_MODEL), rows),
        out_shape=jax.ShapeDtypeStruct((r, D_MODEL), F32),
        scratch_shapes=[
            pltpu.VMEM((tm, D_MODEL), BF16),
            pltpu.VMEM((n_slots, D_MODEL, tf), BF16),
            pltpu.VMEM((n_slots, D_MODEL, tf), BF16),
            pltpu.VMEM((n_slots, tf, D_MODEL), BF16),
            pltpu.SemaphoreType.DMA((3, n_slots)),
        ],
        compiler_params=pltpu.CompilerParams(
            dimension_semantics=("arbitrary",), vmem_limit_bytes=VMEM_LIMIT),
        name=name,
    )(x, yc, ya, w_out, g2, w_gate, w_up, w_down, gf)


def _rope_tables(pos):
    inv = ROPE_THETA ** (-jnp.arange(HALF_HEAD, dtype=F32) / HALF_HEAD)
    ang = pos.astype(F32)[:, None] * inv[None, :]
    cos = jnp.cos(ang)
    sin = jnp.sin(ang)
    reps = LANES // HEAD_DIM
    return (jnp.tile(jnp.concatenate([cos, cos], axis=-1), (1, reps)),
            jnp.tile(jnp.concatenate([-sin, sin], axis=-1), (1, reps)))


def kernel(x_prompt, x_sample, cache_k, cache_v, state_conv, meta_tokens, norm1_g, w_in, conv_w, conv_norm_g,
           attn_norm_g, attn_sinks, w_out, norm2_g, w_gate, w_up, w_down, final_norm_g):
    depth = w_in.shape[0]
    assert depth == 1, "single-layer step only"
    nb, seq, _ = x_prompt.shape
    ns, dec_seq, _ = x_sample.shape
    assert dec_seq == 1

    g1 = norm1_g[0][None]
    g2 = norm2_g[0][None]
    gf = final_norm_g[None]
    cg = conv_norm_g[0][None]
    ag = attn_norm_g[0][None]
    cw = conv_w[0]
    sinks = attn_sinks[0]
    xs = x_sample.reshape(ns, D_MODEL)
    cos_m, sin_m = _rope_tables(jnp.arange(N_META))
    cos_s, sin_s = _rope_tables(jnp.full((ns,), PAST_LEN))
    prev_s = state_conv[0].reshape(ns, (CONV_W - 1) * CONV_DIM)
    w_in_b, kd_meta, vd_meta, u_m, yc_s, q_s, k_st, v_st, k_s, v_s, u_s = _small_proj_call(
        meta_tokens, xs, g1, w_in[0], cw, cg, cos_m, sin_m, cos_s, sin_s, prev_s)

    cos_p, sin_p = _rope_tables(N_META + jnp.arange(seq))
    yc, q, kd, vd, k_tail, v_tail, u_tail, w_out_b, w_gate_b, w_up_b, w_down_b = _proj_call(
        x_prompt, g1, w_in_b, cw, cg, cos_p, sin_p, u_m, tm=512, n_sub=2,
        cast=(w_out[0], w_gate[0], w_up[0], w_down[0]))
    ya = _attn_prompt(q, kd, vd, kd_meta, vd_meta, sinks, ag, n_sub=8)
    rp = nb * seq
    y_prompt = _mlp_call(x_prompt.reshape(rp, D_MODEL), yc.reshape(rp, CONV_DIM), ya.reshape(rp, ATTN_DIM),
                         w_out_b, g2, w_gate_b, w_up_b, w_down_b, gf, tm=512, tf=1024, n_slots=2, name="mlp_prompt")
    y_prompt = y_prompt.reshape(nb, seq, D_MODEL)
    new_k_prompt = k_tail.reshape(1, nb, WINDOW, N_KV_HEADS, HEAD_DIM)
    new_v_prompt = v_tail.reshape(1, nb, WINDOW, N_KV_HEADS, HEAD_DIM)
    new_conv_prompt = u_tail[None]

    cache_kt = jnp.transpose(cache_k[0], (0, 2, 3, 1)).reshape(ns, KV_DIM, WINDOW)
    cache_vt = jnp.transpose(cache_v[0], (0, 2, 3, 1)).reshape(ns, KV_DIM, WINDOW)
    ya_s, nk_s, nv_s = _attn_sample(q_s, k_s, v_s, k_st, v_st, cache_kt, cache_vt, sinks, ag, nseq=16)
    y_sample = _mlp_call(xs, yc_s, ya_s, w_out_b, g2, w_gate_b, w_up_b, w_down_b, gf,
                         tm=ns, tf=512, n_slots=4, name="mlp_sample")
    y_sample = y_sample.reshape(ns, 1, D_MODEL)
    new_k_sample = jnp.transpose(nk_s.reshape(ns, N_KV_HEADS, HEAD_DIM, WINDOW), (0, 3, 1, 2))[None]
    new_v_sample = jnp.transpose(nv_s.reshape(ns, N_KV_HEADS, HEAD_DIM, WINDOW), (0, 3, 1, 2))[None]
    new_conv_sample = u_s.reshape(1, ns, CONV_W - 1, CONV_DIM)

    return (y_prompt, y_sample, new_k_prompt, new_v_prompt, new_conv_prompt,
            new_k_sample, new_v_sample, new_conv_sample)
```

```python
import functools

import jax
import jax.numpy as jnp
from jax import lax
from jax.experimental import pallas as pl
from jax.experimental.pallas import tpu as pltpu

D_MODEL = 2048
N_META = 16
CONV_DIM = 1024
CONV_W = 3
HEAD_DIM = 64
HALF_HEAD = HEAD_DIM // 2
ATTN_DIM = 1024
N_HEADS = 16
N_KV_HEADS = 4
GROUP = N_HEADS // N_KV_HEADS
KV_DIM = N_KV_HEADS * HEAD_DIM
WINDOW = 128
ROPE_THETA = 10000.0
RMS_EPS = 1e-6
NEG_INF = -1e30
PAST_LEN = 16384
LOG2_E = 1.4426950408889634

COL_BG = 0
COL_CG = CONV_DIM
COL_XC = 2 * CONV_DIM
COL_Q = 3 * CONV_DIM
COL_K = COL_Q + ATTN_DIM
COL_V = COL_K + KV_DIM
IN_COLS = COL_V + KV_DIM

LANES = 128
BF16_ROWS_PER_TILE = 16
HEADS_PER_TILE = LANES // HEAD_DIM
VMEM_LIMIT = 60 * 1024 * 1024

PROJ_ROWS = 512
PROJ_SUB_TILES = 2
ATTN_BLOCKS_PER_STEP = 8
SAMPLE_SEQS_PER_STEP = 16
MLP_ROWS = 512
MLP_CHUNK = 1024
MLP_SLOTS = 2
SAMPLE_MLP_CHUNK = 512
SAMPLE_MLP_SLOTS = 4

F32 = jnp.float32
BF16 = jnp.bfloat16


def _rms(x, g):
    return x * lax.rsqrt(jnp.mean(x * x, axis=-1, keepdims=True) + RMS_EPS) * g


def _lane_tile(t, width):
    return jnp.concatenate([t] * (width // t.shape[-1]), axis=-1)


def _rope(t, cos, sin_signed):
    w = t.shape[-1]
    lane = lax.broadcasted_iota(jnp.int32, t.shape, 1)
    first_half = (lane & (HEAD_DIM - 1)) < HALF_HEAD
    swapped = jnp.where(first_half, pltpu.roll(t, w - HALF_HEAD, axis=1), pltpu.roll(t, HALF_HEAD, axis=1))
    return t * _lane_tile(cos, w) + swapped * _lane_tile(sin_signed, w)


def _dup_heads(pair_tile, half):
    lane = lax.broadcasted_iota(jnp.int32, pair_tile.shape, 1)
    swapped = pltpu.roll(pair_tile, HEAD_DIM, axis=1)
    if half == 0:
        return jnp.where(lane < HEAD_DIM, pair_tile, swapped)
    return jnp.where(lane < HEAD_DIM, swapped, pair_tile)


def _dup_all(t):
    tiles = []
    for g in range(N_KV_HEADS):
        pair, half = divmod(g, HEADS_PER_TILE)
        tiles.append(_dup_heads(t[:, pair * LANES:(pair + 1) * LANES], half))
    return jnp.concatenate(tiles, axis=-1).astype(BF16)


def _spread_heads(q):
    lane = lax.broadcasted_iota(jnp.int32, (q.shape[0], LANES), 1)
    tiles = []
    for h in range(N_HEADS):
        tile, half = divmod(h, HEADS_PER_TILE)
        qt = q[:, tile * LANES:(tile + 1) * LANES]
        keep = (lane < HEAD_DIM) if half == 0 else (lane >= HEAD_DIM)
        tiles.append(jnp.where(keep, qt, jnp.zeros_like(qt)))
    return jnp.concatenate(tiles, axis=-1)


def _head_block(t, h):
    return t[:, h * HEAD_DIM:(h + 1) * HEAD_DIM]


def _group_minor(t):
    return jnp.concatenate([_head_block(t, g * GROUP + i) for i in range(GROUP) for g in range(N_KV_HEADS)], axis=-1)


def _group_major(t):
    return jnp.concatenate([_head_block(t, i * N_KV_HEADS + g) for g in range(N_KV_HEADS) for i in range(GROUP)],
                           axis=-1)


def _conv_mix(u, p2, p1, cw):
    row = lax.broadcasted_iota(jnp.int32, u.shape, 0)
    u1 = jnp.where(row == 0, p1, pltpu.roll(u, 1, axis=0))
    u2 = jnp.where(row == 0, p2, jnp.where(row == 1, p1, pltpu.roll(u, 2, axis=0)))
    return cw[0:1, :] * u2 + cw[1:2, :] * u1 + cw[2:3, :] * u


Q_SCALE = HEAD_DIM ** -0.5 * LOG2_E

N_PROJ_IN = 8
N_PROJ_OUT = 7


def _proj_kernel(n_sub, *refs):
    x_ref, g1_ref, w_ref, cw_ref, cg_ref, cos_ref, sin_ref, prev_ref = refs[:N_PROJ_IN]
    n_cast = (len(refs) - 1 - N_PROJ_IN - N_PROJ_OUT) // 2
    cast_in = refs[N_PROJ_IN:N_PROJ_IN + n_cast]
    yc_ref, q_ref, kd_ref, vd_ref, kt_ref, vt_ref, u_ref = refs[N_PROJ_IN + n_cast:N_PROJ_IN + n_cast + N_PROJ_OUT]
    cast_out = refs[N_PROJ_IN + n_cast + N_PROJ_OUT:-1]
    carry_ref = refs[-1]

    ts = x_ref.shape[1] // n_sub
    cw = cw_ref[...]

    @pl.when(pl.program_id(1) == 0)
    def _():
        carry_ref[...] = prev_ref[...]

    p2 = carry_ref[0:1, :]
    p1 = carry_ref[1:2, :]

    for t in range(n_sub):
        rows = slice(t * ts, (t + 1) * ts)
        hn = _rms(x_ref[0, rows, :], g1_ref[...]).astype(BF16)

        def section(lo, width):
            return jnp.dot(hn, w_ref[:, lo:lo + width], preferred_element_type=F32)

        u = section(COL_CG, CONV_DIM) * section(COL_XC, CONV_DIM)
        cy = _conv_mix(u, p2, p1, cw)
        p2 = u[ts - 2:ts - 1, :]
        p1 = u[ts - 1:, :]
        yc_ref[0, rows, :] = _rms(section(COL_BG, CONV_DIM) * cy, cg_ref[...]).astype(BF16)

        cos = cos_ref[rows, :]
        sin = sin_ref[rows, :]
        q = _rope(section(COL_Q, ATTN_DIM), cos, sin) * Q_SCALE
        k = _rope(section(COL_K, KV_DIM), cos, sin)
        v = section(COL_V, KV_DIM)
        q_ref[0, rows, :] = _spread_heads(q.astype(BF16))
        kd_ref[0, rows, :] = _dup_all(k)
        vd_ref[0, rows, :] = _dup_all(v)

        if t == 0:
            for src, dst in zip(cast_in, cast_out):
                dst[...] = src[...].astype(BF16)

    tail = jnp.concatenate([p2, p1], axis=0)
    carry_ref[...] = tail
    u_ref[0] = tail
    tail_rows = kt_ref.shape[1]
    kt_ref[0] = k[ts - tail_rows:, :]
    vt_ref[0] = v[ts - tail_rows:, :]


def _proj_call(x, g1, w_in, conv_w, conv_g, cos, sin, prev, *, tm, n_sub, cast=()):
    nb, s, _ = x.shape
    n_steps = nb * (s // tm)
    slab = lambda b, i: (b * (s // tm) + i, 0)
    cast_specs = [pl.BlockSpec((w.shape[0] // n_steps, w.shape[1]), slab) for w in cast]
    assert all(w.shape[0] % (n_steps * BF16_ROWS_PER_TILE) == 0 for w in cast)
    assert tm // n_sub >= WINDOW
    kd_cols = N_KV_HEADS * LANES
    q_cols = N_HEADS * LANES
    const = lambda b, i: (0, 0)
    rows = lambda b, i: (b, i, 0)
    per_seq = lambda b, i: (b, 0, 0)
    return pl.pallas_call(
        functools.partial(_proj_kernel, n_sub),
        grid=(nb, s // tm),
        in_specs=[
            pl.BlockSpec((1, tm, D_MODEL), rows),
            pl.BlockSpec((1, D_MODEL), const),
            pl.BlockSpec((D_MODEL, IN_COLS), const, pipeline_mode=pl.Buffered(1)),
            pl.BlockSpec((CONV_W, CONV_DIM), const),
            pl.BlockSpec((1, CONV_DIM), const),
            pl.BlockSpec((tm, LANES), lambda b, i: (i, 0)),
            pl.BlockSpec((tm, LANES), lambda b, i: (i, 0)),
            pl.BlockSpec((CONV_W - 1, CONV_DIM), const),
            *cast_specs,
        ],
        out_specs=[
            pl.BlockSpec((1, tm, CONV_DIM), rows),
            pl.BlockSpec((1, tm, q_cols), rows),
            pl.BlockSpec((1, tm, kd_cols), rows),
            pl.BlockSpec((1, tm, kd_cols), rows),
            pl.BlockSpec((1, WINDOW, KV_DIM), per_seq),
            pl.BlockSpec((1, WINDOW, KV_DIM), per_seq),
            pl.BlockSpec((1, CONV_W - 1, CONV_DIM), per_seq),
            *cast_specs,
        ],
        out_shape=[
            jax.ShapeDtypeStruct((nb, s, CONV_DIM), BF16),
            jax.ShapeDtypeStruct((nb, s, q_cols), BF16),
            jax.ShapeDtypeStruct((nb, s, kd_cols), BF16),
            jax.ShapeDtypeStruct((nb, s, kd_cols), BF16),
            jax.ShapeDtypeStruct((nb, WINDOW, KV_DIM), F32),
            jax.ShapeDtypeStruct((nb, WINDOW, KV_DIM), F32),
            jax.ShapeDtypeStruct((nb, CONV_W - 1, CONV_DIM), F32),
            *[jax.ShapeDtypeStruct(w.shape, BF16) for w in cast],
        ],
        scratch_shapes=[pltpu.VMEM((CONV_W - 1, CONV_DIM), F32)],
        compiler_params=pltpu.CompilerParams(
            dimension_semantics=("arbitrary", "arbitrary"), vmem_limit_bytes=VMEM_LIMIT),
        name="proj_prompt",
    )(x, g1, w_in, conv_w, conv_g, cos, sin, prev, *cast)


SMALL_ROWS = 256


def _small_proj_kernel(xm_ref, xs_ref, g1_ref, w_ref, cw_ref, cg_ref, cosm_ref, sinm_ref, coss_ref, sins_ref, prev_ref,
                       wb_ref, kdm_ref, vdm_ref, um_ref, yc_ref, q_ref, kst_ref, vst_ref, ks_ref, vs_ref, us_ref,
                       hn_scr, z_scr):
    j = pl.program_id(0)
    n_meta = xm_ref.shape[0]
    n_all = z_scr.shape[0]

    @pl.when(j == 0)
    def _():
        hn = jnp.concatenate([_rms(xm_ref[...], g1_ref[...]), _rms(xs_ref[...], g1_ref[...])], axis=0).astype(BF16)
        for b in range(hn_scr.shape[0]):
            hn_scr[b] = hn[:, b * SMALL_ROWS:(b + 1) * SMALL_ROWS]
        z_scr[...] = jnp.zeros_like(z_scr)

    wb = w_ref[...].astype(BF16)
    wb_ref[...] = wb
    z_scr[...] += jnp.dot(hn_scr[j], wb, preferred_element_type=F32)

    @pl.when(j == pl.num_programs(0) - 1)
    def _():
        def section(rows, lo, width):
            return z_scr[rows, lo:lo + width]

        cw = cw_ref[...]
        rows = slice(0, n_meta)
        u = section(rows, COL_CG, CONV_DIM) * section(rows, COL_XC, CONV_DIM)
        um_ref[...] = u[n_meta - (CONV_W - 1):, :]
        k = _rope(section(rows, COL_K, KV_DIM), cosm_ref[...], sinm_ref[...])
        v = section(rows, COL_V, KV_DIM)
        pad = jnp.zeros((kdm_ref.shape[0] - n_meta, kdm_ref.shape[1]), BF16)
        kdm_ref[...] = jnp.concatenate([pad, _dup_all(k)], axis=0)
        vdm_ref[...] = jnp.concatenate([pad, _dup_all(v)], axis=0)

        rows = slice(n_meta, n_all)
        u = section(rows, COL_CG, CONV_DIM) * section(rows, COL_XC, CONV_DIM)
        u2 = prev_ref[:, :CONV_DIM]
        u1 = prev_ref[:, CONV_DIM:]
        us_ref[...] = jnp.concatenate([u1, u], axis=-1)
        cy = cw[0:1, :] * u2 + cw[1:2, :] * u1 + cw[2:3, :] * u
        yc_ref[...] = _rms(section(rows, COL_BG, CONV_DIM) * cy, cg_ref[...]).astype(BF16)
        cos = coss_ref[...]
        sin = sins_ref[...]
        q = _rope(section(rows, COL_Q, ATTN_DIM), cos, sin) * Q_SCALE
        k = _rope(section(rows, COL_K, KV_DIM), cos, sin)
        v = section(rows, COL_V, KV_DIM)
        q_ref[...] = _group_minor(q).astype(BF16)
        ks_ref[...] = k
        vs_ref[...] = v
        kst_ref[...] = k.T
        vst_ref[...] = v.T


def _small_proj_call(x_meta, x_sample, g1, w_in, conv_w, conv_g, cos_m, sin_m, cos_s, sin_s, prev_s):
    n_meta = x_meta.shape[0]
    ns = x_sample.shape[0]
    n_blocks = D_MODEL // SMALL_ROWS
    kd_cols = N_KV_HEADS * LANES
    const = lambda j: (0, 0)
    full = lambda shape: pl.BlockSpec(shape, const)
    w_spec = pl.BlockSpec((SMALL_ROWS, IN_COLS), lambda j: (j, 0))
    out_shapes = [
        ((D_MODEL, IN_COLS), BF16),
        ((WINDOW, kd_cols), BF16),
        ((WINDOW, kd_cols), BF16),
        ((CONV_W - 1, CONV_DIM), F32),
        ((ns, CONV_DIM), BF16),
        ((ns, ATTN_DIM), BF16),
        ((KV_DIM, ns), F32),
        ((KV_DIM, ns), F32),
        ((ns, KV_DIM), F32),
        ((ns, KV_DIM), F32),
        ((ns, (CONV_W - 1) * CONV_DIM), F32),
    ]
    return pl.pallas_call(
        _small_proj_kernel,
        grid=(n_blocks,),
        in_specs=[
            full((n_meta, D_MODEL)), full((ns, D_MODEL)), full((1, D_MODEL)), w_spec,
            full((CONV_W, CONV_DIM)), full((1, CONV_DIM)),
            full((n_meta, LANES)), full((n_meta, LANES)), full((ns, LANES)), full((ns, LANES)),
            full((ns, (CONV_W - 1) * CONV_DIM)),
        ],
        out_specs=[w_spec] + [full(shape) for shape, _ in out_shapes[1:]],
        out_shape=[jax.ShapeDtypeStruct(shape, dtype) for shape, dtype in out_shapes],
        scratch_shapes=[pltpu.VMEM((n_blocks, n_meta + ns, SMALL_ROWS), BF16),
                        pltpu.VMEM((n_meta + ns, IN_COLS), F32)],
        compiler_params=pltpu.CompilerParams(
            dimension_semantics=("arbitrary",), vmem_limit_bytes=VMEM_LIMIT),
        name="proj_small",
    )(x_meta, x_sample, g1, w_in, conv_w, conv_g, cos_m, sin_m, cos_s, sin_s, prev_s)


def _attn_prompt_kernel(sinks_ref, q_ref, kc_ref, kp_ref, vc_ref, vp_ref, km_ref, vm_ref, ag_ref,
                        ya_ref, o_scr):
    first = pl.program_id(1) == 0
    blk = WINDOW
    nk = 2 * blk
    n_sub = q_ref.shape[1] // blk

    qi = lax.broadcasted_iota(jnp.int32, (blk, nk), 0)
    kj = lax.broadcasted_iota(jnp.int32, (blk, nk), 1)
    band = (kj >= qi) & (kj <= qi + WINDOW)
    band_first = band & (kj >= jnp.where(first, blk - N_META, 0))

    lane = lax.broadcasted_iota(jnp.int32, (blk, LANES), 1)
    ones = jnp.ones((nk, LANES), BF16)
    for t in range(n_sub):
        rows = slice(t * blk, (t + 1) * blk)
        valid = band_first if t == 0 else band
        for g in range(N_KV_HEADS):
            cols = slice(g * LANES, (g + 1) * LANES)
            if t == 0:
                k_prev = jnp.where(first, km_ref[:, cols], kp_ref[0, :, cols])
                v_prev = jnp.where(first, vm_ref[:, cols], vp_ref[0, :, cols])
            else:
                k_prev = kc_ref[0, (t - 1) * blk:t * blk, cols]
                v_prev = vc_ref[0, (t - 1) * blk:t * blk, cols]
            kd = jnp.concatenate([k_prev, kc_ref[0, rows, cols]], axis=0)
            vd = jnp.concatenate([v_prev, vc_ref[0, rows, cols]], axis=0)
            v_rhs = jnp.concatenate([vd, ones], axis=1)
            for tile in range(g * GROUP // HEADS_PER_TILE, (g + 1) * GROUP // HEADS_PER_TILE):
                ov, m = [], []
                for h in (HEADS_PER_TILE * tile, HEADS_PER_TILE * tile + 1):
                    qm = q_ref[0, rows, h * LANES:(h + 1) * LANES]
                    s = lax.dot_general(qm, kd, (((1,), (1,)), ((), ())), preferred_element_type=F32)
                    s = jnp.where(valid, s, NEG_INF)
                    m.append(jnp.max(s, axis=-1, keepdims=True))
                    p = jnp.exp2(s - m[-1]).astype(BF16)
                    ov.append(jnp.dot(p, v_rhs, preferred_element_type=F32))
                low = lane < HEAD_DIM
                sink = jnp.where(low[:1], sinks_ref[HEADS_PER_TILE * tile], sinks_ref[HEADS_PER_TILE * tile + 1])
                denom = (jnp.where(low, ov[0][:, LANES:], ov[1][:, LANES:])
                         + jnp.exp2(sink * LOG2_E - jnp.where(low, m[0], m[1])))
                o_scr[rows, tile * LANES:(tile + 1) * LANES] = (
                    jnp.where(low, ov[0][:, :LANES], ov[1][:, :LANES]) / denom)
        ya_ref[0, rows, :] = _rms(o_scr[rows, :], ag_ref[...]).astype(BF16)


def _attn_prompt(q, kd, vd, kd_meta, vd_meta, sinks, attn_g, *, n_sub):
    nb, s, kd_cols = kd.shape
    blk = WINDOW
    tq = n_sub * blk
    cur = lambda b, j: (b, j, 0)
    prv = lambda b, j: (b, jnp.maximum(j * n_sub - 1, 0), 0)
    const = lambda b, j: (0, 0)
    return pl.pallas_call(
        _attn_prompt_kernel,
        grid=(nb, s // tq),
        in_specs=[
            pl.BlockSpec(memory_space=pltpu.SMEM),
            pl.BlockSpec((1, tq, N_HEADS * LANES), cur),
            pl.BlockSpec((1, tq, kd_cols), cur),
            pl.BlockSpec((1, blk, kd_cols), prv),
            pl.BlockSpec((1, tq, kd_cols), cur),
            pl.BlockSpec((1, blk, kd_cols), prv),
            pl.BlockSpec((blk, kd_cols), const),
            pl.BlockSpec((blk, kd_cols), const),
            pl.BlockSpec((1, ATTN_DIM), const),
        ],
        out_specs=pl.BlockSpec((1, tq, ATTN_DIM), cur),
        out_shape=jax.ShapeDtypeStruct((nb, s, ATTN_DIM), BF16),
        scratch_shapes=[pltpu.VMEM((tq, ATTN_DIM), F32)],
        compiler_params=pltpu.CompilerParams(
            dimension_semantics=("arbitrary", "arbitrary"), vmem_limit_bytes=VMEM_LIMIT),
        name="attn_prompt",
    )(sinks, q, kd, kd, vd, vd, kd_meta, vd_meta, attn_g)


def _attn_sample_kernel(sinks_ref, q_ref, kn_ref, vn_ref, knt_ref, vnt_ref, ck_ref, cv_ref, ag_ref,
                        ya_ref, nk_ref, nv_ref):
    nseq = q_ref.shape[0]
    base = pl.program_id(0) * nseq
    n_rows = N_HEADS * nseq
    qf = q_ref[...].astype(F32)
    kv_of_lane = lax.broadcasted_iota(jnp.int32, (nseq, KV_DIM), 1) // HEAD_DIM
    q_rows, sink_rows = [], []
    for r in range(N_HEADS):
        i, g = divmod(r, N_KV_HEADS)
        q_rows.append(jnp.where(kv_of_lane == g, qf[:, i * KV_DIM:(i + 1) * KV_DIM], 0.0))
        sink_rows.append(jnp.full((nseq, 1), sinks_ref[g * GROUP + i] * LOG2_E, F32))
    q_all = jnp.concatenate(q_rows, axis=0)
    sink = jnp.concatenate(sink_rows, axis=0)
    q_bf = q_all.astype(BF16)

    key_lane = lax.broadcasted_iota(jnp.int32, (KV_DIM, WINDOW), 1)

    def shifted(cache, new_t, n):
        col = jnp.sum(jnp.where(key_lane == base + n, new_t, 0.0), axis=1, keepdims=True)
        return jnp.where(key_lane == WINDOW - 1, col, pltpu.roll(cache, WINDOW - 1, axis=1))

    seq_of_row = lax.broadcasted_iota(jnp.int32, (n_rows, WINDOW), 0) % nseq
    s_c = jnp.zeros((n_rows, WINDOW), F32)
    for n in range(nseq):
        kt = ck_ref[n]
        s_n = jnp.dot(q_bf, kt.astype(BF16), preferred_element_type=F32)
        s_c = jnp.where(seq_of_row == n, s_n, s_c)
        nk_ref[n] = shifted(kt, knt_ref[...], n)
    kn = jnp.concatenate([kn_ref[...]] * N_HEADS, axis=0)
    vn = jnp.concatenate([vn_ref[...]] * N_HEADS, axis=0)
    s_new = jnp.sum(q_all * kn, axis=-1, keepdims=True)
    m = jnp.maximum(jnp.maximum(jnp.max(s_c, axis=-1, keepdims=True), s_new), sink)
    p_c = jnp.exp2(s_c - m)
    p_new = jnp.exp2(s_new - m)
    denom = jnp.sum(p_c, axis=-1, keepdims=True) + p_new + jnp.exp2(sink - m)
    p_bf = p_c.astype(BF16)

    seq_of_row = lax.broadcasted_iota(jnp.int32, (n_rows, KV_DIM), 0) % nseq
    o = jnp.zeros((n_rows, KV_DIM), F32)
    for n in range(nseq):
        vt = cv_ref[n]
        o_n = lax.dot_general(p_bf, vt.astype(BF16), (((1,), (1,)), ((), ())), preferred_element_type=F32)
        o = jnp.where(seq_of_row == n, o_n, o)
        nv_ref[n] = shifted(vt, vnt_ref[...], n)
    kv_of_row = (lax.broadcasted_iota(jnp.int32, (n_rows, KV_DIM), 0) // nseq) % N_KV_HEADS
    own = (lax.broadcasted_iota(jnp.int32, (n_rows, KV_DIM), 1) // HEAD_DIM) == kv_of_row
    o = jnp.where(own, (o + p_new * vn) / denom, 0.0)
    slabs = []
    for i in range(GROUP):
        lo = i * N_KV_HEADS * nseq
        slabs.append(sum(o[lo + g * nseq:lo + (g + 1) * nseq] for g in range(N_KV_HEADS)))
    ya_ref[...] = _rms(_group_major(jnp.concatenate(slabs, axis=-1)), ag_ref[...]).astype(BF16)


def _attn_sample(q, k_new, v_new, k_new_t, v_new_t, cache_kt, cache_vt, sinks, attn_g, *, nseq):
    n = q.shape[0]
    rows = lambda i: (i, 0)
    seqs = lambda i: (i, 0, 0)
    const = lambda i: (0, 0)
    return pl.pallas_call(
        _attn_sample_kernel,
        grid=(n // nseq,),
        in_specs=[
            pl.BlockSpec(memory_space=pltpu.SMEM),
            pl.BlockSpec((nseq, ATTN_DIM), rows),
            pl.BlockSpec((nseq, KV_DIM), rows),
            pl.BlockSpec((nseq, KV_DIM), rows),
            pl.BlockSpec((KV_DIM, n), const),
            pl.BlockSpec((KV_DIM, n), const),
            pl.BlockSpec((nseq, KV_DIM, WINDOW), seqs),
            pl.BlockSpec((nseq, KV_DIM, WINDOW), seqs),
            pl.BlockSpec((1, ATTN_DIM), const),
        ],
        out_specs=[
            pl.BlockSpec((nseq, ATTN_DIM), rows),
            pl.BlockSpec((nseq, KV_DIM, WINDOW), seqs),
            pl.BlockSpec((nseq, KV_DIM, WINDOW), seqs),
        ],
        out_shape=[
            jax.ShapeDtypeStruct((n, ATTN_DIM), BF16),
            jax.ShapeDtypeStruct((n, KV_DIM, WINDOW), F32),
            jax.ShapeDtypeStruct((n, KV_DIM, WINDOW), F32),
        ],
        compiler_params=pltpu.CompilerParams(
            dimension_semantics=("arbitrary",), vmem_limit_bytes=VMEM_LIMIT),
        name="attn_sample",
    )(sinks, q, k_new, v_new, k_new_t, v_new_t, cache_kt, cache_vt, attn_g)


def _mlp_kernel(chunks, x_ref, yc_ref, ya_ref, wo_ref, g2_ref, wg_hbm, wu_hbm, wd_hbm, gf_ref,
                y_ref, hn_scr, wg_buf, wu_buf, wd_buf, sem):
    i = pl.program_id(0)
    n_tiles = pl.num_programs(0)
    n_slots = wg_buf.shape[0]
    n_chunks = len(chunks)
    ahead = n_slots - 1

    def chunk_copies(c):
        slot = c % n_slots
        off, width = chunks[c]
        cols = pl.ds(off, width)
        return (pltpu.make_async_copy(wg_hbm.at[:, cols], wg_buf.at[slot, :, :width], sem.at[0, slot]),
                pltpu.make_async_copy(wu_hbm.at[:, cols], wu_buf.at[slot, :, :width], sem.at[1, slot]),
                pltpu.make_async_copy(wd_hbm.at[cols, :], wd_buf.at[slot, :width, :], sem.at[2, slot]))

    def start(c):
        for cp in chunk_copies(c):
            cp.start()

    def wait(c):
        for cp in chunk_copies(c):
            cp.wait()

    @pl.when(i == 0)
    def _():
        for c in range(min(ahead, n_chunks)):
            start(c)

    mix = (jnp.dot(yc_ref[...], wo_ref[:CONV_DIM, :], preferred_element_type=F32)
           + jnp.dot(ya_ref[...], wo_ref[CONV_DIM:, :], preferred_element_type=F32))
    h = x_ref[...] + mix
    y_ref[...] = h
    hn_scr[...] = _rms(h, g2_ref[...]).astype(BF16)

    for c in range(n_chunks):
        nxt = c + ahead
        if nxt < n_chunks:
            start(nxt)
        else:
            @pl.when(i + 1 < n_tiles)
            def _():
                start(nxt - n_chunks)
        wait(c)
        slot = c % n_slots
        width = chunks[c][1]
        hn = hn_scr[...]
        gate = jnp.dot(hn, wg_buf[slot, :, :width], preferred_element_type=F32)
        up = jnp.dot(hn, wu_buf[slot, :, :width], preferred_element_type=F32)
        act = (gate * jax.nn.sigmoid(gate) * up).astype(BF16)
        y_ref[...] += jnp.dot(act, wd_buf[slot, :width, :], preferred_element_type=F32)

    y_ref[...] = _rms(y_ref[...], gf_ref[...])


def _mlp_chunks(d_ff, tf):
    bounds = list(range(0, d_ff, tf)) + [d_ff]
    return tuple((lo, hi - lo) for lo, hi in zip(bounds[:-1], bounds[1:]))


def _mlp_call(x, yc, ya, w_out, g2, w_gate, w_up, w_down, gf, *, tm, tf, n_slots, name):
    r = x.shape[0]
    chunks = _mlp_chunks(w_gate.shape[1], tf)
    assert r == tm or len(chunks) % n_slots == 0
    assert len(chunks) >= n_slots and all(w % LANES == 0 for _, w in chunks)
    rows = lambda i: (i, 0)
    const = lambda i: (0, 0)
    return pl.pallas_call(
        functools.partial(_mlp_kernel, chunks),
        grid=(r // tm,),
        in_specs=[
            pl.BlockSpec((tm, D_MODEL), rows),
            pl.BlockSpec((tm, CONV_DIM), rows),
            pl.BlockSpec((tm, ATTN_DIM), rows),
            pl.BlockSpec((D_MODEL, D_MODEL), const, pipeline_mode=pl.Buffered(1)),
            pl.BlockSpec((1, D_MODEL), const),
            pl.BlockSpec(memory_space=pl.ANY),
            pl.BlockSpec(memory_space=pl.ANY),
            pl.BlockSpec(memory_space=pl.ANY),
            pl.BlockSpec((1, D_MODEL), const),
        ],
        out_specs=pl.BlockSpec((tm, D_MODEL), rows),
        out_shape=jax.ShapeDtypeStruct((r, D_MODEL), F32),
        scratch_shapes=[
            pltpu.VMEM((tm, D_MODEL), BF16),
            pltpu.VMEM((n_slots, D_MODEL, tf), BF16),
            pltpu.VMEM((n_slots, D_MODEL, tf), BF16),
            pltpu.VMEM((n_slots, tf, D_MODEL), BF16),
            pltpu.SemaphoreType.DMA((3, n_slots)),
        ],
        compiler_params=pltpu.CompilerParams(
            dimension_semantics=("arbitrary",), vmem_limit_bytes=VMEM_LIMIT),
        name=name,
    )(x, yc, ya, w_out, g2, w_gate, w_up, w_down, gf)


def _rope_tables(pos):
    inv = ROPE_THETA ** (-jnp.arange(HALF_HEAD, dtype=F32) / HALF_HEAD)
    ang = pos.astype(F32)[:, None] * inv[None, :]
    cos = jnp.cos(ang)
    sin = jnp.sin(ang)
    reps = LANES // HEAD_DIM
    return (jnp.tile(jnp.concatenate([cos, cos], axis=-1), (1, reps)),
            jnp.tile(jnp.concatenate([-sin, sin], axis=-1), (1, reps)))


def kernel(x_prompt, x_sample, cache_k, cache_v, state_conv, meta_tokens, norm1_g, w_in, conv_w, conv_norm_g,
           attn_norm_g, attn_sinks, w_out, norm2_g, w_gate, w_up, w_down, final_norm_g):
    depth = w_in.shape[0]
    assert depth == 1, "single-layer step only"
    nb, seq, _ = x_prompt.shape
    ns, dec_seq, _ = x_sample.shape
    assert dec_seq == 1

    g1 = norm1_g[0][None]
    g2 = norm2_g[0][None]
    gf = final_norm_g[None]
    cg = conv_norm_g[0][None]
    ag = attn_norm_g[0][None]
    cw = conv_w[0]
    sinks = attn_sinks[0]
    xs = x_sample.reshape(ns, D_MODEL)
    cos_m, sin_m = _rope_tables(jnp.arange(N_META))
    cos_s, sin_s = _rope_tables(jnp.full((ns,), PAST_LEN))
    prev_s = state_conv[0].reshape(ns, (CONV_W - 1) * CONV_DIM)
    w_in_b, kd_meta, vd_meta, u_m, yc_s, q_s, k_st, v_st, k_s, v_s, u_s = _small_proj_call(
        meta_tokens, xs, g1, w_in[0], cw, cg, cos_m, sin_m, cos_s, sin_s, prev_s)

    cos_p, sin_p = _rope_tables(N_META + jnp.arange(seq))
    yc, q, kd, vd, k_tail, v_tail, u_tail, w_out_b, w_gate_b, w_up_b, w_down_b = _proj_call(
        x_prompt, g1, w_in_b, cw, cg, cos_p, sin_p, u_m, tm=PROJ_ROWS, n_sub=PROJ_SUB_TILES,
        cast=(w_out[0], w_gate[0], w_up[0], w_down[0]))
    ya = _attn_prompt(q, kd, vd, kd_meta, vd_meta, sinks, ag, n_sub=ATTN_BLOCKS_PER_STEP)
    rp = nb * seq
    y_prompt = _mlp_call(x_prompt.reshape(rp, D_MODEL), yc.reshape(rp, CONV_DIM), ya.reshape(rp, ATTN_DIM),
                         w_out_b, g2, w_gate_b, w_up_b, w_down_b, gf,
                         tm=MLP_ROWS, tf=MLP_CHUNK, n_slots=MLP_SLOTS, name="mlp_prompt")
    y_prompt = y_prompt.reshape(nb, seq, D_MODEL)
    new_k_prompt = k_tail.reshape(1, nb, WINDOW, N_KV_HEADS, HEAD_DIM)
    new_v_prompt = v_tail.reshape(1, nb, WINDOW, N_KV_HEADS, HEAD_DIM)
    new_conv_prompt = u_tail[None]

    cache_kt = jnp.transpose(cache_k[0], (0, 2, 3, 1)).reshape(ns, KV_DIM, WINDOW)
    cache_vt = jnp.transpose(cache_v[0], (0, 2, 3, 1)).reshape(ns, KV_DIM, WINDOW)
    ya_s, nk_s, nv_s = _attn_sample(q_s, k_s, v_s, k_st, v_st, cache_kt, cache_vt, sinks, ag,
                                    nseq=SAMPLE_SEQS_PER_STEP)
    y_sample = _mlp_call(xs, yc_s, ya_s, w_out_b, g2, w_gate_b, w_up_b, w_down_b, gf,
                         tm=ns, tf=SAMPLE_MLP_CHUNK, n_slots=SAMPLE_MLP_SLOTS, name="mlp_sample")
    y_sample = y_sample.reshape(ns, 1, D_MODEL)
    new_k_sample = jnp.transpose(nk_s.reshape(ns, N_KV_HEADS, HEAD_DIM, WINDOW), (0, 3, 1, 2))[None]
    new_v_sample = jnp.transpose(nv_s.reshape(ns, N_KV_HEADS, HEAD_DIM, WINDOW), (0, 3, 1, 2))[None]
    new_conv_sample = u_s.reshape(1, ns, CONV_W - 1, CONV_DIM)

    return (y_prompt, y_sample, new_k_prompt, new_v_prompt, new_conv_prompt,
            new_k_sample, new_v_sample, new_conv_sample)
```

```python
import functools

import jax
import jax.numpy as jnp
from jax import lax
from jax.experimental import pallas as pl
from jax.experimental.pallas import tpu as pltpu

D_MODEL = 2048
N_META = 16
CONV_DIM = 1024
CONV_W = 3
HEAD_DIM = 64
HALF_HEAD = HEAD_DIM // 2
ATTN_DIM = 1024
N_HEADS = 16
N_KV_HEADS = 4
GROUP = N_HEADS // N_KV_HEADS
KV_DIM = N_KV_HEADS * HEAD_DIM
WINDOW = 128
ROPE_THETA = 10000.0
RMS_EPS = 1e-6
NEG_INF = -1e30
PAST_LEN = 16384
LOG2_E = 1.4426950408889634

COL_BG = 0
COL_CG = CONV_DIM
COL_XC = 2 * CONV_DIM
COL_Q = 3 * CONV_DIM
COL_K = COL_Q + ATTN_DIM
COL_V = COL_K + KV_DIM
IN_COLS = COL_V + KV_DIM

LANES = 128
BF16_ROWS_PER_TILE = 16
HEADS_PER_TILE = LANES // HEAD_DIM
VMEM_LIMIT = 60 * 1024 * 1024

PROJ_ROWS = 512
PROJ_SUB_TILES = 2
ATTN_BLOCKS_PER_STEP = 8
SAMPLE_SEQS_PER_STEP = 16
MLP_ROWS = 512
MLP_CHUNK = 1024
MLP_SLOTS = 2
SAMPLE_MLP_CHUNK = 512

F32 = jnp.float32
BF16 = jnp.bfloat16


def _rms(x, g):
    return x * lax.rsqrt(jnp.mean(x * x, axis=-1, keepdims=True) + RMS_EPS) * g


def _lane_tile(t, width):
    return jnp.concatenate([t] * (width // t.shape[-1]), axis=-1)


def _rope(t, cos, sin_signed):
    w = t.shape[-1]
    lane = lax.broadcasted_iota(jnp.int32, t.shape, 1)
    first_half = (lane & (HEAD_DIM - 1)) < HALF_HEAD
    swapped = jnp.where(first_half, pltpu.roll(t, w - HALF_HEAD, axis=1), pltpu.roll(t, HALF_HEAD, axis=1))
    return t * _lane_tile(cos, w) + swapped * _lane_tile(sin_signed, w)


def _dup_heads(pair_tile, half):
    lane = lax.broadcasted_iota(jnp.int32, pair_tile.shape, 1)
    swapped = pltpu.roll(pair_tile, HEAD_DIM, axis=1)
    if half == 0:
        return jnp.where(lane < HEAD_DIM, pair_tile, swapped)
    return jnp.where(lane < HEAD_DIM, swapped, pair_tile)


def _dup_all(t):
    tiles = []
    for g in range(N_KV_HEADS):
        pair, half = divmod(g, HEADS_PER_TILE)
        tiles.append(_dup_heads(t[:, pair * LANES:(pair + 1) * LANES], half))
    return jnp.concatenate(tiles, axis=-1).astype(BF16)


def _spread_heads(q):
    lane = lax.broadcasted_iota(jnp.int32, (q.shape[0], LANES), 1)
    tiles = []
    for h in range(N_HEADS):
        tile, half = divmod(h, HEADS_PER_TILE)
        qt = q[:, tile * LANES:(tile + 1) * LANES]
        keep = (lane < HEAD_DIM) if half == 0 else (lane >= HEAD_DIM)
        tiles.append(jnp.where(keep, qt, jnp.zeros_like(qt)))
    return jnp.concatenate(tiles, axis=-1)


def _head_block(t, h):
    return t[:, h * HEAD_DIM:(h + 1) * HEAD_DIM]


def _group_minor(t):
    return jnp.concatenate([_head_block(t, g * GROUP + i) for i in range(GROUP) for g in range(N_KV_HEADS)], axis=-1)


def _group_major(t):
    return jnp.concatenate([_head_block(t, i * N_KV_HEADS + g) for g in range(N_KV_HEADS) for i in range(GROUP)],
                           axis=-1)


def _conv_mix(u, p2, p1, cw):
    row = lax.broadcasted_iota(jnp.int32, u.shape, 0)
    u1 = jnp.where(row == 0, p1, pltpu.roll(u, 1, axis=0))
    u2 = jnp.where(row == 0, p2, jnp.where(row == 1, p1, pltpu.roll(u, 2, axis=0)))
    return cw[0:1, :] * u2 + cw[1:2, :] * u1 + cw[2:3, :] * u


Q_SCALE = HEAD_DIM ** -0.5 * LOG2_E

N_PROJ_IN = 8
N_PROJ_OUT = 7


def _proj_kernel(n_sub, *refs):
    x_ref, g1_ref, w_ref, cw_ref, cg_ref, cos_ref, sin_ref, prev_ref = refs[:N_PROJ_IN]
    n_cast = (len(refs) - 1 - N_PROJ_IN - N_PROJ_OUT) // 2
    cast_in = refs[N_PROJ_IN:N_PROJ_IN + n_cast]
    yc_ref, q_ref, kd_ref, vd_ref, kt_ref, vt_ref, u_ref = refs[N_PROJ_IN + n_cast:N_PROJ_IN + n_cast + N_PROJ_OUT]
    cast_out = refs[N_PROJ_IN + n_cast + N_PROJ_OUT:-1]
    carry_ref = refs[-1]

    ts = x_ref.shape[1] // n_sub
    cw = cw_ref[...]

    @pl.when(pl.program_id(1) == 0)
    def _():
        carry_ref[...] = prev_ref[...]

    p2 = carry_ref[0:1, :]
    p1 = carry_ref[1:2, :]

    for t in range(n_sub):
        rows = slice(t * ts, (t + 1) * ts)
        hn = _rms(x_ref[0, rows, :], g1_ref[...]).astype(BF16)

        def section(lo, width):
            return jnp.dot(hn, w_ref[:, lo:lo + width], preferred_element_type=F32)

        u = section(COL_CG, CONV_DIM) * section(COL_XC, CONV_DIM)
        cy = _conv_mix(u, p2, p1, cw)
        p2 = u[ts - 2:ts - 1, :]
        p1 = u[ts - 1:, :]
        yc_ref[0, rows, :] = _rms(section(COL_BG, CONV_DIM) * cy, cg_ref[...]).astype(BF16)

        cos = cos_ref[rows, :]
        sin = sin_ref[rows, :]
        q = _rope(section(COL_Q, ATTN_DIM), cos, sin) * Q_SCALE
        k = _rope(section(COL_K, KV_DIM), cos, sin)
        v = section(COL_V, KV_DIM)
        q_ref[0, rows, :] = _spread_heads(q.astype(BF16))
        kd_ref[0, rows, :] = _dup_all(k)
        vd_ref[0, rows, :] = _dup_all(v)

        if t == 0:
            for src, dst in zip(cast_in, cast_out):
                dst[...] = src[...].astype(BF16)

    tail = jnp.concatenate([p2, p1], axis=0)
    carry_ref[...] = tail
    u_ref[0] = tail
    tail_rows = kt_ref.shape[1]
    kt_ref[0] = k[ts - tail_rows:, :]
    vt_ref[0] = v[ts - tail_rows:, :]


def _proj_call(x, g1, w_in, conv_w, conv_g, cos, sin, prev, *, tm, n_sub, cast=()):
    nb, s, _ = x.shape
    n_steps = nb * (s // tm)
    slab = lambda b, i: (b * (s // tm) + i, 0)
    cast_specs = [pl.BlockSpec((w.shape[0] // n_steps, w.shape[1]), slab) for w in cast]
    assert all(w.shape[0] % (n_steps * BF16_ROWS_PER_TILE) == 0 for w in cast)
    assert tm // n_sub >= WINDOW
    kd_cols = N_KV_HEADS * LANES
    q_cols = N_HEADS * LANES
    const = lambda b, i: (0, 0)
    rows = lambda b, i: (b, i, 0)
    per_seq = lambda b, i: (b, 0, 0)
    return pl.pallas_call(
        functools.partial(_proj_kernel, n_sub),
        grid=(nb, s // tm),
        in_specs=[
            pl.BlockSpec((1, tm, D_MODEL), rows),
            pl.BlockSpec((1, D_MODEL), const),
            pl.BlockSpec((D_MODEL, IN_COLS), const, pipeline_mode=pl.Buffered(1)),
            pl.BlockSpec((CONV_W, CONV_DIM), const),
            pl.BlockSpec((1, CONV_DIM), const),
            pl.BlockSpec((tm, LANES), lambda b, i: (i, 0)),
            pl.BlockSpec((tm, LANES), lambda b, i: (i, 0)),
            pl.BlockSpec((CONV_W - 1, CONV_DIM), const),
            *cast_specs,
        ],
        out_specs=[
            pl.BlockSpec((1, tm, CONV_DIM), rows),
            pl.BlockSpec((1, tm, q_cols), rows),
            pl.BlockSpec((1, tm, kd_cols), rows),
            pl.BlockSpec((1, tm, kd_cols), rows),
            pl.BlockSpec((1, WINDOW, KV_DIM), per_seq),
            pl.BlockSpec((1, WINDOW, KV_DIM), per_seq),
            pl.BlockSpec((1, CONV_W - 1, CONV_DIM), per_seq),
            *cast_specs,
        ],
        out_shape=[
            jax.ShapeDtypeStruct((nb, s, CONV_DIM), BF16),
            jax.ShapeDtypeStruct((nb, s, q_cols), BF16),
            jax.ShapeDtypeStruct((nb, s, kd_cols), BF16),
            jax.ShapeDtypeStruct((nb, s, kd_cols), BF16),
            jax.ShapeDtypeStruct((nb, WINDOW, KV_DIM), F32),
            jax.ShapeDtypeStruct((nb, WINDOW, KV_DIM), F32),
            jax.ShapeDtypeStruct((nb, CONV_W - 1, CONV_DIM), F32),
            *[jax.ShapeDtypeStruct(w.shape, BF16) for w in cast],
        ],
        scratch_shapes=[pltpu.VMEM((CONV_W - 1, CONV_DIM), F32)],
        compiler_params=pltpu.CompilerParams(
            dimension_semantics=("arbitrary", "arbitrary"), vmem_limit_bytes=VMEM_LIMIT),
        name="proj_prompt",
    )(x, g1, w_in, conv_w, conv_g, cos, sin, prev, *cast)


SMALL_ROWS = 256


def _small_proj_kernel(xm_ref, xs_ref, g1_ref, w_ref, cw_ref, cg_ref, cosm_ref, sinm_ref, coss_ref, sins_ref, prev_ref,
                       wb_ref, kdm_ref, vdm_ref, um_ref, yc_ref, q_ref, kst_ref, vst_ref, ks_ref, vs_ref, us_ref,
                       hn_scr, z_scr):
    j = pl.program_id(0)
    n_meta = xm_ref.shape[0]
    n_all = z_scr.shape[0]

    @pl.when(j == 0)
    def _():
        hn = jnp.concatenate([_rms(xm_ref[...], g1_ref[...]), _rms(xs_ref[...], g1_ref[...])], axis=0).astype(BF16)
        for b in range(hn_scr.shape[0]):
            hn_scr[b] = hn[:, b * SMALL_ROWS:(b + 1) * SMALL_ROWS]
        z_scr[...] = jnp.zeros_like(z_scr)

    wb = w_ref[...].astype(BF16)
    wb_ref[...] = wb
    z_scr[...] += jnp.dot(hn_scr[j], wb, preferred_element_type=F32)

    @pl.when(j == pl.num_programs(0) - 1)
    def _():
        def section(rows, lo, width):
            return z_scr[rows, lo:lo + width]

        cw = cw_ref[...]
        rows = slice(0, n_meta)
        u = section(rows, COL_CG, CONV_DIM) * section(rows, COL_XC, CONV_DIM)
        um_ref[...] = u[n_meta - (CONV_W - 1):, :]
        k = _rope(section(rows, COL_K, KV_DIM), cosm_ref[...], sinm_ref[...])
        v = section(rows, COL_V, KV_DIM)
        pad = jnp.zeros((kdm_ref.shape[0] - n_meta, kdm_ref.shape[1]), BF16)
        kdm_ref[...] = jnp.concatenate([pad, _dup_all(k)], axis=0)
        vdm_ref[...] = jnp.concatenate([pad, _dup_all(v)], axis=0)

        rows = slice(n_meta, n_all)
        u = section(rows, COL_CG, CONV_DIM) * section(rows, COL_XC, CONV_DIM)
        u2 = prev_ref[:, :CONV_DIM]
        u1 = prev_ref[:, CONV_DIM:]
        us_ref[...] = jnp.concatenate([u1, u], axis=-1)
        cy = cw[0:1, :] * u2 + cw[1:2, :] * u1 + cw[2:3, :] * u
        yc_ref[...] = _rms(section(rows, COL_BG, CONV_DIM) * cy, cg_ref[...]).astype(BF16)
        cos = coss_ref[...]
        sin = sins_ref[...]
        q = _rope(section(rows, COL_Q, ATTN_DIM), cos, sin) * Q_SCALE
        k = _rope(section(rows, COL_K, KV_DIM), cos, sin)
        v = section(rows, COL_V, KV_DIM)
        q_ref[...] = _group_minor(q).astype(BF16)
        ks_ref[...] = k
        vs_ref[...] = v
        kst_ref[...] = k.T
        vst_ref[...] = v.T


def _small_proj_call(x_meta, x_sample, g1, w_in, conv_w, conv_g, cos_m, sin_m, cos_s, sin_s, prev_s):
    n_meta = x_meta.shape[0]
    ns = x_sample.shape[0]
    n_blocks = D_MODEL // SMALL_ROWS
    kd_cols = N_KV_HEADS * LANES
    const = lambda j: (0, 0)
    full = lambda shape: pl.BlockSpec(shape, const)
    w_spec = pl.BlockSpec((SMALL_ROWS, IN_COLS), lambda j: (j, 0))
    out_shapes = [
        ((D_MODEL, IN_COLS), BF16),
        ((WINDOW, kd_cols), BF16),
        ((WINDOW, kd_cols), BF16),
        ((CONV_W - 1, CONV_DIM), F32),
        ((ns, CONV_DIM), BF16),
        ((ns, ATTN_DIM), BF16),
        ((KV_DIM, ns), F32),
        ((KV_DIM, ns), F32),
        ((ns, KV_DIM), F32),
        ((ns, KV_DIM), F32),
        ((ns, (CONV_W - 1) * CONV_DIM), F32),
    ]
    return pl.pallas_call(
        _small_proj_kernel,
        grid=(n_blocks,),
        in_specs=[
            full((n_meta, D_MODEL)), full((ns, D_MODEL)), full((1, D_MODEL)), w_spec,
            full((CONV_W, CONV_DIM)), full((1, CONV_DIM)),
            full((n_meta, LANES)), full((n_meta, LANES)), full((ns, LANES)), full((ns, LANES)),
            full((ns, (CONV_W - 1) * CONV_DIM)),
        ],
        out_specs=[w_spec] + [full(shape) for shape, _ in out_shapes[1:]],
        out_shape=[jax.ShapeDtypeStruct(shape, dtype) for shape, dtype in out_shapes],
        scratch_shapes=[pltpu.VMEM((n_blocks, n_meta + ns, SMALL_ROWS), BF16),
                        pltpu.VMEM((n_meta + ns, IN_COLS), F32)],
        compiler_params=pltpu.CompilerParams(
            dimension_semantics=("arbitrary",), vmem_limit_bytes=VMEM_LIMIT),
        name="proj_small",
    )(x_meta, x_sample, g1, w_in, conv_w, conv_g, cos_m, sin_m, cos_s, sin_s, prev_s)


N_ATTN_IN = 9


def _mlp_rider(step, n_chunks, x_ref, yc_ref, ya_ref, wo_ref, g2_ref, wg_ref, wu_ref, wd_ref, gf_ref, y_ref, hn_scr):

    @pl.when(step == 0)
    def _():
        mix = (jnp.dot(yc_ref[...], wo_ref[:CONV_DIM, :], preferred_element_type=F32)
               + jnp.dot(ya_ref[...], wo_ref[CONV_DIM:, :], preferred_element_type=F32))
        h = x_ref[...] + mix
        y_ref[...] = h
        hn_scr[...] = _rms(h, g2_ref[...]).astype(BF16)

    @pl.when(step < n_chunks)
    def _():
        hn = hn_scr[...]
        gate = jnp.dot(hn, wg_ref[...], preferred_element_type=F32)
        up = jnp.dot(hn, wu_ref[...], preferred_element_type=F32)
        act = (gate * jax.nn.sigmoid(gate) * up).astype(BF16)
        y_ref[...] += jnp.dot(act, wd_ref[...], preferred_element_type=F32)

    @pl.when(step == n_chunks - 1)
    def _():
        y_ref[...] = _rms(y_ref[...], gf_ref[...])


def _attn_prompt_kernel(n_rider_chunks, *refs):
    sinks_ref, q_ref, kc_ref, kp_ref, vc_ref, vp_ref, km_ref, vm_ref, ag_ref = refs[:N_ATTN_IN]
    if n_rider_chunks:
        rider_in = refs[N_ATTN_IN:-4]
        ya_ref, yr_ref, o_scr, hnr_scr = refs[-4:]
        step = pl.program_id(0) * pl.num_programs(1) + pl.program_id(1)
        _mlp_rider(step, n_rider_chunks, *rider_in, yr_ref, hnr_scr)
    else:
        ya_ref, o_scr = refs[N_ATTN_IN:]
    first = pl.program_id(1) == 0
    blk = WINDOW
    nk = 2 * blk
    n_sub = q_ref.shape[1] // blk

    qi = lax.broadcasted_iota(jnp.int32, (blk, nk), 0)
    kj = lax.broadcasted_iota(jnp.int32, (blk, nk), 1)
    band = (kj >= qi) & (kj <= qi + WINDOW)
    band_first = band & (kj >= jnp.where(first, blk - N_META, 0))

    lane = lax.broadcasted_iota(jnp.int32, (blk, LANES), 1)
    ones = jnp.ones((nk, LANES), BF16)
    for t in range(n_sub):
        rows = slice(t * blk, (t + 1) * blk)
        valid = band_first if t == 0 else band
        for g in range(N_KV_HEADS):
            cols = slice(g * LANES, (g + 1) * LANES)
            if t == 0:
                k_prev = jnp.where(first, km_ref[:, cols], kp_ref[0, :, cols])
                v_prev = jnp.where(first, vm_ref[:, cols], vp_ref[0, :, cols])
            else:
                k_prev = kc_ref[0, (t - 1) * blk:t * blk, cols]
                v_prev = vc_ref[0, (t - 1) * blk:t * blk, cols]
            kd = jnp.concatenate([k_prev, kc_ref[0, rows, cols]], axis=0)
            vd = jnp.concatenate([v_prev, vc_ref[0, rows, cols]], axis=0)
            v_rhs = jnp.concatenate([vd, ones], axis=1)
            for tile in range(g * GROUP // HEADS_PER_TILE, (g + 1) * GROUP // HEADS_PER_TILE):
                ov, m = [], []
                for h in (HEADS_PER_TILE * tile, HEADS_PER_TILE * tile + 1):
                    qm = q_ref[0, rows, h * LANES:(h + 1) * LANES]
                    s = lax.dot_general(qm, kd, (((1,), (1,)), ((), ())), preferred_element_type=F32)
                    s = jnp.where(valid, s, NEG_INF)
                    m.append(jnp.max(s, axis=-1, keepdims=True))
                    p = jnp.exp2(s - m[-1]).astype(BF16)
                    ov.append(jnp.dot(p, v_rhs, preferred_element_type=F32))
                low = lane < HEAD_DIM
                sink = jnp.where(low[:1], sinks_ref[HEADS_PER_TILE * tile], sinks_ref[HEADS_PER_TILE * tile + 1])
                denom = (jnp.where(low, ov[0][:, LANES:], ov[1][:, LANES:])
                         + jnp.exp2(sink * LOG2_E - jnp.where(low, m[0], m[1])))
                o_scr[rows, tile * LANES:(tile + 1) * LANES] = (
                    jnp.where(low, ov[0][:, :LANES], ov[1][:, :LANES]) / denom)
        ya_ref[0, rows, :] = _rms(o_scr[rows, :], ag_ref[...]).astype(BF16)


def _attn_prompt(q, kd, vd, kd_meta, vd_meta, sinks, attn_g, *, n_sub, rider=None, rider_chunk=None):
    nb, s, kd_cols = kd.shape
    blk = WINDOW
    tq = n_sub * blk
    n_inner = s // tq
    cur = lambda b, j: (b, j, 0)
    prv = lambda b, j: (b, jnp.maximum(j * n_sub - 1, 0), 0)
    const = lambda b, j: (0, 0)
    once = lambda shape: pl.BlockSpec(shape, const, pipeline_mode=pl.Buffered(1))
    in_specs = [
        pl.BlockSpec(memory_space=pltpu.SMEM),
        pl.BlockSpec((1, tq, N_HEADS * LANES), cur),
        pl.BlockSpec((1, tq, kd_cols), cur),
        pl.BlockSpec((1, blk, kd_cols), prv),
        pl.BlockSpec((1, tq, kd_cols), cur),
        pl.BlockSpec((1, blk, kd_cols), prv),
        pl.BlockSpec((blk, kd_cols), const),
        pl.BlockSpec((blk, kd_cols), const),
        pl.BlockSpec((1, ATTN_DIM), const),
    ]
    out_specs = [pl.BlockSpec((1, tq, ATTN_DIM), cur)]
    out_shape = [jax.ShapeDtypeStruct((nb, s, ATTN_DIM), BF16)]
    scratch = [pltpu.VMEM((tq, ATTN_DIM), F32)]
    rider = tuple(rider) if rider is not None else ()
    n_rider_chunks = 0
    if rider:
        x_r, yc_r, ya_r, w_out, g2, w_gate, w_up, w_down, gf = rider
        nr = x_r.shape[0]
        d_ff = w_gate.shape[1]
        assert d_ff % rider_chunk == 0 and d_ff // rider_chunk <= nb * n_inner
        n_rider_chunks = d_ff // rider_chunk
        last_chunk = n_rider_chunks - 1
        chunk = lambda b, j: jnp.minimum(b * n_inner + j, last_chunk)
        in_specs += [
            once((nr, D_MODEL)), once((nr, CONV_DIM)), once((nr, ATTN_DIM)), once((D_MODEL, D_MODEL)),
            pl.BlockSpec((1, D_MODEL), const),
            pl.BlockSpec((D_MODEL, rider_chunk), lambda b, j: (0, chunk(b, j))),
            pl.BlockSpec((D_MODEL, rider_chunk), lambda b, j: (0, chunk(b, j))),
            pl.BlockSpec((rider_chunk, D_MODEL), lambda b, j: (chunk(b, j), 0)),
            pl.BlockSpec((1, D_MODEL), const),
        ]
        out_specs.append(pl.BlockSpec((nr, D_MODEL), const))
        out_shape.append(jax.ShapeDtypeStruct((nr, D_MODEL), F32))
        scratch.append(pltpu.VMEM((nr, D_MODEL), BF16))
    return pl.pallas_call(
        functools.partial(_attn_prompt_kernel, n_rider_chunks),
        grid=(nb, n_inner),
        in_specs=in_specs,
        out_specs=out_specs,
        out_shape=out_shape,
        scratch_shapes=scratch,
        compiler_params=pltpu.CompilerParams(
            dimension_semantics=("arbitrary", "arbitrary"), vmem_limit_bytes=VMEM_LIMIT),
        name="attn_prompt",
    )(sinks, q, kd, kd, vd, vd, kd_meta, vd_meta, attn_g, *rider)


def _attn_sample_kernel(sinks_ref, q_ref, kn_ref, vn_ref, knt_ref, vnt_ref, ck_ref, cv_ref, ag_ref,
                        ya_ref, nk_ref, nv_ref):
    nseq = q_ref.shape[0]
    base = pl.program_id(0) * nseq
    n_rows = N_HEADS * nseq
    qf = q_ref[...].astype(F32)
    kv_of_lane = lax.broadcasted_iota(jnp.int32, (nseq, KV_DIM), 1) // HEAD_DIM
    q_rows, sink_rows = [], []
    for r in range(N_HEADS):
        i, g = divmod(r, N_KV_HEADS)
        q_rows.append(jnp.where(kv_of_lane == g, qf[:, i * KV_DIM:(i + 1) * KV_DIM], 0.0))
        sink_rows.append(jnp.full((nseq, 1), sinks_ref[g * GROUP + i] * LOG2_E, F32))
    q_all = jnp.concatenate(q_rows, axis=0)
    sink = jnp.concatenate(sink_rows, axis=0)
    q_bf = q_all.astype(BF16)

    key_lane = lax.broadcasted_iota(jnp.int32, (KV_DIM, WINDOW), 1)

    def shifted(cache, new_t, n):
        col = jnp.sum(jnp.where(key_lane == base + n, new_t, 0.0), axis=1, keepdims=True)
        return jnp.where(key_lane == WINDOW - 1, col, pltpu.roll(cache, WINDOW - 1, axis=1))

    seq_of_row = lax.broadcasted_iota(jnp.int32, (n_rows, WINDOW), 0) % nseq
    s_c = jnp.zeros((n_rows, WINDOW), F32)
    for n in range(nseq):
        kt = ck_ref[n]
        s_n = jnp.dot(q_bf, kt.astype(BF16), preferred_element_type=F32)
        s_c = jnp.where(seq_of_row == n, s_n, s_c)
        nk_ref[n] = shifted(kt, knt_ref[...], n)
    kn = jnp.concatenate([kn_ref[...]] * N_HEADS, axis=0)
    vn = jnp.concatenate([vn_ref[...]] * N_HEADS, axis=0)
    s_new = jnp.sum(q_all * kn, axis=-1, keepdims=True)
    m = jnp.maximum(jnp.maximum(jnp.max(s_c, axis=-1, keepdims=True), s_new), sink)
    p_c = jnp.exp2(s_c - m)
    p_new = jnp.exp2(s_new - m)
    denom = jnp.sum(p_c, axis=-1, keepdims=True) + p_new + jnp.exp2(sink - m)
    p_bf = p_c.astype(BF16)

    seq_of_row = lax.broadcasted_iota(jnp.int32, (n_rows, KV_DIM), 0) % nseq
    o = jnp.zeros((n_rows, KV_DIM), F32)
    for n in range(nseq):
        vt = cv_ref[n]
        o_n = lax.dot_general(p_bf, vt.astype(BF16), (((1,), (1,)), ((), ())), preferred_element_type=F32)
        o = jnp.where(seq_of_row == n, o_n, o)
        nv_ref[n] = shifted(vt, vnt_ref[...], n)
    kv_of_row = (lax.broadcasted_iota(jnp.int32, (n_rows, KV_DIM), 0) // nseq) % N_KV_HEADS
    own = (lax.broadcasted_iota(jnp.int32, (n_rows, KV_DIM), 1) // HEAD_DIM) == kv_of_row
    o = jnp.where(own, (o + p_new * vn) / denom, 0.0)
    slabs = []
    for i in range(GROUP):
        lo = i * N_KV_HEADS * nseq
        slabs.append(sum(o[lo + g * nseq:lo + (g + 1) * nseq] for g in range(N_KV_HEADS)))
    ya_ref[...] = _rms(_group_major(jnp.concatenate(slabs, axis=-1)), ag_ref[...]).astype(BF16)


def _attn_sample(q, k_new, v_new, k_new_t, v_new_t, cache_kt, cache_vt, sinks, attn_g, *, nseq):
    n = q.shape[0]
    rows = lambda i: (i, 0)
    seqs = lambda i: (i, 0, 0)
    const = lambda i: (0, 0)
    return pl.pallas_call(
        _attn_sample_kernel,
        grid=(n // nseq,),
        in_specs=[
            pl.BlockSpec(memory_space=pltpu.SMEM),
            pl.BlockSpec((nseq, ATTN_DIM), rows),
            pl.BlockSpec((nseq, KV_DIM), rows),
            pl.BlockSpec((nseq, KV_DIM), rows),
            pl.BlockSpec((KV_DIM, n), const),
            pl.BlockSpec((KV_DIM, n), const),
            pl.BlockSpec((nseq, KV_DIM, WINDOW), seqs),
            pl.BlockSpec((nseq, KV_DIM, WINDOW), seqs),
            pl.BlockSpec((1, ATTN_DIM), const),
        ],
        out_specs=[
            pl.BlockSpec((nseq, ATTN_DIM), rows),
            pl.BlockSpec((nseq, KV_DIM, WINDOW), seqs),
            pl.BlockSpec((nseq, KV_DIM, WINDOW), seqs),
        ],
        out_shape=[
            jax.ShapeDtypeStruct((n, ATTN_DIM), BF16),
            jax.ShapeDtypeStruct((n, KV_DIM, WINDOW), F32),
            jax.ShapeDtypeStruct((n, KV_DIM, WINDOW), F32),
        ],
        compiler_params=pltpu.CompilerParams(
            dimension_semantics=("arbitrary",), vmem_limit_bytes=VMEM_LIMIT),
        name="attn_sample",
    )(sinks, q, k_new, v_new, k_new_t, v_new_t, cache_kt, cache_vt, attn_g)


def _mlp_kernel(chunks, x_ref, yc_ref, ya_ref, wo_ref, g2_ref, wg_hbm, wu_hbm, wd_hbm, gf_ref,
                y_ref, hn_scr, wg_buf, wu_buf, wd_buf, sem):
    i = pl.program_id(0)
    n_tiles = pl.num_programs(0)
    n_slots = wg_buf.shape[0]
    n_chunks = len(chunks)
    ahead = n_slots - 1

    def chunk_copies(c):
        slot = c % n_slots
        off, width = chunks[c]
        cols = pl.ds(off, width)
        return (pltpu.make_async_copy(wg_hbm.at[:, cols], wg_buf.at[slot, :, :width], sem.at[0, slot]),
                pltpu.make_async_copy(wu_hbm.at[:, cols], wu_buf.at[slot, :, :width], sem.at[1, slot]),
                pltpu.make_async_copy(wd_hbm.at[cols, :], wd_buf.at[slot, :width, :], sem.at[2, slot]))

    def start(c):
        for cp in chunk_copies(c):
            cp.start()

    def wait(c):
        for cp in chunk_copies(c):
            cp.wait()

    @pl.when(i == 0)
    def _():
        for c in range(min(ahead, n_chunks)):
            start(c)

    mix = (jnp.dot(yc_ref[...], wo_ref[:CONV_DIM, :], preferred_element_type=F32)
           + jnp.dot(ya_ref[...], wo_ref[CONV_DIM:, :], preferred_element_type=F32))
    h = x_ref[...] + mix
    y_ref[...] = h
    hn_scr[...] = _rms(h, g2_ref[...]).astype(BF16)

    for c in range(n_chunks):
        nxt = c + ahead
        if nxt < n_chunks:
            start(nxt)
        else:
            @pl.when(i + 1 < n_tiles)
            def _():
                start(nxt - n_chunks)
        wait(c)
        slot = c % n_slots
        width = chunks[c][1]
        hn = hn_scr[...]
        gate = jnp.dot(hn, wg_buf[slot, :, :width], preferred_element_type=F32)
        up = jnp.dot(hn, wu_buf[slot, :, :width], preferred_element_type=F32)
        act = (gate * jax.nn.sigmoid(gate) * up).astype(BF16)
        y_ref[...] += jnp.dot(act, wd_buf[slot, :width, :], preferred_element_type=F32)

    y_ref[...] = _rms(y_ref[...], gf_ref[...])


def _mlp_chunks(d_ff, tf):
    bounds = list(range(0, d_ff, tf)) + [d_ff]
    return tuple((lo, hi - lo) for lo, hi in zip(bounds[:-1], bounds[1:]))


def _mlp_call(x, yc, ya, w_out, g2, w_gate, w_up, w_down, gf, *, tm, tf, n_slots, name):
    r = x.shape[0]
    chunks = _mlp_chunks(w_gate.shape[1], tf)
    assert r == tm or len(chunks) % n_slots == 0
    assert len(chunks) >= n_slots and all(w % LANES == 0 for _, w in chunks)
    rows = lambda i: (i, 0)
    const = lambda i: (0, 0)
    return pl.pallas_call(
        functools.partial(_mlp_kernel, chunks),
        grid=(r // tm,),
        in_specs=[
            pl.BlockSpec((tm, D_MODEL), rows),
            pl.BlockSpec((tm, CONV_DIM), rows),
            pl.BlockSpec((tm, ATTN_DIM), rows),
            pl.BlockSpec((D_MODEL, D_MODEL), const, pipeline_mode=pl.Buffered(1)),
            pl.BlockSpec((1, D_MODEL), const),
            pl.BlockSpec(memory_space=pl.ANY),
            pl.BlockSpec(memory_space=pl.ANY),
            pl.BlockSpec(memory_space=pl.ANY),
            pl.BlockSpec((1, D_MODEL), const),
        ],
        out_specs=pl.BlockSpec((tm, D_MODEL), rows),
        out_shape=jax.ShapeDtypeStruct((r, D_MODEL), F32),
        scratch_shapes=[
            pltpu.VMEM((tm, D_MODEL), BF16),
            pltpu.VMEM((n_slots, D_MODEL, tf), BF16),
            pltpu.VMEM((n_slots, D_MODEL, tf), BF16),
            pltpu.VMEM((n_slots, tf, D_MODEL), BF16),
            pltpu.SemaphoreType.DMA((3, n_slots)),
        ],
        compiler_params=pltpu.CompilerParams(
            dimension_semantics=("arbitrary",), vmem_limit_bytes=VMEM_LIMIT),
        name=name,
    )(x, yc, ya, w_out, g2, w_gate, w_up, w_down, gf)


def _rope_tables(pos):
    inv = ROPE_THETA ** (-jnp.arange(HALF_HEAD, dtype=F32) / HALF_HEAD)
    ang = pos.astype(F32)[:, None] * inv[None, :]
    cos = jnp.cos(ang)
    sin = jnp.sin(ang)
    reps = LANES // HEAD_DIM
    return (jnp.tile(jnp.concatenate([cos, cos], axis=-1), (1, reps)),
            jnp.tile(jnp.concatenate([-sin, sin], axis=-1), (1, reps)))


def kernel(x_prompt, x_sample, cache_k, cache_v, state_conv, meta_tokens, norm1_g, w_in, conv_w, conv_norm_g,
           attn_norm_g, attn_sinks, w_out, norm2_g, w_gate, w_up, w_down, final_norm_g):
    depth = w_in.shape[0]
    assert depth == 1, "single-layer step only"
    nb, seq, _ = x_prompt.shape
    ns, dec_seq, _ = x_sample.shape
    assert dec_seq == 1

    g1 = norm1_g[0][None]
    g2 = norm2_g[0][None]
    gf = final_norm_g[None]
    cg = conv_norm_g[0][None]
    ag = attn_norm_g[0][None]
    cw = conv_w[0]
    sinks = attn_sinks[0]
    xs = x_sample.reshape(ns, D_MODEL)
    cos_m, sin_m = _rope_tables(jnp.arange(N_META))
    cos_s, sin_s = _rope_tables(jnp.full((ns,), PAST_LEN))
    prev_s = state_conv[0].reshape(ns, (CONV_W - 1) * CONV_DIM)
    w_in_b, kd_meta, vd_meta, u_m, yc_s, q_s, k_st, v_st, k_s, v_s, u_s = _small_proj_call(
        meta_tokens, xs, g1, w_in[0], cw, cg, cos_m, sin_m, cos_s, sin_s, prev_s)

    cos_p, sin_p = _rope_tables(N_META + jnp.arange(seq))
    yc, q, kd, vd, k_tail, v_tail, u_tail, w_out_b, w_gate_b, w_up_b, w_down_b = _proj_call(
        x_prompt, g1, w_in_b, cw, cg, cos_p, sin_p, u_m, tm=PROJ_ROWS, n_sub=PROJ_SUB_TILES,
        cast=(w_out[0], w_gate[0], w_up[0], w_down[0]))
    new_k_prompt = k_tail.reshape(1, nb, WINDOW, N_KV_HEADS, HEAD_DIM)
    new_v_prompt = v_tail.reshape(1, nb, WINDOW, N_KV_HEADS, HEAD_DIM)
    new_conv_prompt = u_tail[None]

    cache_kt = jnp.transpose(cache_k[0], (0, 2, 3, 1)).reshape(ns, KV_DIM, WINDOW)
    cache_vt = jnp.transpose(cache_v[0], (0, 2, 3, 1)).reshape(ns, KV_DIM, WINDOW)
    ya_s, nk_s, nv_s = _attn_sample(q_s, k_s, v_s, k_st, v_st, cache_kt, cache_vt, sinks, ag,
                                    nseq=SAMPLE_SEQS_PER_STEP)
    new_k_sample = jnp.transpose(nk_s.reshape(ns, N_KV_HEADS, HEAD_DIM, WINDOW), (0, 3, 1, 2))[None]
    new_v_sample = jnp.transpose(nv_s.reshape(ns, N_KV_HEADS, HEAD_DIM, WINDOW), (0, 3, 1, 2))[None]
    new_conv_sample = u_s.reshape(1, ns, CONV_W - 1, CONV_DIM)

    ya, y_sample = _attn_prompt(q, kd, vd, kd_meta, vd_meta, sinks, ag, n_sub=ATTN_BLOCKS_PER_STEP,
                                rider=(xs, yc_s, ya_s, w_out_b, g2, w_gate_b, w_up_b, w_down_b, gf),
                                rider_chunk=SAMPLE_MLP_CHUNK)
    y_sample = y_sample.reshape(ns, 1, D_MODEL)
    rp = nb * seq
    y_prompt = _mlp_call(x_prompt.reshape(rp, D_MODEL), yc.reshape(rp, CONV_DIM), ya.reshape(rp, ATTN_DIM),
                         w_out_b, g2, w_gate_b, w_up_b, w_down_b, gf,
                         tm=MLP_ROWS, tf=MLP_CHUNK, n_slots=MLP_SLOTS, name="mlp_prompt")
    y_prompt = y_prompt.reshape(nb, seq, D_MODEL)

    return (y_prompt, y_sample, new_k_prompt, new_v_prompt, new_conv_prompt,
            new_k_sample, new_v_sample, new_conv_sample)
```

```python
import functools

import jax
import jax.numpy as jnp
from jax import lax
from jax.experimental import pallas as pl
from jax.experimental.pallas import tpu as pltpu

D_MODEL = 2048
N_META = 16
CONV_DIM = 1024
CONV_W = 3
HEAD_DIM = 64
HALF_HEAD = HEAD_DIM // 2
ATTN_DIM = 1024
N_HEADS = 16
N_KV_HEADS = 4
GROUP = N_HEADS // N_KV_HEADS
KV_DIM = N_KV_HEADS * HEAD_DIM
WINDOW = 128
ROPE_THETA = 10000.0
RMS_EPS = 1e-6
NEG_INF = -1e30
PAST_LEN = 16384
LOG2_E = 1.4426950408889634

COL_BG = 0
COL_CG = CONV_DIM
COL_XC = 2 * CONV_DIM
COL_Q = 3 * CONV_DIM
COL_K = COL_Q + ATTN_DIM
COL_V = COL_K + KV_DIM
IN_COLS = COL_V + KV_DIM

LANES = 128
BF16_ROWS_PER_TILE = 16
HEADS_PER_TILE = LANES // HEAD_DIM
VMEM_LIMIT = 63 * 1024 * 1024

PROJ_ROWS = 512
PROJ_SUB_TILES = 2
ATTN_BLOCKS_PER_STEP = 8
SAMPLE_SEQS_PER_STEP = 16
MLP_ROWS = 512
MLP_CHUNK = 1024
MLP_SLOTS = 2
MLP_PIECE = 512
SAMPLE_MLP_CHUNK = 512
SAMPLE_MLP_SLOTS = 4

F32 = jnp.float32
BF16 = jnp.bfloat16


def _rms(x, g):
    return x * lax.rsqrt(jnp.mean(x * x, axis=-1, keepdims=True) + RMS_EPS) * g


def _lane_tile(t, width):
    return jnp.concatenate([t] * (width // t.shape[-1]), axis=-1)


def _rope(t, cos, sin_signed):
    w = t.shape[-1]
    lane = lax.broadcasted_iota(jnp.int32, t.shape, 1)
    first_half = (lane & (HEAD_DIM - 1)) < HALF_HEAD
    swapped = jnp.where(first_half, pltpu.roll(t, w - HALF_HEAD, axis=1), pltpu.roll(t, HALF_HEAD, axis=1))
    return t * _lane_tile(cos, w) + swapped * _lane_tile(sin_signed, w)


def _dup_heads(pair_tile, half):
    lane = lax.broadcasted_iota(jnp.int32, pair_tile.shape, 1)
    swapped = pltpu.roll(pair_tile, HEAD_DIM, axis=1)
    if half == 0:
        return jnp.where(lane < HEAD_DIM, pair_tile, swapped)
    return jnp.where(lane < HEAD_DIM, swapped, pair_tile)


def _dup_all(t):
    tiles = []
    for g in range(N_KV_HEADS):
        pair, half = divmod(g, HEADS_PER_TILE)
        tiles.append(_dup_heads(t[:, pair * LANES:(pair + 1) * LANES], half))
    return jnp.concatenate(tiles, axis=-1).astype(BF16)


def _spread_heads(q):
    lane = lax.broadcasted_iota(jnp.int32, (q.shape[0], LANES), 1)
    tiles = []
    for h in range(N_HEADS):
        tile, half = divmod(h, HEADS_PER_TILE)
        qt = q[:, tile * LANES:(tile + 1) * LANES]
        keep = (lane < HEAD_DIM) if half == 0 else (lane >= HEAD_DIM)
        tiles.append(jnp.where(keep, qt, jnp.zeros_like(qt)))
    return jnp.concatenate(tiles, axis=-1)


def _head_block(t, h):
    return t[:, h * HEAD_DIM:(h + 1) * HEAD_DIM]


def _group_minor(t):
    return jnp.concatenate([_head_block(t, g * GROUP + i) for i in range(GROUP) for g in range(N_KV_HEADS)], axis=-1)


def _group_major(t):
    return jnp.concatenate([_head_block(t, i * N_KV_HEADS + g) for g in range(N_KV_HEADS) for i in range(GROUP)],
                           axis=-1)


def _conv_mix(u, p2, p1, cw):
    row = lax.broadcasted_iota(jnp.int32, u.shape, 0)
    u1 = jnp.where(row == 0, p1, pltpu.roll(u, 1, axis=0))
    u2 = jnp.where(row == 0, p2, jnp.where(row == 1, p1, pltpu.roll(u, 2, axis=0)))
    return cw[0:1, :] * u2 + cw[1:2, :] * u1 + cw[2:3, :] * u


Q_SCALE = HEAD_DIM ** -0.5 * LOG2_E

N_PROJ_IN = 8
N_PROJ_OUT = 7


def _proj_kernel(n_sub, *refs):
    x_ref, g1_ref, w_ref, cw_ref, cg_ref, cos_ref, sin_ref, prev_ref = refs[:N_PROJ_IN]
    n_cast = (len(refs) - 1 - N_PROJ_IN - N_PROJ_OUT) // 2
    cast_in = refs[N_PROJ_IN:N_PROJ_IN + n_cast]
    yc_ref, q_ref, kd_ref, vd_ref, kt_ref, vt_ref, u_ref = refs[N_PROJ_IN + n_cast:N_PROJ_IN + n_cast + N_PROJ_OUT]
    cast_out = refs[N_PROJ_IN + n_cast + N_PROJ_OUT:-1]
    carry_ref = refs[-1]

    ts = x_ref.shape[1] // n_sub
    cw = cw_ref[...]

    @pl.when(pl.program_id(1) == 0)
    def _():
        carry_ref[...] = prev_ref[...]

    p2 = carry_ref[0:1, :]
    p1 = carry_ref[1:2, :]

    for t in range(n_sub):
        rows = slice(t * ts, (t + 1) * ts)
        hn = _rms(x_ref[0, rows, :], g1_ref[...]).astype(BF16)

        def section(lo, width):
            return jnp.dot(hn, w_ref[:, lo:lo + width], preferred_element_type=F32)

        u = section(COL_CG, CONV_DIM) * section(COL_XC, CONV_DIM)
        cy = _conv_mix(u, p2, p1, cw)
        p2 = u[ts - 2:ts - 1, :]
        p1 = u[ts - 1:, :]
        yc_ref[0, rows, :] = _rms(section(COL_BG, CONV_DIM) * cy, cg_ref[...]).astype(BF16)

        cos = cos_ref[rows, :]
        sin = sin_ref[rows, :]
        q = _rope(section(COL_Q, ATTN_DIM), cos, sin) * Q_SCALE
        k = _rope(section(COL_K, KV_DIM), cos, sin)
        v = section(COL_V, KV_DIM)
        q_ref[0, rows, :] = _spread_heads(q.astype(BF16))
        kd_ref[0, rows, :] = _dup_all(k)
        vd_ref[0, rows, :] = _dup_all(v)

        if t == 0:
            for src, dst in zip(cast_in, cast_out):
                dst[...] = src[...].astype(BF16)

    tail = jnp.concatenate([p2, p1], axis=0)
    carry_ref[...] = tail
    u_ref[0] = tail
    tail_rows = kt_ref.shape[1]
    kt_ref[0] = k[ts - tail_rows:, :]
    vt_ref[0] = v[ts - tail_rows:, :]


def _proj_call(x, g1, w_in, conv_w, conv_g, cos, sin, prev, *, tm, n_sub, cast=()):
    nb, s, _ = x.shape
    n_steps = nb * (s // tm)
    slab = lambda b, i: (b * (s // tm) + i, 0)
    cast_specs = [pl.BlockSpec((w.shape[0] // n_steps, w.shape[1]), slab) for w in cast]
    assert all(w.shape[0] % (n_steps * BF16_ROWS_PER_TILE) == 0 for w in cast)
    assert tm // n_sub >= WINDOW
    kd_cols = N_KV_HEADS * LANES
    q_cols = N_HEADS * LANES
    const = lambda b, i: (0, 0)
    rows = lambda b, i: (b, i, 0)
    per_seq = lambda b, i: (b, 0, 0)
    return pl.pallas_call(
        functools.partial(_proj_kernel, n_sub),
        grid=(nb, s // tm),
        in_specs=[
            pl.BlockSpec((1, tm, D_MODEL), rows),
            pl.BlockSpec((1, D_MODEL), const),
            pl.BlockSpec((D_MODEL, IN_COLS), const, pipeline_mode=pl.Buffered(1)),
            pl.BlockSpec((CONV_W, CONV_DIM), const),
            pl.BlockSpec((1, CONV_DIM), const),
            pl.BlockSpec((tm, LANES), lambda b, i: (i, 0)),
            pl.BlockSpec((tm, LANES), lambda b, i: (i, 0)),
            pl.BlockSpec((CONV_W - 1, CONV_DIM), const),
            *cast_specs,
        ],
        out_specs=[
            pl.BlockSpec((1, tm, CONV_DIM), rows),
            pl.BlockSpec((1, tm, q_cols), rows),
            pl.BlockSpec((1, tm, kd_cols), rows),
            pl.BlockSpec((1, tm, kd_cols), rows),
            pl.BlockSpec((1, WINDOW, KV_DIM), per_seq),
            pl.BlockSpec((1, WINDOW, KV_DIM), per_seq),
            pl.BlockSpec((1, CONV_W - 1, CONV_DIM), per_seq),
            *cast_specs,
        ],
        out_shape=[
            jax.ShapeDtypeStruct((nb, s, CONV_DIM), BF16),
            jax.ShapeDtypeStruct((nb, s, q_cols), BF16),
            jax.ShapeDtypeStruct((nb, s, kd_cols), BF16),
            jax.ShapeDtypeStruct((nb, s, kd_cols), BF16),
            jax.ShapeDtypeStruct((nb, WINDOW, KV_DIM), F32),
            jax.ShapeDtypeStruct((nb, WINDOW, KV_DIM), F32),
            jax.ShapeDtypeStruct((nb, CONV_W - 1, CONV_DIM), F32),
            *[jax.ShapeDtypeStruct(w.shape, BF16) for w in cast],
        ],
        scratch_shapes=[pltpu.VMEM((CONV_W - 1, CONV_DIM), F32)],
        compiler_params=pltpu.CompilerParams(
            dimension_semantics=("arbitrary", "arbitrary"), vmem_limit_bytes=VMEM_LIMIT),
        name="proj_prompt",
    )(x, g1, w_in, conv_w, conv_g, cos, sin, prev, *cast)


SMALL_ROWS = 256


def _small_proj_kernel(xm_ref, xs_ref, g1_ref, w_ref, cw_ref, cg_ref, cosm_ref, sinm_ref, coss_ref, sins_ref, prev_ref,
                       wb_ref, kdm_ref, vdm_ref, um_ref, yc_ref, q_ref, kst_ref, vst_ref, ks_ref, vs_ref, us_ref,
                       hn_scr, z_scr):
    j = pl.program_id(0)
    n_meta = xm_ref.shape[0]
    n_all = z_scr.shape[0]

    @pl.when(j == 0)
    def _():
        hn = jnp.concatenate([_rms(xm_ref[...], g1_ref[...]), _rms(xs_ref[...], g1_ref[...])], axis=0).astype(BF16)
        for b in range(hn_scr.shape[0]):
            hn_scr[b] = hn[:, b * SMALL_ROWS:(b + 1) * SMALL_ROWS]
        z_scr[...] = jnp.zeros_like(z_scr)

    wb = w_ref[...].astype(BF16)
    wb_ref[...] = wb
    z_scr[...] += jnp.dot(hn_scr[j], wb, preferred_element_type=F32)

    @pl.when(j == pl.num_programs(0) - 1)
    def _():
        def section(rows, lo, width):
            return z_scr[rows, lo:lo + width]

        cw = cw_ref[...]
        rows = slice(0, n_meta)
        u = section(rows, COL_CG, CONV_DIM) * section(rows, COL_XC, CONV_DIM)
        um_ref[...] = u[n_meta - (CONV_W - 1):, :]
        k = _rope(section(rows, COL_K, KV_DIM), cosm_ref[...], sinm_ref[...])
        v = section(rows, COL_V, KV_DIM)
        pad = jnp.zeros((kdm_ref.shape[0] - n_meta, kdm_ref.shape[1]), BF16)
        kdm_ref[...] = jnp.concatenate([pad, _dup_all(k)], axis=0)
        vdm_ref[...] = jnp.concatenate([pad, _dup_all(v)], axis=0)

        rows = slice(n_meta, n_all)
        u = section(rows, COL_CG, CONV_DIM) * section(rows, COL_XC, CONV_DIM)
        u2 = prev_ref[:, :CONV_DIM]
        u1 = prev_ref[:, CONV_DIM:]
        us_ref[...] = jnp.concatenate([u1, u], axis=-1)
        cy = cw[0:1, :] * u2 + cw[1:2, :] * u1 + cw[2:3, :] * u
        yc_ref[...] = _rms(section(rows, COL_BG, CONV_DIM) * cy, cg_ref[...]).astype(BF16)
        cos = coss_ref[...]
        sin = sins_ref[...]
        q = _rope(section(rows, COL_Q, ATTN_DIM), cos, sin) * Q_SCALE
        k = _rope(section(rows, COL_K, KV_DIM), cos, sin)
        v = section(rows, COL_V, KV_DIM)
        q_ref[...] = _group_minor(q).astype(BF16)
        ks_ref[...] = k
        vs_ref[...] = v
        kst_ref[...] = k.T
        vst_ref[...] = v.T


def _small_proj_call(x_meta, x_sample, g1, w_in, conv_w, conv_g, cos_m, sin_m, cos_s, sin_s, prev_s):
    n_meta = x_meta.shape[0]
    ns = x_sample.shape[0]
    n_blocks = D_MODEL // SMALL_ROWS
    kd_cols = N_KV_HEADS * LANES
    const = lambda j: (0, 0)
    full = lambda shape: pl.BlockSpec(shape, const)
    w_spec = pl.BlockSpec((SMALL_ROWS, IN_COLS), lambda j: (j, 0))
    out_shapes = [
        ((D_MODEL, IN_COLS), BF16),
        ((WINDOW, kd_cols), BF16),
        ((WINDOW, kd_cols), BF16),
        ((CONV_W - 1, CONV_DIM), F32),
        ((ns, CONV_DIM), BF16),
        ((ns, ATTN_DIM), BF16),
        ((KV_DIM, ns), F32),
        ((KV_DIM, ns), F32),
        ((ns, KV_DIM), F32),
        ((ns, KV_DIM), F32),
        ((ns, (CONV_W - 1) * CONV_DIM), F32),
    ]
    return pl.pallas_call(
        _small_proj_kernel,
        grid=(n_blocks,),
        in_specs=[
            full((n_meta, D_MODEL)), full((ns, D_MODEL)), full((1, D_MODEL)), w_spec,
            full((CONV_W, CONV_DIM)), full((1, CONV_DIM)),
            full((n_meta, LANES)), full((n_meta, LANES)), full((ns, LANES)), full((ns, LANES)),
            full((ns, (CONV_W - 1) * CONV_DIM)),
        ],
        out_specs=[w_spec] + [full(shape) for shape, _ in out_shapes[1:]],
        out_shape=[jax.ShapeDtypeStruct(shape, dtype) for shape, dtype in out_shapes],
        scratch_shapes=[pltpu.VMEM((n_blocks, n_meta + ns, SMALL_ROWS), BF16),
                        pltpu.VMEM((n_meta + ns, IN_COLS), F32)],
        compiler_params=pltpu.CompilerParams(
            dimension_semantics=("arbitrary",), vmem_limit_bytes=VMEM_LIMIT),
        name="proj_small",
    )(x_meta, x_sample, g1, w_in, conv_w, conv_g, cos_m, sin_m, cos_s, sin_s, prev_s)


def _attn_prompt_kernel(sinks_ref, q_ref, kc_ref, kp_ref, vc_ref, vp_ref, km_ref, vm_ref, ag_ref,
                        ya_ref, o_scr):
    first = pl.program_id(1) == 0
    blk = WINDOW
    nk = 2 * blk
    n_sub = q_ref.shape[1] // blk

    qi = lax.broadcasted_iota(jnp.int32, (blk, nk), 0)
    kj = lax.broadcasted_iota(jnp.int32, (blk, nk), 1)
    band = (kj >= qi) & (kj <= qi + WINDOW)
    band_first = band & (kj >= jnp.where(first, blk - N_META, 0))

    lane = lax.broadcasted_iota(jnp.int32, (blk, LANES), 1)
    ones = jnp.ones((nk, LANES), BF16)
    for t in range(n_sub):
        rows = slice(t * blk, (t + 1) * blk)
        valid = band_first if t == 0 else band
        for g in range(N_KV_HEADS):
            cols = slice(g * LANES, (g + 1) * LANES)
            if t == 0:
                k_prev = jnp.where(first, km_ref[:, cols], kp_ref[0, :, cols])
                v_prev = jnp.where(first, vm_ref[:, cols], vp_ref[0, :, cols])
            else:
                k_prev = kc_ref[0, (t - 1) * blk:t * blk, cols]
                v_prev = vc_ref[0, (t - 1) * blk:t * blk, cols]
            kd = jnp.concatenate([k_prev, kc_ref[0, rows, cols]], axis=0)
            vd = jnp.concatenate([v_prev, vc_ref[0, rows, cols]], axis=0)
            v_rhs = jnp.concatenate([vd, ones], axis=1)
            for tile in range(g * GROUP // HEADS_PER_TILE, (g + 1) * GROUP // HEADS_PER_TILE):
                ov, m = [], []
                for h in (HEADS_PER_TILE * tile, HEADS_PER_TILE * tile + 1):
                    qm = q_ref[0, rows, h * LANES:(h + 1) * LANES]
                    s = lax.dot_general(qm, kd, (((1,), (1,)), ((), ())), preferred_element_type=F32)
                    s = jnp.where(valid, s, NEG_INF)
                    m.append(jnp.max(s, axis=-1, keepdims=True))
                    p = jnp.exp2(s - m[-1]).astype(BF16)
                    ov.append(jnp.dot(p, v_rhs, preferred_element_type=F32))
                low = lane < HEAD_DIM
                sink = jnp.where(low[:1], sinks_ref[HEADS_PER_TILE * tile], sinks_ref[HEADS_PER_TILE * tile + 1])
                denom = (jnp.where(low, ov[0][:, LANES:], ov[1][:, LANES:])
                         + jnp.exp2(sink * LOG2_E - jnp.where(low, m[0], m[1])))
                o_scr[rows, tile * LANES:(tile + 1) * LANES] = (
                    jnp.where(low, ov[0][:, :LANES], ov[1][:, :LANES]) / denom)
        ya_ref[0, rows, :] = _rms(o_scr[rows, :], ag_ref[...]).astype(BF16)


def _attn_prompt(q, kd, vd, kd_meta, vd_meta, sinks, attn_g, *, n_sub):
    nb, s, kd_cols = kd.shape
    blk = WINDOW
    tq = n_sub * blk
    cur = lambda b, j: (b, j, 0)
    prv = lambda b, j: (b, jnp.maximum(j * n_sub - 1, 0), 0)
    const = lambda b, j: (0, 0)
    return pl.pallas_call(
        _attn_prompt_kernel,
        grid=(nb, s // tq),
        in_specs=[
            pl.BlockSpec(memory_space=pltpu.SMEM),
            pl.BlockSpec((1, tq, N_HEADS * LANES), cur),
            pl.BlockSpec((1, tq, kd_cols), cur),
            pl.BlockSpec((1, blk, kd_cols), prv),
            pl.BlockSpec((1, tq, kd_cols), cur),
            pl.BlockSpec((1, blk, kd_cols), prv),
            pl.BlockSpec((blk, kd_cols), const),
            pl.BlockSpec((blk, kd_cols), const),
            pl.BlockSpec((1, ATTN_DIM), const),
        ],
        out_specs=pl.BlockSpec((1, tq, ATTN_DIM), cur),
        out_shape=jax.ShapeDtypeStruct((nb, s, ATTN_DIM), BF16),
        scratch_shapes=[pltpu.VMEM((tq, ATTN_DIM), F32)],
        compiler_params=pltpu.CompilerParams(
            dimension_semantics=("arbitrary", "arbitrary"), vmem_limit_bytes=VMEM_LIMIT),
        name="attn_prompt",
    )(sinks, q, kd, kd, vd, vd, kd_meta, vd_meta, attn_g)


def _attn_sample_kernel(sinks_ref, q_ref, kn_ref, vn_ref, knt_ref, vnt_ref, ck_ref, cv_ref, ag_ref,
                        ya_ref, nk_ref, nv_ref):
    nseq = q_ref.shape[0]
    base = pl.program_id(0) * nseq
    n_rows = N_HEADS * nseq
    qf = q_ref[...].astype(F32)
    kv_of_lane = lax.broadcasted_iota(jnp.int32, (nseq, KV_DIM), 1) // HEAD_DIM
    q_rows, sink_rows = [], []
    for r in range(N_HEADS):
        i, g = divmod(r, N_KV_HEADS)
        q_rows.append(jnp.where(kv_of_lane == g, qf[:, i * KV_DIM:(i + 1) * KV_DIM], 0.0))
        sink_rows.append(jnp.full((nseq, 1), sinks_ref[g * GROUP + i] * LOG2_E, F32))
    q_all = jnp.concatenate(q_rows, axis=0)
    sink = jnp.concatenate(sink_rows, axis=0)
    q_bf = q_all.astype(BF16)

    key_lane = lax.broadcasted_iota(jnp.int32, (KV_DIM, WINDOW), 1)

    def shifted(cache, new_t, n):
        col = jnp.sum(jnp.where(key_lane == base + n, new_t, 0.0), axis=1, keepdims=True)
        return jnp.where(key_lane == WINDOW - 1, col, pltpu.roll(cache, WINDOW - 1, axis=1))

    seq_of_row = lax.broadcasted_iota(jnp.int32, (n_rows, WINDOW), 0) % nseq
    s_c = jnp.zeros((n_rows, WINDOW), F32)
    for n in range(nseq):
        kt = ck_ref[n]
        s_n = jnp.dot(q_bf, kt.astype(BF16), preferred_element_type=F32)
        s_c = jnp.where(seq_of_row == n, s_n, s_c)
        nk_ref[n] = shifted(kt, knt_ref[...], n)
    kn = jnp.concatenate([kn_ref[...]] * N_HEADS, axis=0)
    vn = jnp.concatenate([vn_ref[...]] * N_HEADS, axis=0)
    s_new = jnp.sum(q_all * kn, axis=-1, keepdims=True)
    m = jnp.maximum(jnp.maximum(jnp.max(s_c, axis=-1, keepdims=True), s_new), sink)
    p_c = jnp.exp2(s_c - m)
    p_new = jnp.exp2(s_new - m)
    denom = jnp.sum(p_c, axis=-1, keepdims=True) + p_new + jnp.exp2(sink - m)
    p_bf = p_c.astype(BF16)

    seq_of_row = lax.broadcasted_iota(jnp.int32, (n_rows, KV_DIM), 0) % nseq
    o = jnp.zeros((n_rows, KV_DIM), F32)
    for n in range(nseq):
        vt = cv_ref[n]
        o_n = lax.dot_general(p_bf, vt.astype(BF16), (((1,), (1,)), ((), ())), preferred_element_type=F32)
        o = jnp.where(seq_of_row == n, o_n, o)
        nv_ref[n] = shifted(vt, vnt_ref[...], n)
    kv_of_row = (lax.broadcasted_iota(jnp.int32, (n_rows, KV_DIM), 0) // nseq) % N_KV_HEADS
    own = (lax.broadcasted_iota(jnp.int32, (n_rows, KV_DIM), 1) // HEAD_DIM) == kv_of_row
    o = jnp.where(own, (o + p_new * vn) / denom, 0.0)
    slabs = []
    for i in range(GROUP):
        lo = i * N_KV_HEADS * nseq
        slabs.append(sum(o[lo + g * nseq:lo + (g + 1) * nseq] for g in range(N_KV_HEADS)))
    ya_ref[...] = _rms(_group_major(jnp.concatenate(slabs, axis=-1)), ag_ref[...]).astype(BF16)


def _attn_sample(q, k_new, v_new, k_new_t, v_new_t, cache_kt, cache_vt, sinks, attn_g, *, nseq):
    n = q.shape[0]
    rows = lambda i: (i, 0)
    seqs = lambda i: (i, 0, 0)
    const = lambda i: (0, 0)
    return pl.pallas_call(
        _attn_sample_kernel,
        grid=(n // nseq,),
        in_specs=[
            pl.BlockSpec(memory_space=pltpu.SMEM),
            pl.BlockSpec((nseq, ATTN_DIM), rows),
            pl.BlockSpec((nseq, KV_DIM), rows),
            pl.BlockSpec((nseq, KV_DIM), rows),
            pl.BlockSpec((KV_DIM, n), const),
            pl.BlockSpec((KV_DIM, n), const),
            pl.BlockSpec((nseq, KV_DIM, WINDOW), seqs),
            pl.BlockSpec((nseq, KV_DIM, WINDOW), seqs),
            pl.BlockSpec((1, ATTN_DIM), const),
        ],
        out_specs=[
            pl.BlockSpec((nseq, ATTN_DIM), rows),
            pl.BlockSpec((nseq, KV_DIM, WINDOW), seqs),
            pl.BlockSpec((nseq, KV_DIM, WINDOW), seqs),
        ],
        out_shape=[
            jax.ShapeDtypeStruct((n, ATTN_DIM), BF16),
            jax.ShapeDtypeStruct((n, KV_DIM, WINDOW), F32),
            jax.ShapeDtypeStruct((n, KV_DIM, WINDOW), F32),
        ],
        compiler_params=pltpu.CompilerParams(
            dimension_semantics=("arbitrary",), vmem_limit_bytes=VMEM_LIMIT),
        name="attn_sample",
    )(sinks, q, k_new, v_new, k_new_t, v_new_t, cache_kt, cache_vt, attn_g)


def _mlp_kernel(chunks, x_ref, yc_ref, ya_ref, wo_ref, g2_ref, wg_hbm, wu_hbm, wd_hbm, gf_ref,
                y_ref, hn_scr, wg_buf, wu_buf, wd_buf, sem):
    i = pl.program_id(0)
    n_tiles = pl.num_programs(0)
    n_slots = wg_buf.shape[0]
    n_chunks = len(chunks)
    ahead = n_slots - 1

    def copies(off, width, slot):
        cols = pl.ds(off, width)
        return (pltpu.make_async_copy(wg_hbm.at[:, cols], wg_buf.at[slot, :, :width], sem.at[0, slot]),
                pltpu.make_async_copy(wu_hbm.at[:, cols], wu_buf.at[slot, :, :width], sem.at[1, slot]),
                pltpu.make_async_copy(wd_hbm.at[cols, :], wd_buf.at[slot, :width, :], sem.at[2, slot]))

    def start(c):
        for cp in copies(*chunks[c], c % n_slots):
            cp.start()

    def wait(c):
        for cp in copies(*chunks[c], c % n_slots):
            cp.wait()

    def consume(slot, width):
        hn = hn_scr[...]
        for lo in range(0, width, MLP_PIECE):
            cols = slice(lo, min(lo + MLP_PIECE, width))
            gate = jnp.dot(hn, wg_buf[slot, :, cols], preferred_element_type=F32)
            up = jnp.dot(hn, wu_buf[slot, :, cols], preferred_element_type=F32)
            act = (gate * jax.nn.sigmoid(gate) * up).astype(BF16)
            y_ref[...] += jnp.dot(act, wd_buf[slot, cols, :], preferred_element_type=F32)

    @pl.when(i == 0)
    def _():
        for c in range(min(ahead, n_chunks)):
            start(c)

    mix = (jnp.dot(yc_ref[...], wo_ref[:CONV_DIM, :], preferred_element_type=F32)
           + jnp.dot(ya_ref[...], wo_ref[CONV_DIM:, :], preferred_element_type=F32))
    h = x_ref[...] + mix
    y_ref[...] = h
    hn_scr[...] = _rms(h, g2_ref[...]).astype(BF16)

    tf = chunks[0][1]
    n_rolled = 0
    while n_rolled + ahead < n_chunks and all(w == tf for _, w in chunks[n_rolled:n_rolled + ahead + 1]):
        n_rolled += 1

    def rolled(c, carry):
        nxt = c + ahead
        for cp in copies(pl.multiple_of(nxt * tf, tf), tf, nxt % n_slots):
            cp.start()
        slot = c % n_slots
        for cp in copies(pl.multiple_of(c * tf, tf), tf, slot):
            cp.wait()
        consume(slot, tf)
        return carry

    lax.fori_loop(0, n_rolled, rolled, 0)

    for c in range(n_rolled, n_chunks):
        nxt = c + ahead
        if nxt < n_chunks:
            start(nxt)
        else:
            @pl.when(i + 1 < n_tiles)
            def _():
                start(nxt - n_chunks)
        wait(c)
        consume(c % n_slots, chunks[c][1])

    y_ref[...] = _rms(y_ref[...], gf_ref[...])


def _mlp_chunks(d_ff, tf):
    bounds = list(range(0, d_ff, tf)) + [d_ff]
    return tuple((lo, hi - lo) for lo, hi in zip(bounds[:-1], bounds[1:]))


def _mlp_call(x, yc, ya, w_out, g2, w_gate, w_up, w_down, gf, *, tm, tf, n_slots, name):
    r = x.shape[0]
    chunks = _mlp_chunks(w_gate.shape[1], tf)
    assert r == tm or len(chunks) % n_slots == 0
    assert len(chunks) >= n_slots and all(w % LANES == 0 for _, w in chunks)
    rows = lambda i: (i, 0)
    const = lambda i: (0, 0)
    return pl.pallas_call(
        functools.partial(_mlp_kernel, chunks),
        grid=(r // tm,),
        in_specs=[
            pl.BlockSpec((tm, D_MODEL), rows),
            pl.BlockSpec((tm, CONV_DIM), rows),
            pl.BlockSpec((tm, ATTN_DIM), rows),
            pl.BlockSpec((D_MODEL, D_MODEL), const, pipeline_mode=pl.Buffered(1)),
            pl.BlockSpec((1, D_MODEL), const),
            pl.BlockSpec(memory_space=pl.ANY),
            pl.BlockSpec(memory_space=pl.ANY),
            pl.BlockSpec(memory_space=pl.ANY),
            pl.BlockSpec((1, D_MODEL), const),
        ],
        out_specs=pl.BlockSpec((tm, D_MODEL), rows),
        out_shape=jax.ShapeDtypeStruct((r, D_MODEL), F32),
        scratch_shapes=[
            pltpu.VMEM((tm, D_MODEL), BF16),
            pltpu.VMEM((n_slots, D_MODEL, tf), BF16),
            pltpu.VMEM((n_slots, D_MODEL, tf), BF16),
            pltpu.VMEM((n_slots, tf, D_MODEL), BF16),
            pltpu.SemaphoreType.DMA((3, n_slots)),
        ],
        compiler_params=pltpu.CompilerParams(
            dimension_semantics=("arbitrary",), vmem_limit_bytes=VMEM_LIMIT),
        name=name,
    )(x, yc, ya, w_out, g2, w_gate, w_up, w_down, gf)


def _rope_tables(pos):
    inv = ROPE_THETA ** (-jnp.arange(HALF_HEAD, dtype=F32) / HALF_HEAD)
    ang = pos.astype(F32)[:, None] * inv[None, :]
    cos = jnp.cos(ang)
    sin = jnp.sin(ang)
    reps = LANES // HEAD_DIM
    return (jnp.tile(jnp.concatenate([cos, cos], axis=-1), (1, reps)),
            jnp.tile(jnp.concatenate([-sin, sin], axis=-1), (1, reps)))


def kernel(x_prompt, x_sample, cache_k, cache_v, state_conv, meta_tokens, norm1_g, w_in, conv_w, conv_norm_g,
           attn_norm_g, attn_sinks, w_out, norm2_g, w_gate, w_up, w_down, final_norm_g):
    depth = w_in.shape[0]
    assert depth == 1, "single-layer step only"
    nb, seq, _ = x_prompt.shape
    ns, dec_seq, _ = x_sample.shape
    assert dec_seq == 1

    g1 = norm1_g[0][None]
    g2 = norm2_g[0][None]
    gf = final_norm_g[None]
    cg = conv_norm_g[0][None]
    ag = attn_norm_g[0][None]
    cw = conv_w[0]
    sinks = attn_sinks[0]
    xs = x_sample.reshape(ns, D_MODEL)
    cos_m, sin_m = _rope_tables(jnp.arange(N_META))
    cos_s, sin_s = _rope_tables(jnp.full((ns,), PAST_LEN))
    prev_s = state_conv[0].reshape(ns, (CONV_W - 1) * CONV_DIM)
    w_in_b, kd_meta, vd_meta, u_m, yc_s, q_s, k_st, v_st, k_s, v_s, u_s = _small_proj_call(
        meta_tokens, xs, g1, w_in[0], cw, cg, cos_m, sin_m, cos_s, sin_s, prev_s)

    cos_p, sin_p = _rope_tables(N_META + jnp.arange(seq))
    yc, q, kd, vd, k_tail, v_tail, u_tail, w_out_b, w_gate_b, w_up_b, w_down_b = _proj_call(
        x_prompt, g1, w_in_b, cw, cg, cos_p, sin_p, u_m, tm=PROJ_ROWS, n_sub=PROJ_SUB_TILES,
        cast=(w_out[0], w_gate[0], w_up[0], w_down[0]))
    ya = _attn_prompt(q, kd, vd, kd_meta, vd_meta, sinks, ag, n_sub=ATTN_BLOCKS_PER_STEP)
    rp = nb * seq
    y_prompt = _mlp_call(x_prompt.reshape(rp, D_MODEL), yc.reshape(rp, CONV_DIM), ya.reshape(rp, ATTN_DIM),
                         w_out_b, g2, w_gate_b, w_up_b, w_down_b, gf,
                         tm=MLP_ROWS, tf=MLP_CHUNK, n_slots=MLP_SLOTS, name="mlp_prompt")
    y_prompt = y_prompt.reshape(nb, seq, D_MODEL)
    new_k_prompt = k_tail.reshape(1, nb, WINDOW, N_KV_HEADS, HEAD_DIM)
    new_v_prompt = v_tail.reshape(1, nb, WINDOW, N_KV_HEADS, HEAD_DIM)
    new_conv_prompt = u_tail[None]

    cache_kt = jnp.transpose(cache_k[0], (0, 2, 3, 1)).reshape(ns, KV_DIM, WINDOW)
    cache_vt = jnp.transpose(cache_v[0], (0, 2, 3, 1)).reshape(ns, KV_DIM, WINDOW)
    ya_s, nk_s, nv_s = _attn_sample(q_s, k_s, v_s, k_st, v_st, cache_kt, cache_vt, sinks, ag,
                                    nseq=SAMPLE_SEQS_PER_STEP)
    y_sample = _mlp_call(xs, yc_s, ya_s, w_out_b, g2, w_gate_b, w_up_b, w_down_b, gf,
                         tm=ns, tf=SAMPLE_MLP_CHUNK, n_slots=SAMPLE_MLP_SLOTS, name="mlp_sample")
    y_sample = y_sample.reshape(ns, 1, D_MODEL)
    new_k_sample = jnp.transpose(nk_s.reshape(ns, N_KV_HEADS, HEAD_DIM, WINDOW), (0, 3, 1, 2))[None]
    new_v_sample = jnp.transpose(nv_s.reshape(ns, N_KV_HEADS, HEAD_DIM, WINDOW), (0, 3, 1, 2))[None]
    new_conv_sample = u_s.reshape(1, ns, CONV_W - 1, CONV_DIM)

    return (y_prompt, y_sample, new_k_prompt, new_v_prompt, new_conv_prompt,
            new_k_sample, new_v_sample, new_conv_sample)
```

```python
import functools

import jax
import jax.numpy as jnp
from jax import lax
from jax.experimental import pallas as pl
from jax.experimental.pallas import tpu as pltpu

D_MODEL = 2048
N_META = 16
CONV_DIM = 1024
CONV_W = 3
HEAD_DIM = 64
HALF_HEAD = HEAD_DIM // 2
ATTN_DIM = 1024
N_HEADS = 16
N_KV_HEADS = 4
GROUP = N_HEADS // N_KV_HEADS
KV_DIM = N_KV_HEADS * HEAD_DIM
WINDOW = 128
ROPE_THETA = 10000.0
RMS_EPS = 1e-6
NEG_INF = -1e30
PAST_LEN = 16384
LOG2_E = 1.4426950408889634

COL_BG = 0
COL_CG = CONV_DIM
COL_XC = 2 * CONV_DIM
COL_Q = 3 * CONV_DIM
COL_K = COL_Q + ATTN_DIM
COL_V = COL_K + KV_DIM
IN_COLS = COL_V + KV_DIM

LANES = 128
BF16_ROWS_PER_TILE = 16
HEADS_PER_TILE = LANES // HEAD_DIM
VMEM_LIMIT = 60 * 1024 * 1024

PROJ_ROWS = 512
PROJ_SUB_TILES = 2
ATTN_BLOCKS_PER_STEP = 4
SAMPLE_SEQS_PER_STEP = 16
MLP_ROWS = 512
MLP_CHUNK = 1024
MLP_SLOTS = 2
SAMPLE_MLP_CHUNK = 512
SAMPLE_MLP_SLOTS = 4

F32 = jnp.float32
BF16 = jnp.bfloat16


def _rms(x, g):
    return x * lax.rsqrt(jnp.mean(x * x, axis=-1, keepdims=True) + RMS_EPS) * g


def _lane_tile(t, width):
    return jnp.concatenate([t] * (width // t.shape[-1]), axis=-1)


def _rope(t, cos, sin_signed):
    w = t.shape[-1]
    lane = lax.broadcasted_iota(jnp.int32, t.shape, 1)
    first_half = (lane & (HEAD_DIM - 1)) < HALF_HEAD
    swapped = jnp.where(first_half, pltpu.roll(t, w - HALF_HEAD, axis=1), pltpu.roll(t, HALF_HEAD, axis=1))
    return t * _lane_tile(cos, w) + swapped * _lane_tile(sin_signed, w)


def _dup_heads(pair_tile, half):
    lane = lax.broadcasted_iota(jnp.int32, pair_tile.shape, 1)
    swapped = pltpu.roll(pair_tile, HEAD_DIM, axis=1)
    if half == 0:
        return jnp.where(lane < HEAD_DIM, pair_tile, swapped)
    return jnp.where(lane < HEAD_DIM, swapped, pair_tile)


def _dup_all(t):
    tiles = []
    for g in range(N_KV_HEADS):
        pair, half = divmod(g, HEADS_PER_TILE)
        tiles.append(_dup_heads(t[:, pair * LANES:(pair + 1) * LANES], half))
    return jnp.concatenate(tiles, axis=-1).astype(BF16)


def _spread_heads(q):
    lane = lax.broadcasted_iota(jnp.int32, (q.shape[0], LANES), 1)
    tiles = []
    for h in range(N_HEADS):
        tile, half = divmod(h, HEADS_PER_TILE)
        qt = q[:, tile * LANES:(tile + 1) * LANES]
        keep = (lane < HEAD_DIM) if half == 0 else (lane >= HEAD_DIM)
        tiles.append(jnp.where(keep, qt, jnp.zeros_like(qt)))
    return jnp.concatenate(tiles, axis=-1)


def _head_block(t, h):
    return t[:, h * HEAD_DIM:(h + 1) * HEAD_DIM]


def _group_minor(t):
    return jnp.concatenate([_head_block(t, g * GROUP + i) for i in range(GROUP) for g in range(N_KV_HEADS)], axis=-1)


def _group_major(t):
    return jnp.concatenate([_head_block(t, i * N_KV_HEADS + g) for g in range(N_KV_HEADS) for i in range(GROUP)],
                           axis=-1)


def _conv_mix(u, p2, p1, cw):
    row = lax.broadcasted_iota(jnp.int32, u.shape, 0)
    u1 = jnp.where(row == 0, p1, pltpu.roll(u, 1, axis=0))
    u2 = jnp.where(row == 0, p2, jnp.where(row == 1, p1, pltpu.roll(u, 2, axis=0)))
    return cw[0:1, :] * u2 + cw[1:2, :] * u1 + cw[2:3, :] * u


Q_SCALE = HEAD_DIM ** -0.5 * LOG2_E

N_PROJ_IN = 8
N_PROJ_OUT = 7


def _proj_kernel(n_sub, *refs):
    x_ref, g1_ref, w_ref, cw_ref, cg_ref, cos_ref, sin_ref, prev_ref = refs[:N_PROJ_IN]
    n_cast = (len(refs) - 1 - N_PROJ_IN - N_PROJ_OUT) // 2
    cast_in = refs[N_PROJ_IN:N_PROJ_IN + n_cast]
    yc_ref, q_ref, kd_ref, vd_ref, kt_ref, vt_ref, u_ref = refs[N_PROJ_IN + n_cast:N_PROJ_IN + n_cast + N_PROJ_OUT]
    cast_out = refs[N_PROJ_IN + n_cast + N_PROJ_OUT:-1]
    carry_ref = refs[-1]

    ts = x_ref.shape[1] // n_sub
    cw = cw_ref[...]

    @pl.when(pl.program_id(1) == 0)
    def _():
        carry_ref[...] = prev_ref[...]

    p2 = carry_ref[0:1, :]
    p1 = carry_ref[1:2, :]

    for t in range(n_sub):
        rows = slice(t * ts, (t + 1) * ts)
        hn = _rms(x_ref[0, rows, :], g1_ref[...]).astype(BF16)

        def section(lo, width):
            return jnp.dot(hn, w_ref[:, lo:lo + width], preferred_element_type=F32)

        u = section(COL_CG, CONV_DIM) * section(COL_XC, CONV_DIM)
        cy = _conv_mix(u, p2, p1, cw)
        p2 = u[ts - 2:ts - 1, :]
        p1 = u[ts - 1:, :]
        yc_ref[0, rows, :] = _rms(section(COL_BG, CONV_DIM) * cy, cg_ref[...]).astype(BF16)

        cos = cos_ref[rows, :]
        sin = sin_ref[rows, :]
        q = _rope(section(COL_Q, ATTN_DIM), cos, sin) * Q_SCALE
        k = _rope(section(COL_K, KV_DIM), cos, sin)
        v = section(COL_V, KV_DIM)
        q_ref[0, rows, :] = _spread_heads(q.astype(BF16))
        kd_ref[0, rows, :] = _dup_all(k)
        vd_ref[0, rows, :] = _dup_all(v)

        if t == 0:
            for src, dst in zip(cast_in, cast_out):
                dst[...] = src[...].astype(BF16)

    tail = jnp.concatenate([p2, p1], axis=0)
    carry_ref[...] = tail
    u_ref[0] = tail
    tail_rows = kt_ref.shape[1]
    kt_ref[0] = k[ts - tail_rows:, :]
    vt_ref[0] = v[ts - tail_rows:, :]


def _proj_call(x, g1, w_in, conv_w, conv_g, cos, sin, prev, *, tm, n_sub, cast=()):
    nb, s, _ = x.shape
    n_steps = nb * (s // tm)
    slab = lambda b, i: (b * (s // tm) + i, 0)
    cast_specs = [pl.BlockSpec((w.shape[0] // n_steps, w.shape[1]), slab) for w in cast]
    assert all(w.shape[0] % (n_steps * BF16_ROWS_PER_TILE) == 0 for w in cast)
    assert tm // n_sub >= WINDOW
    kd_cols = N_KV_HEADS * LANES
    q_cols = N_HEADS * LANES
    const = lambda b, i: (0, 0)
    rows = lambda b, i: (b, i, 0)
    per_seq = lambda b, i: (b, 0, 0)
    return pl.pallas_call(
        functools.partial(_proj_kernel, n_sub),
        grid=(nb, s // tm),
        in_specs=[
            pl.BlockSpec((1, tm, D_MODEL), rows),
            pl.BlockSpec((1, D_MODEL), const),
            pl.BlockSpec((D_MODEL, IN_COLS), const, pipeline_mode=pl.Buffered(1)),
            pl.BlockSpec((CONV_W, CONV_DIM), const),
            pl.BlockSpec((1, CONV_DIM), const),
            pl.BlockSpec((tm, LANES), lambda b, i: (i, 0)),
            pl.BlockSpec((tm, LANES), lambda b, i: (i, 0)),
            pl.BlockSpec((CONV_W - 1, CONV_DIM), const),
            *cast_specs,
        ],
        out_specs=[
            pl.BlockSpec((1, tm, CONV_DIM), rows),
            pl.BlockSpec((1, tm, q_cols), rows),
            pl.BlockSpec((1, tm, kd_cols), rows),
            pl.BlockSpec((1, tm, kd_cols), rows),
            pl.BlockSpec((1, WINDOW, KV_DIM), per_seq),
            pl.BlockSpec((1, WINDOW, KV_DIM), per_seq),
            pl.BlockSpec((1, CONV_W - 1, CONV_DIM), per_seq),
            *cast_specs,
        ],
        out_shape=[
            jax.ShapeDtypeStruct((nb, s, CONV_DIM), BF16),
            jax.ShapeDtypeStruct((nb, s, q_cols), BF16),
            jax.ShapeDtypeStruct((nb, s, kd_cols), BF16),
            jax.ShapeDtypeStruct((nb, s, kd_cols), BF16),
            jax.ShapeDtypeStruct((nb, WINDOW, KV_DIM), F32),
            jax.ShapeDtypeStruct((nb, WINDOW, KV_DIM), F32),
            jax.ShapeDtypeStruct((nb, CONV_W - 1, CONV_DIM), F32),
            *[jax.ShapeDtypeStruct(w.shape, BF16) for w in cast],
        ],
        scratch_shapes=[pltpu.VMEM((CONV_W - 1, CONV_DIM), F32)],
        compiler_params=pltpu.CompilerParams(
            dimension_semantics=("arbitrary", "arbitrary"), vmem_limit_bytes=VMEM_LIMIT),
        name="proj_prompt",
    )(x, g1, w_in, conv_w, conv_g, cos, sin, prev, *cast)


SMALL_ROWS = 256


def _small_proj_kernel(xm_ref, xs_ref, g1_ref, w_ref, cw_ref, cg_ref, cosm_ref, sinm_ref, coss_ref, sins_ref, prev_ref,
                       wb_ref, kdm_ref, vdm_ref, um_ref, yc_ref, q_ref, kst_ref, vst_ref, ks_ref, vs_ref, us_ref,
                       hn_scr, z_scr):
    j = pl.program_id(0)
    n_meta = xm_ref.shape[0]
    n_all = z_scr.shape[0]

    @pl.when(j == 0)
    def _():
        hn = jnp.concatenate([_rms(xm_ref[...], g1_ref[...]), _rms(xs_ref[...], g1_ref[...])], axis=0).astype(BF16)
        for b in range(hn_scr.shape[0]):
            hn_scr[b] = hn[:, b * SMALL_ROWS:(b + 1) * SMALL_ROWS]
        z_scr[...] = jnp.zeros_like(z_scr)

    wb = w_ref[...].astype(BF16)
    wb_ref[...] = wb
    z_scr[...] += jnp.dot(hn_scr[j], wb, preferred_element_type=F32)

    @pl.when(j == pl.num_programs(0) - 1)
    def _():
        def section(rows, lo, width):
            return z_scr[rows, lo:lo + width]

        cw = cw_ref[...]
        rows = slice(0, n_meta)
        u = section(rows, COL_CG, CONV_DIM) * section(rows, COL_XC, CONV_DIM)
        um_ref[...] = u[n_meta - (CONV_W - 1):, :]
        k = _rope(section(rows, COL_K, KV_DIM), cosm_ref[...], sinm_ref[...])
        v = section(rows, COL_V, KV_DIM)
        pad = jnp.zeros((kdm_ref.shape[0] - n_meta, kdm_ref.shape[1]), BF16)
        kdm_ref[...] = jnp.concatenate([pad, _dup_all(k)], axis=0)
        vdm_ref[...] = jnp.concatenate([pad, _dup_all(v)], axis=0)

        rows = slice(n_meta, n_all)
        u = section(rows, COL_CG, CONV_DIM) * section(rows, COL_XC, CONV_DIM)
        u2 = prev_ref[:, :CONV_DIM]
        u1 = prev_ref[:, CONV_DIM:]
        us_ref[...] = jnp.concatenate([u1, u], axis=-1)
        cy = cw[0:1, :] * u2 + cw[1:2, :] * u1 + cw[2:3, :] * u
        yc_ref[...] = _rms(section(rows, COL_BG, CONV_DIM) * cy, cg_ref[...]).astype(BF16)
        cos = coss_ref[...]
        sin = sins_ref[...]
        q = _rope(section(rows, COL_Q, ATTN_DIM), cos, sin) * Q_SCALE
        k = _rope(section(rows, COL_K, KV_DIM), cos, sin)
        v = section(rows, COL_V, KV_DIM)
        q_ref[...] = _group_minor(q).astype(BF16)
        ks_ref[...] = k
        vs_ref[...] = v
        kst_ref[...] = k.T
        vst_ref[...] = v.T


def _small_proj_call(x_meta, x_sample, g1, w_in, conv_w, conv_g, cos_m, sin_m, cos_s, sin_s, prev_s):
    n_meta = x_meta.shape[0]
    ns = x_sample.shape[0]
    n_blocks = D_MODEL // SMALL_ROWS
    kd_cols = N_KV_HEADS * LANES
    const = lambda j: (0, 0)
    full = lambda shape: pl.BlockSpec(shape, const)
    w_spec = pl.BlockSpec((SMALL_ROWS, IN_COLS), lambda j: (j, 0))
    out_shapes = [
        ((D_MODEL, IN_COLS), BF16),
        ((WINDOW, kd_cols), BF16),
        ((WINDOW, kd_cols), BF16),
        ((CONV_W - 1, CONV_DIM), F32),
        ((ns, CONV_DIM), BF16),
        ((ns, ATTN_DIM), BF16),
        ((KV_DIM, ns), F32),
        ((KV_DIM, ns), F32),
        ((ns, KV_DIM), F32),
        ((ns, KV_DIM), F32),
        ((ns, (CONV_W - 1) * CONV_DIM), F32),
    ]
    return pl.pallas_call(
        _small_proj_kernel,
        grid=(n_blocks,),
        in_specs=[
            full((n_meta, D_MODEL)), full((ns, D_MODEL)), full((1, D_MODEL)), w_spec,
            full((CONV_W, CONV_DIM)), full((1, CONV_DIM)),
            full((n_meta, LANES)), full((n_meta, LANES)), full((ns, LANES)), full((ns, LANES)),
            full((ns, (CONV_W - 1) * CONV_DIM)),
        ],
        out_specs=[w_spec] + [full(shape) for shape, _ in out_shapes[1:]],
        out_shape=[jax.ShapeDtypeStruct(shape, dtype) for shape, dtype in out_shapes],
        scratch_shapes=[pltpu.VMEM((n_blocks, n_meta + ns, SMALL_ROWS), BF16),
                        pltpu.VMEM((n_meta + ns, IN_COLS), F32)],
        compiler_params=pltpu.CompilerParams(
            dimension_semantics=("arbitrary",), vmem_limit_bytes=VMEM_LIMIT),
        name="proj_small",
    )(x_meta, x_sample, g1, w_in, conv_w, conv_g, cos_m, sin_m, cos_s, sin_s, prev_s)


def _attn_prompt_kernel(sinks_ref, q_ref, kc_ref, kp_ref, vc_ref, vp_ref, km_ref, vm_ref, ag_ref,
                        ya_ref, o_scr):
    first = pl.program_id(1) == 0
    blk = WINDOW
    nk = 2 * blk
    n_sub = q_ref.shape[1] // blk

    qi = lax.broadcasted_iota(jnp.int32, (blk, nk), 0)
    kj = lax.broadcasted_iota(jnp.int32, (blk, nk), 1)
    band = (kj >= qi) & (kj <= qi + WINDOW)
    band_first = band & (kj >= jnp.where(first, blk - N_META, 0))

    lane = lax.broadcasted_iota(jnp.int32, (blk, LANES), 1)
    ones = jnp.ones((nk, LANES), BF16)
    for t in range(n_sub):
        rows = slice(t * blk, (t + 1) * blk)
        valid = band_first if t == 0 else band
        for g in range(N_KV_HEADS):
            cols = slice(g * LANES, (g + 1) * LANES)
            if t == 0:
                k_prev = jnp.where(first, km_ref[:, cols], kp_ref[0, :, cols])
                v_prev = jnp.where(first, vm_ref[:, cols], vp_ref[0, :, cols])
            else:
                k_prev = kc_ref[0, (t - 1) * blk:t * blk, cols]
                v_prev = vc_ref[0, (t - 1) * blk:t * blk, cols]
            kd = jnp.concatenate([k_prev, kc_ref[0, rows, cols]], axis=0)
            vd = jnp.concatenate([v_prev, vc_ref[0, rows, cols]], axis=0)
            v_rhs = jnp.concatenate([vd, ones], axis=1)
            for tile in range(g * GROUP // HEADS_PER_TILE, (g + 1) * GROUP // HEADS_PER_TILE):
                ov, m = [], []
                for h in (HEADS_PER_TILE * tile, HEADS_PER_TILE * tile + 1):
                    qm = q_ref[0, rows, h * LANES:(h + 1) * LANES]
                    s = lax.dot_general(qm, kd, (((1,), (1,)), ((), ())), preferred_element_type=F32)
                    s = jnp.where(valid, s, NEG_INF)
                    m.append(jnp.max(s, axis=-1, keepdims=True))
                    p = jnp.exp2(s - m[-1]).astype(BF16)
                    ov.append(jnp.dot(p, v_rhs, preferred_element_type=F32))
                low = lane < HEAD_DIM
                sink = jnp.where(low[:1], sinks_ref[HEADS_PER_TILE * tile], sinks_ref[HEADS_PER_TILE * tile + 1])
                denom = (jnp.where(low, ov[0][:, LANES:], ov[1][:, LANES:])
                         + jnp.exp2(sink * LOG2_E - jnp.where(low, m[0], m[1])))
                o_scr[rows, tile * LANES:(tile + 1) * LANES] = (
                    jnp.where(low, ov[0][:, :LANES], ov[1][:, :LANES]) / denom)
        ya_ref[0, rows, :] = _rms(o_scr[rows, :], ag_ref[...]).astype(BF16)


def _attn_prompt(q, kd, vd, kd_meta, vd_meta, sinks, attn_g, *, n_sub):
    nb, s, kd_cols = kd.shape
    blk = WINDOW
    tq = n_sub * blk
    cur = lambda b, j: (b, j, 0)
    prv = lambda b, j: (b, jnp.maximum(j * n_sub - 1, 0), 0)
    const = lambda b, j: (0, 0)
    return pl.pallas_call(
        _attn_prompt_kernel,
        grid=(nb, s // tq),
        in_specs=[
            pl.BlockSpec(memory_space=pltpu.SMEM),
            pl.BlockSpec((1, tq, N_HEADS * LANES), cur),
            pl.BlockSpec((1, tq, kd_cols), cur),
            pl.BlockSpec((1, blk, kd_cols), prv),
            pl.BlockSpec((1, tq, kd_cols), cur),
            pl.BlockSpec((1, blk, kd_cols), prv),
            pl.BlockSpec((blk, kd_cols), const),
            pl.BlockSpec((blk, kd_cols), const),
            pl.BlockSpec((1, ATTN_DIM), const),
        ],
        out_specs=pl.BlockSpec((1, tq, ATTN_DIM), cur),
        out_shape=jax.ShapeDtypeStruct((nb, s, ATTN_DIM), BF16),
        scratch_shapes=[pltpu.VMEM((tq, ATTN_DIM), F32)],
        compiler_params=pltpu.CompilerParams(
            dimension_semantics=("arbitrary", "arbitrary"), vmem_limit_bytes=VMEM_LIMIT),
        name="attn_prompt",
    )(sinks, q, kd, kd, vd, vd, kd_meta, vd_meta, attn_g)


def _attn_sample_kernel(sinks_ref, q_ref, kn_ref, vn_ref, knt_ref, vnt_ref, ck_ref, cv_ref, ag_ref,
                        ya_ref, nk_ref, nv_ref):
    nseq = q_ref.shape[0]
    base = pl.program_id(0) * nseq
    n_rows = N_HEADS * nseq
    qf = q_ref[...].astype(F32)
    kv_of_lane = lax.broadcasted_iota(jnp.int32, (nseq, KV_DIM), 1) // HEAD_DIM
    q_rows, sink_rows = [], []
    for r in range(N_HEADS):
        i, g = divmod(r, N_KV_HEADS)
        q_rows.append(jnp.where(kv_of_lane == g, qf[:, i * KV_DIM:(i + 1) * KV_DIM], 0.0))
        sink_rows.append(jnp.full((nseq, 1), sinks_ref[g * GROUP + i] * LOG2_E, F32))
    q_all = jnp.concatenate(q_rows, axis=0)
    sink = jnp.concatenate(sink_rows, axis=0)
    q_bf = q_all.astype(BF16)

    key_lane = lax.broadcasted_iota(jnp.int32, (KV_DIM, WINDOW), 1)

    def shifted(cache, new_t, n):
        col = jnp.sum(jnp.where(key_lane == base + n, new_t, 0.0), axis=1, keepdims=True)
        return jnp.where(key_lane == WINDOW - 1, col, pltpu.roll(cache, WINDOW - 1, axis=1))

    seq_of_row = lax.broadcasted_iota(jnp.int32, (n_rows, WINDOW), 0) % nseq
    s_c = jnp.zeros((n_rows, WINDOW), F32)
    for n in range(nseq):
        kt = ck_ref[n]
        s_n = jnp.dot(q_bf, kt.astype(BF16), preferred_element_type=F32)
        s_c = jnp.where(seq_of_row == n, s_n, s_c)
        nk_ref[n] = shifted(kt, knt_ref[...], n)
    kn = jnp.concatenate([kn_ref[...]] * N_HEADS, axis=0)
    vn = jnp.concatenate([vn_ref[...]] * N_HEADS, axis=0)
    s_new = jnp.sum(q_all * kn, axis=-1, keepdims=True)
    m = jnp.maximum(jnp.maximum(jnp.max(s_c, axis=-1, keepdims=True), s_new), sink)
    p_c = jnp.exp2(s_c - m)
    p_new = jnp.exp2(s_new - m)
    denom = jnp.sum(p_c, axis=-1, keepdims=True) + p_new + jnp.exp2(sink - m)
    p_bf = p_c.astype(BF16)

    seq_of_row = lax.broadcasted_iota(jnp.int32, (n_rows, KV_DIM), 0) % nseq
    o = jnp.zeros((n_rows, KV_DIM), F32)
    for n in range(nseq):
        vt = cv_ref[n]
        o_n = lax.dot_general(p_bf, vt.astype(BF16), (((1,), (1,)), ((), ())), preferred_element_type=F32)
        o = jnp.where(seq_of_row == n, o_n, o)
        nv_ref[n] = shifted(vt, vnt_ref[...], n)
    kv_of_row = (lax.broadcasted_iota(jnp.int32, (n_rows, KV_DIM), 0) // nseq) % N_KV_HEADS
    own = (lax.broadcasted_iota(jnp.int32, (n_rows, KV_DIM), 1) // HEAD_DIM) == kv_of_row
    o = jnp.where(own, (o + p_new * vn) / denom, 0.0)
    slabs = []
    for i in range(GROUP):
        lo = i * N_KV_HEADS * nseq
        slabs.append(sum(o[lo + g * nseq:lo + (g + 1) * nseq] for g in range(N_KV_HEADS)))
    ya_ref[...] = _rms(_group_major(jnp.concatenate(slabs, axis=-1)), ag_ref[...]).astype(BF16)


def _attn_sample(q, k_new, v_new, k_new_t, v_new_t, cache_kt, cache_vt, sinks, attn_g, *, nseq):
    n = q.shape[0]
    rows = lambda i: (i, 0)
    seqs = lambda i: (i, 0, 0)
    const = lambda i: (0, 0)
    return pl.pallas_call(
        _attn_sample_kernel,
        grid=(n // nseq,),
        in_specs=[
            pl.BlockSpec(memory_space=pltpu.SMEM),
            pl.BlockSpec((nseq, ATTN_DIM), rows),
            pl.BlockSpec((nseq, KV_DIM), rows),
            pl.BlockSpec((nseq, KV_DIM), rows),
            pl.BlockSpec((KV_DIM, n), const),
            pl.BlockSpec((KV_DIM, n), const),
            pl.BlockSpec((nseq, KV_DIM, WINDOW), seqs),
            pl.BlockSpec((nseq, KV_DIM, WINDOW), seqs),
            pl.BlockSpec((1, ATTN_DIM), const),
        ],
        out_specs=[
            pl.BlockSpec((nseq, ATTN_DIM), rows),
            pl.BlockSpec((nseq, KV_DIM, WINDOW), seqs),
            pl.BlockSpec((nseq, KV_DIM, WINDOW), seqs),
        ],
        out_shape=[
            jax.ShapeDtypeStruct((n, ATTN_DIM), BF16),
            jax.ShapeDtypeStruct((n, KV_DIM, WINDOW), F32),
            jax.ShapeDtypeStruct((n, KV_DIM, WINDOW), F32),
        ],
        compiler_params=pltpu.CompilerParams(
            dimension_semantics=("arbitrary",), vmem_limit_bytes=VMEM_LIMIT),
        name="attn_sample",
    )(sinks, q, k_new, v_new, k_new_t, v_new_t, cache_kt, cache_vt, attn_g)


def _mlp_kernel(chunks, x_ref, yc_ref, ya_ref, wo_ref, g2_ref, wg_hbm, wu_hbm, wd_hbm, gf_ref,
                y_ref, hn_scr, wg_buf, wu_buf, wd_buf, sem):
    i = pl.program_id(0)
    n_tiles = pl.num_programs(0)
    n_slots = wg_buf.shape[0]
    n_chunks = len(chunks)
    ahead = n_slots - 1

    def chunk_copies(c):
        slot = c % n_slots
        off, width = chunks[c]
        cols = pl.ds(off, width)
        return (pltpu.make_async_copy(wg_hbm.at[:, cols], wg_buf.at[slot, :, :width], sem.at[0, slot]),
                pltpu.make_async_copy(wu_hbm.at[:, cols], wu_buf.at[slot, :, :width], sem.at[1, slot]),
                pltpu.make_async_copy(wd_hbm.at[cols, :], wd_buf.at[slot, :width, :], sem.at[2, slot]))

    def start(c):
        for cp in chunk_copies(c):
            cp.start()

    def wait(c):
        for cp in chunk_copies(c):
            cp.wait()

    @pl.when(i == 0)
    def _():
        for c in range(min(ahead, n_chunks)):
            start(c)

    mix = (jnp.dot(yc_ref[...], wo_ref[:CONV_DIM, :], preferred_element_type=F32)
           + jnp.dot(ya_ref[...], wo_ref[CONV_DIM:, :], preferred_element_type=F32))
    h = x_ref[...] + mix
    y_ref[...] = h
    hn_scr[...] = _rms(h, g2_ref[...]).astype(BF16)

    for c in range(n_chunks):
        nxt = c + ahead
        if nxt < n_chunks:
            start(nxt)
        else:
            @pl.when(i + 1 < n_tiles)
            def _():
                start(nxt - n_chunks)
        wait(c)
        slot = c % n_slots
        width = chunks[c][1]
        hn = hn_scr[...]
        gate = jnp.dot(hn, wg_buf[slot, :, :width], preferred_element_type=F32)
        up = jnp.dot(hn, wu_buf[slot, :, :width], preferred_element_type=F32)
        act = (gate * jax.nn.sigmoid(gate) * up).astype(BF16)
        y_ref[...] += jnp.dot(act, wd_buf[slot, :width, :], preferred_element_type=F32)

    y_ref[...] = _rms(y_ref[...], gf_ref[...])


def _mlp_chunks(d_ff, tf):
    bounds = list(range(0, d_ff, tf)) + [d_ff]
    return tuple((lo, hi - lo) for lo, hi in zip(bounds[:-1], bounds[1:]))


def _mlp_call(x, yc, ya, w_out, g2, w_gate, w_up, w_down, gf, *, tm, tf, n_slots, name):
    r = x.shape[0]
    chunks = _mlp_chunks(w_gate.shape[1], tf)
    assert r == tm or len(chunks) % n_slots == 0
    assert len(chunks) >= n_slots and all(w % LANES == 0 for _, w in chunks)
    rows = lambda i: (i, 0)
    const = lambda i: (0, 0)
    return pl.pallas_call(
        functools.partial(_mlp_kernel, chunks),
        grid=(r // tm,),
        in_specs=[
            pl.BlockSpec((tm, D_MODEL), rows),
            pl.BlockSpec((tm, CONV_DIM), rows),
            pl.BlockSpec((tm, ATTN_DIM), rows),
            pl.BlockSpec((D_MODEL, D_MODEL), const, pipeline_mode=pl.Buffered(1)),
            pl.BlockSpec((1, D_MODEL), const),
            pl.BlockSpec(memory_space=pl.ANY),
            pl.BlockSpec(memory_space=pl.ANY),
            pl.BlockSpec(memory_space=pl.ANY),
            pl.BlockSpec((1, D_MODEL), const),
        ],
        out_specs=pl.BlockSpec((tm, D_MODEL), rows),
        out_shape=jax.ShapeDtypeStruct((r, D_MODEL), F32),
        scratch_shapes=[
            pltpu.VMEM((tm, D_MODEL), BF16),
            pltpu.VMEM((n_slots, D_MODEL, tf), BF16),
            pltpu.VMEM((n_slots, D_MODEL, tf), BF16),
            pltpu.VMEM((n_slots, tf, D_MODEL), BF16),
            pltpu.SemaphoreType.DMA((3, n_slots)),
        ],
        compiler_params=pltpu.CompilerParams(
            dimension_semantics=("arbitrary",), vmem_limit_bytes=VMEM_LIMIT),
        name=name,
    )(x, yc, ya, w_out, g2, w_gate, w_up, w_down, gf)


def _rope_tables(pos):
    inv = ROPE_THETA ** (-jnp.arange(HALF_HEAD, dtype=F32) / HALF_HEAD)
    ang = pos.astype(F32)[:, None] * inv[None, :]
    cos = jnp.cos(ang)
    sin = jnp.sin(ang)
    reps = LANES // HEAD_DIM
    return (jnp.tile(jnp.concatenate([cos, cos], axis=-1), (1, reps)),
            jnp.tile(jnp.concatenate([-sin, sin], axis=-1), (1, reps)))


def kernel(x_prompt, x_sample, cache_k, cache_v, state_conv, meta_tokens, norm1_g, w_in, conv_w, conv_norm_g,
           attn_norm_g, attn_sinks, w_out, norm2_g, w_gate, w_up, w_down, final_norm_g):
    depth = w_in.shape[0]
    assert depth == 1, "single-layer step only"
    nb, seq, _ = x_prompt.shape
    ns, dec_seq, _ = x_sample.shape
    assert dec_seq == 1

    g1 = norm1_g[0][None]
    g2 = norm2_g[0][None]
    gf = final_norm_g[None]
    cg = conv_norm_g[0][None]
    ag = attn_norm_g[0][None]
    cw = conv_w[0]
    sinks = attn_sinks[0]
    xs = x_sample.reshape(ns, D_MODEL)
    cos_m, sin_m = _rope_tables(jnp.arange(N_META))
    cos_s, sin_s = _rope_tables(jnp.full((ns,), PAST_LEN))
    prev_s = state_conv[0].reshape(ns, (CONV_W - 1) * CONV_DIM)
    w_in_b, kd_meta, vd_meta, u_m, yc_s, q_s, k_st, v_st, k_s, v_s, u_s = _small_proj_call(
        meta_tokens, xs, g1, w_in[0], cw, cg, cos_m, sin_m, cos_s, sin_s, prev_s)

    cos_p, sin_p = _rope_tables(N_META + jnp.arange(seq))
    yc, q, kd, vd, k_tail, v_tail, u_tail, w_out_b, w_gate_b, w_up_b, w_down_b = _proj_call(
        x_prompt, g1, w_in_b, cw, cg, cos_p, sin_p, u_m, tm=PROJ_ROWS, n_sub=PROJ_SUB_TILES,
        cast=(w_out[0], w_gate[0], w_up[0], w_down[0]))
    ya = _attn_prompt(q, kd, vd, kd_meta, vd_meta, sinks, ag, n_sub=ATTN_BLOCKS_PER_STEP)
    rp = nb * seq
    y_prompt = _mlp_call(x_prompt.reshape(rp, D_MODEL), yc.reshape(rp, CONV_DIM), ya.reshape(rp, ATTN_DIM),
                         w_out_b, g2, w_gate_b, w_up_b, w_down_b, gf,
                         tm=MLP_ROWS, tf=MLP_CHUNK, n_slots=MLP_SLOTS, name="mlp_prompt")
    y_prompt = y_prompt.reshape(nb, seq, D_MODEL)
    new_k_prompt = k_tail.reshape(1, nb, WINDOW, N_KV_HEADS, HEAD_DIM)
    new_v_prompt = v_tail.reshape(1, nb, WINDOW, N_KV_HEADS, HEAD_DIM)
    new_conv_prompt = u_tail[None]

    cache_kt = jnp.transpose(cache_k[0], (0, 2, 3, 1)).reshape(ns, KV_DIM, WINDOW)
    cache_vt = jnp.transpose(cache_v[0], (0, 2, 3, 1)).reshape(ns, KV_DIM, WINDOW)
    ya_s, nk_s, nv_s = _attn_sample(q_s, k_s, v_s, k_st, v_st, cache_kt, cache_vt, sinks, ag,
                                    nseq=SAMPLE_SEQS_PER_STEP)
    y_sample = _mlp_call(xs, yc_s, ya_s, w_out_b, g2, w_gate_b, w_up_b, w_down_b, gf,
                         tm=ns, tf=SAMPLE_MLP_CHUNK, n_slots=SAMPLE_MLP_SLOTS, name="mlp_sample")
    y_sample = y_sample.reshape(ns, 1, D_MODEL)
    new_k_sample = jnp.transpose(nk_s.reshape(ns, N_KV_HEADS, HEAD_DIM, WINDOW), (0, 3, 1, 2))[None]
    new_v_sample = jnp.transpose(nv_s.reshape(ns, N_KV_HEADS, HEAD_DIM, WINDOW), (0, 3, 1, 2))[None]
    new_conv_sample = u_s.reshape(1, ns, CONV_W - 1, CONV_DIM)

    return (y_prompt, y_sample, new_k_prompt, new_v_prompt, new_conv_prompt,
            new_k_sample, new_v_sample, new_conv_sample)
```

```python
import functools

import jax
import jax.numpy as jnp
from jax import lax
from jax.experimental import pallas as pl
from jax.experimental.pallas import tpu as pltpu

D_MODEL = 2048
N_META = 16
CONV_DIM = 1024
CONV_W = 3
HEAD_DIM = 64
HALF_HEAD = HEAD_DIM // 2
ATTN_DIM = 1024
N_HEADS = 16
N_KV_HEADS = 4
GROUP = N_HEADS // N_KV_HEADS
KV_DIM = N_KV_HEADS * HEAD_DIM
WINDOW = 128
ROPE_THETA = 10000.0
RMS_EPS = 1e-6
NEG_INF = -1e30
PAST_LEN = 16384
LOG2_E = 1.4426950408889634

COL_BG = 0
COL_CG = CONV_DIM
COL_XC = 2 * CONV_DIM
COL_Q = 3 * CONV_DIM
COL_K = COL_Q + ATTN_DIM
COL_V = COL_K + KV_DIM
IN_COLS = COL_V + KV_DIM

LANES = 128
BF16_ROWS_PER_TILE = 16
HEADS_PER_TILE = LANES // HEAD_DIM
VMEM_LIMIT = 63 * 1024 * 1024

PROJ_ROWS = 512
PROJ_SUB_TILES = 2
ATTN_BLOCKS_PER_STEP = 8
SAMPLE_SEQS_PER_STEP = 16
MLP_ROWS = 512
MLP_CHUNK = 1024
MLP_SLOTS = 2
SAMPLE_MLP_CHUNK = 512
SAMPLE_MLP_SLOTS = 4

F32 = jnp.float32
BF16 = jnp.bfloat16


def _rms(x, g):
    return x * lax.rsqrt(jnp.mean(x * x, axis=-1, keepdims=True) + RMS_EPS) * g


def _lane_tile(t, width):
    return jnp.concatenate([t] * (width // t.shape[-1]), axis=-1)


def _rope(t, cos, sin_signed):
    w = t.shape[-1]
    lane = lax.broadcasted_iota(jnp.int32, t.shape, 1)
    first_half = (lane & (HEAD_DIM - 1)) < HALF_HEAD
    swapped = jnp.where(first_half, pltpu.roll(t, w - HALF_HEAD, axis=1), pltpu.roll(t, HALF_HEAD, axis=1))
    return t * _lane_tile(cos, w) + swapped * _lane_tile(sin_signed, w)


def _dup_heads(pair_tile, half):
    lane = lax.broadcasted_iota(jnp.int32, pair_tile.shape, 1)
    swapped = pltpu.roll(pair_tile, HEAD_DIM, axis=1)
    if half == 0:
        return jnp.where(lane < HEAD_DIM, pair_tile, swapped)
    return jnp.where(lane < HEAD_DIM, swapped, pair_tile)


def _dup_all(t):
    tiles = []
    for g in range(N_KV_HEADS):
        pair, half = divmod(g, HEADS_PER_TILE)
        tiles.append(_dup_heads(t[:, pair * LANES:(pair + 1) * LANES], half))
    return jnp.concatenate(tiles, axis=-1).astype(BF16)


def _spread_heads(q):
    lane = lax.broadcasted_iota(jnp.int32, (q.shape[0], LANES), 1)
    tiles = []
    for h in range(N_HEADS):
        tile, half = divmod(h, HEADS_PER_TILE)
        qt = q[:, tile * LANES:(tile + 1) * LANES]
        keep = (lane < HEAD_DIM) if half == 0 else (lane >= HEAD_DIM)
        tiles.append(jnp.where(keep, qt, jnp.zeros_like(qt)))
    return jnp.concatenate(tiles, axis=-1)


def _head_block(t, h):
    return t[:, h * HEAD_DIM:(h + 1) * HEAD_DIM]


def _group_minor(t):
    return jnp.concatenate([_head_block(t, g * GROUP + i) for i in range(GROUP) for g in range(N_KV_HEADS)], axis=-1)


def _group_major(t):
    return jnp.concatenate([_head_block(t, i * N_KV_HEADS + g) for g in range(N_KV_HEADS) for i in range(GROUP)],
                           axis=-1)


def _conv_mix(u, p2, p1, cw):
    row = lax.broadcasted_iota(jnp.int32, u.shape, 0)
    u1 = jnp.where(row == 0, p1, pltpu.roll(u, 1, axis=0))
    u2 = jnp.where(row == 0, p2, jnp.where(row == 1, p1, pltpu.roll(u, 2, axis=0)))
    return cw[0:1, :] * u2 + cw[1:2, :] * u1 + cw[2:3, :] * u


Q_SCALE = HEAD_DIM ** -0.5 * LOG2_E

N_PROJ_IN = 8
N_PROJ_OUT = 7


def _proj_kernel(n_sub, *refs):
    x_ref, g1_ref, w_ref, cw_ref, cg_ref, cos_ref, sin_ref, prev_ref = refs[:N_PROJ_IN]
    n_cast = (len(refs) - 1 - N_PROJ_IN - N_PROJ_OUT) // 2
    cast_in = refs[N_PROJ_IN:N_PROJ_IN + n_cast]
    yc_ref, q_ref, kd_ref, vd_ref, kt_ref, vt_ref, u_ref = refs[N_PROJ_IN + n_cast:N_PROJ_IN + n_cast + N_PROJ_OUT]
    cast_out = refs[N_PROJ_IN + n_cast + N_PROJ_OUT:-1]
    carry_ref = refs[-1]

    ts = x_ref.shape[1] // n_sub
    cw = cw_ref[...]

    @pl.when(pl.program_id(1) == 0)
    def _():
        carry_ref[...] = prev_ref[...]

    p2 = carry_ref[0:1, :]
    p1 = carry_ref[1:2, :]

    for t in range(n_sub):
        rows = slice(t * ts, (t + 1) * ts)
        hn = _rms(x_ref[0, rows, :], g1_ref[...]).astype(BF16)

        def section(lo, width):
            return jnp.dot(hn, w_ref[:, lo:lo + width], preferred_element_type=F32)

        u = section(COL_CG, CONV_DIM) * section(COL_XC, CONV_DIM)
        cy = _conv_mix(u, p2, p1, cw)
        p2 = u[ts - 2:ts - 1, :]
        p1 = u[ts - 1:, :]
        yc_ref[0, rows, :] = _rms(section(COL_BG, CONV_DIM) * cy, cg_ref[...]).astype(BF16)

        cos = cos_ref[rows, :]
        sin = sin_ref[rows, :]
        q = _rope(section(COL_Q, ATTN_DIM), cos, sin) * Q_SCALE
        k = _rope(section(COL_K, KV_DIM), cos, sin)
        v = section(COL_V, KV_DIM)
        q_ref[0, rows, :] = _spread_heads(q.astype(BF16))
        kd_ref[0, rows, :] = _dup_all(k)
        vd_ref[0, rows, :] = _dup_all(v)

        if t == 0:
            for src, dst in zip(cast_in, cast_out):
                dst[...] = src[...].astype(BF16)

    tail = jnp.concatenate([p2, p1], axis=0)
    carry_ref[...] = tail
    u_ref[0] = tail
    tail_rows = kt_ref.shape[1]
    kt_ref[0] = k[ts - tail_rows:, :]
    vt_ref[0] = v[ts - tail_rows:, :]


def _proj_call(x, g1, w_in, conv_w, conv_g, cos, sin, prev, *, tm, n_sub, cast=()):
    nb, s, _ = x.shape
    n_steps = nb * (s // tm)
    slab = lambda b, i: (b * (s // tm) + i, 0)
    cast_specs = [pl.BlockSpec((w.shape[0] // n_steps, w.shape[1]), slab) for w in cast]
    assert all(w.shape[0] % (n_steps * BF16_ROWS_PER_TILE) == 0 for w in cast)
    assert tm // n_sub >= WINDOW
    kd_cols = N_KV_HEADS * LANES
    q_cols = N_HEADS * LANES
    const = lambda b, i: (0, 0)
    rows = lambda b, i: (b, i, 0)
    per_seq = lambda b, i: (b, 0, 0)
    return pl.pallas_call(
        functools.partial(_proj_kernel, n_sub),
        grid=(nb, s // tm),
        in_specs=[
            pl.BlockSpec((1, tm, D_MODEL), rows),
            pl.BlockSpec((1, D_MODEL), const),
            pl.BlockSpec((D_MODEL, IN_COLS), const, pipeline_mode=pl.Buffered(1)),
            pl.BlockSpec((CONV_W, CONV_DIM), const),
            pl.BlockSpec((1, CONV_DIM), const),
            pl.BlockSpec((tm, LANES), lambda b, i: (i, 0)),
            pl.BlockSpec((tm, LANES), lambda b, i: (i, 0)),
            pl.BlockSpec((CONV_W - 1, CONV_DIM), const),
            *cast_specs,
        ],
        out_specs=[
            pl.BlockSpec((1, tm, CONV_DIM), rows),
            pl.BlockSpec((1, tm, q_cols), rows),
            pl.BlockSpec((1, tm, kd_cols), rows),
            pl.BlockSpec((1, tm, kd_cols), rows),
            pl.BlockSpec((1, WINDOW, KV_DIM), per_seq),
            pl.BlockSpec((1, WINDOW, KV_DIM), per_seq),
            pl.BlockSpec((1, CONV_W - 1, CONV_DIM), per_seq),
            *cast_specs,
        ],
        out_shape=[
            jax.ShapeDtypeStruct((nb, s, CONV_DIM), BF16),
            jax.ShapeDtypeStruct((nb, s, q_cols), BF16),
            jax.ShapeDtypeStruct((nb, s, kd_cols), BF16),
            jax.ShapeDtypeStruct((nb, s, kd_cols), BF16),
            jax.ShapeDtypeStruct((nb, WINDOW, KV_DIM), F32),
            jax.ShapeDtypeStruct((nb, WINDOW, KV_DIM), F32),
            jax.ShapeDtypeStruct((nb, CONV_W - 1, CONV_DIM), F32),
            *[jax.ShapeDtypeStruct(w.shape, BF16) for w in cast],
        ],
        scratch_shapes=[pltpu.VMEM((CONV_W - 1, CONV_DIM), F32)],
        compiler_params=pltpu.CompilerParams(
            dimension_semantics=("arbitrary", "arbitrary"), vmem_limit_bytes=VMEM_LIMIT),
        name="proj_prompt",
    )(x, g1, w_in, conv_w, conv_g, cos, sin, prev, *cast)


SMALL_ROWS = 256


def _small_proj_kernel(xm_ref, xs_ref, g1_ref, w_ref, cw_ref, cg_ref, cosm_ref, sinm_ref, coss_ref, sins_ref, prev_ref,
                       wb_ref, kdm_ref, vdm_ref, um_ref, yc_ref, q_ref, kst_ref, vst_ref, ks_ref, vs_ref, us_ref,
                       hn_scr, z_scr):
    j = pl.program_id(0)
    n_meta = xm_ref.shape[0]
    n_all = z_scr.shape[0]

    @pl.when(j == 0)
    def _():
        hn = jnp.concatenate([_rms(xm_ref[...], g1_ref[...]), _rms(xs_ref[...], g1_ref[...])], axis=0).astype(BF16)
        for b in range(hn_scr.shape[0]):
            hn_scr[b] = hn[:, b * SMALL_ROWS:(b + 1) * SMALL_ROWS]
        z_scr[...] = jnp.zeros_like(z_scr)

    wb = w_ref[...].astype(BF16)
    wb_ref[...] = wb
    z_scr[...] += jnp.dot(hn_scr[j], wb, preferred_element_type=F32)

    @pl.when(j == pl.num_programs(0) - 1)
    def _():
        def section(rows, lo, width):
            return z_scr[rows, lo:lo + width]

        cw = cw_ref[...]
        rows = slice(0, n_meta)
        u = section(rows, COL_CG, CONV_DIM) * section(rows, COL_XC, CONV_DIM)
        um_ref[...] = u[n_meta - (CONV_W - 1):, :]
        k = _rope(section(rows, COL_K, KV_DIM), cosm_ref[...], sinm_ref[...])
        v = section(rows, COL_V, KV_DIM)
        pad = jnp.zeros((kdm_ref.shape[0] - n_meta, kdm_ref.shape[1]), BF16)
        kdm_ref[...] = jnp.concatenate([pad, _dup_all(k)], axis=0)
        vdm_ref[...] = jnp.concatenate([pad, _dup_all(v)], axis=0)

        rows = slice(n_meta, n_all)
        u = section(rows, COL_CG, CONV_DIM) * section(rows, COL_XC, CONV_DIM)
        u2 = prev_ref[:, :CONV_DIM]
        u1 = prev_ref[:, CONV_DIM:]
        us_ref[...] = jnp.concatenate([u1, u], axis=-1)
        cy = cw[0:1, :] * u2 + cw[1:2, :] * u1 + cw[2:3, :] * u
        yc_ref[...] = _rms(section(rows, COL_BG, CONV_DIM) * cy, cg_ref[...]).astype(BF16)
        cos = coss_ref[...]
        sin = sins_ref[...]
        q = _rope(section(rows, COL_Q, ATTN_DIM), cos, sin) * Q_SCALE
        k = _rope(section(rows, COL_K, KV_DIM), cos, sin)
        v = section(rows, COL_V, KV_DIM)
        q_ref[...] = _group_minor(q).astype(BF16)
        ks_ref[...] = k
        vs_ref[...] = v
        kst_ref[...] = k.T
        vst_ref[...] = v.T


def _small_proj_call(x_meta, x_sample, g1, w_in, conv_w, conv_g, cos_m, sin_m, cos_s, sin_s, prev_s):
    n_meta = x_meta.shape[0]
    ns = x_sample.shape[0]
    n_blocks = D_MODEL // SMALL_ROWS
    kd_cols = N_KV_HEADS * LANES
    const = lambda j: (0, 0)
    full = lambda shape: pl.BlockSpec(shape, const)
    w_spec = pl.BlockSpec((SMALL_ROWS, IN_COLS), lambda j: (j, 0))
    out_shapes = [
        ((D_MODEL, IN_COLS), BF16),
        ((WINDOW, kd_cols), BF16),
        ((WINDOW, kd_cols), BF16),
        ((CONV_W - 1, CONV_DIM), F32),
        ((ns, CONV_DIM), BF16),
        ((ns, ATTN_DIM), BF16),
        ((KV_DIM, ns), F32),
        ((KV_DIM, ns), F32),
        ((ns, KV_DIM), F32),
        ((ns, KV_DIM), F32),
        ((ns, (CONV_W - 1) * CONV_DIM), F32),
    ]
    return pl.pallas_call(
        _small_proj_kernel,
        grid=(n_blocks,),
        in_specs=[
            full((n_meta, D_MODEL)), full((ns, D_MODEL)), full((1, D_MODEL)), w_spec,
            full((CONV_W, CONV_DIM)), full((1, CONV_DIM)),
            full((n_meta, LANES)), full((n_meta, LANES)), full((ns, LANES)), full((ns, LANES)),
            full((ns, (CONV_W - 1) * CONV_DIM)),
        ],
        out_specs=[w_spec] + [full(shape) for shape, _ in out_shapes[1:]],
        out_shape=[jax.ShapeDtypeStruct(shape, dtype) for shape, dtype in out_shapes],
        scratch_shapes=[pltpu.VMEM((n_blocks, n_meta + ns, SMALL_ROWS), BF16),
                        pltpu.VMEM((n_meta + ns, IN_COLS), F32)],
        compiler_params=pltpu.CompilerParams(
            dimension_semantics=("arbitrary",), vmem_limit_bytes=VMEM_LIMIT),
        name="proj_small",
    )(x_meta, x_sample, g1, w_in, conv_w, conv_g, cos_m, sin_m, cos_s, sin_s, prev_s)


def _attn_prompt_kernel(sinks_ref, q_ref, kc_ref, kp_ref, vc_ref, vp_ref, km_ref, vm_ref, ag_ref,
                        ya_ref, o_scr):
    first = pl.program_id(1) == 0
    blk = WINDOW
    nk = 2 * blk
    n_sub = q_ref.shape[1] // blk

    qi = lax.broadcasted_iota(jnp.int32, (blk, nk), 0)
    kj = lax.broadcasted_iota(jnp.int32, (blk, nk), 1)
    band = (kj >= qi) & (kj <= qi + WINDOW)
    band_first = band & (kj >= jnp.where(first, blk - N_META, 0))

    lane = lax.broadcasted_iota(jnp.int32, (blk, LANES), 1)
    ones = jnp.ones((nk, LANES), BF16)
    for t in range(n_sub):
        rows = slice(t * blk, (t + 1) * blk)
        valid = band_first if t == 0 else band
        for g in range(N_KV_HEADS):
            cols = slice(g * LANES, (g + 1) * LANES)
            if t == 0:
                k_prev = jnp.where(first, km_ref[:, cols], kp_ref[0, :, cols])
                v_prev = jnp.where(first, vm_ref[:, cols], vp_ref[0, :, cols])
            else:
                k_prev = kc_ref[0, (t - 1) * blk:t * blk, cols]
                v_prev = vc_ref[0, (t - 1) * blk:t * blk, cols]
            kd = jnp.concatenate([k_prev, kc_ref[0, rows, cols]], axis=0)
            vd = jnp.concatenate([v_prev, vc_ref[0, rows, cols]], axis=0)
            v_rhs = jnp.concatenate([vd, ones], axis=1)
            for tile in range(g * GROUP // HEADS_PER_TILE, (g + 1) * GROUP // HEADS_PER_TILE):
                ov, m = [], []
                for h in (HEADS_PER_TILE * tile, HEADS_PER_TILE * tile + 1):
                    qm = q_ref[0, rows, h * LANES:(h + 1) * LANES]
                    s = lax.dot_general(qm, kd, (((1,), (1,)), ((), ())), preferred_element_type=F32)
                    s = jnp.where(valid, s, NEG_INF)
                    m.append(jnp.max(s, axis=-1, keepdims=True))
                    p = jnp.exp2(s - m[-1]).astype(BF16)
                    ov.append(jnp.dot(p, v_rhs, preferred_element_type=F32))
                low = lane < HEAD_DIM
                sink = jnp.where(low[:1], sinks_ref[HEADS_PER_TILE * tile], sinks_ref[HEADS_PER_TILE * tile + 1])
                denom = (jnp.where(low, ov[0][:, LANES:], ov[1][:, LANES:])
                         + jnp.exp2(sink * LOG2_E - jnp.where(low, m[0], m[1])))
                o_scr[rows, tile * LANES:(tile + 1) * LANES] = (
                    jnp.where(low, ov[0][:, :LANES], ov[1][:, :LANES]) / denom)
        ya_ref[0, rows, :] = _rms(o_scr[rows, :], ag_ref[...]).astype(BF16)


def _attn_prompt(q, kd, vd, kd_meta, vd_meta, sinks, attn_g, *, n_sub):
    nb, s, kd_cols = kd.shape
    blk = WINDOW
    tq = n_sub * blk
    cur = lambda b, j: (b, j, 0)
    prv = lambda b, j: (b, jnp.maximum(j * n_sub - 1, 0), 0)
    const = lambda b, j: (0, 0)
    return pl.pallas_call(
        _attn_prompt_kernel,
        grid=(nb, s // tq),
        in_specs=[
            pl.BlockSpec(memory_space=pltpu.SMEM),
            pl.BlockSpec((1, tq, N_HEADS * LANES), cur),
            pl.BlockSpec((1, tq, kd_cols), cur),
            pl.BlockSpec((1, blk, kd_cols), prv),
            pl.BlockSpec((1, tq, kd_cols), cur),
            pl.BlockSpec((1, blk, kd_cols), prv),
            pl.BlockSpec((blk, kd_cols), const),
            pl.BlockSpec((blk, kd_cols), const),
            pl.BlockSpec((1, ATTN_DIM), const),
        ],
        out_specs=pl.BlockSpec((1, tq, ATTN_DIM), cur),
        out_shape=jax.ShapeDtypeStruct((nb, s, ATTN_DIM), BF16),
        scratch_shapes=[pltpu.VMEM((tq, ATTN_DIM), F32)],
        compiler_params=pltpu.CompilerParams(
            dimension_semantics=("arbitrary", "arbitrary"), vmem_limit_bytes=VMEM_LIMIT),
        name="attn_prompt",
    )(sinks, q, kd, kd, vd, vd, kd_meta, vd_meta, attn_g)


def _attn_sample_kernel(sinks_ref, q_ref, kn_ref, vn_ref, knt_ref, vnt_ref, ck_ref, cv_ref, ag_ref,
                        ya_ref, nk_ref, nv_ref):
    nseq = q_ref.shape[0]
    base = pl.program_id(0) * nseq
    n_rows = N_HEADS * nseq
    qf = q_ref[...].astype(F32)
    kv_of_lane = lax.broadcasted_iota(jnp.int32, (nseq, KV_DIM), 1) // HEAD_DIM
    q_rows, sink_rows = [], []
    for r in range(N_HEADS):
        i, g = divmod(r, N_KV_HEADS)
        q_rows.append(jnp.where(kv_of_lane == g, qf[:, i * KV_DIM:(i + 1) * KV_DIM], 0.0))
        sink_rows.append(jnp.full((nseq, 1), sinks_ref[g * GROUP + i] * LOG2_E, F32))
    q_all = jnp.concatenate(q_rows, axis=0)
    sink = jnp.concatenate(sink_rows, axis=0)
    q_bf = q_all.astype(BF16)

    key_lane = lax.broadcasted_iota(jnp.int32, (KV_DIM, WINDOW), 1)

    def shifted(cache, new_t, n):
        col = jnp.sum(jnp.where(key_lane == base + n, new_t, 0.0), axis=1, keepdims=True)
        return jnp.where(key_lane == WINDOW - 1, col, pltpu.roll(cache, WINDOW - 1, axis=1))

    seq_of_row = lax.broadcasted_iota(jnp.int32, (n_rows, WINDOW), 0) % nseq
    s_c = jnp.zeros((n_rows, WINDOW), F32)
    for n in range(nseq):
        kt = ck_ref[n]
        s_n = jnp.dot(q_bf, kt.astype(BF16), preferred_element_type=F32)
        s_c = jnp.where(seq_of_row == n, s_n, s_c)
        nk_ref[n] = shifted(kt, knt_ref[...], n)
    kn = jnp.concatenate([kn_ref[...]] * N_HEADS, axis=0)
    vn = jnp.concatenate([vn_ref[...]] * N_HEADS, axis=0)
    s_new = jnp.sum(q_all * kn, axis=-1, keepdims=True)
    m = jnp.maximum(jnp.maximum(jnp.max(s_c, axis=-1, keepdims=True), s_new), sink)
    p_c = jnp.exp2(s_c - m)
    p_new = jnp.exp2(s_new - m)
    denom = jnp.sum(p_c, axis=-1, keepdims=True) + p_new + jnp.exp2(sink - m)
    p_bf = p_c.astype(BF16)

    seq_of_row = lax.broadcasted_iota(jnp.int32, (n_rows, KV_DIM), 0) % nseq
    o = jnp.zeros((n_rows, KV_DIM), F32)
    for n in range(nseq):
        vt = cv_ref[n]
        o_n = lax.dot_general(p_bf, vt.astype(BF16), (((1,), (1,)), ((), ())), preferred_element_type=F32)
        o = jnp.where(seq_of_row == n, o_n, o)
        nv_ref[n] = shifted(vt, vnt_ref[...], n)
    kv_of_row = (lax.broadcasted_iota(jnp.int32, (n_rows, KV_DIM), 0) // nseq) % N_KV_HEADS
    own = (lax.broadcasted_iota(jnp.int32, (n_rows, KV_DIM), 1) // HEAD_DIM) == kv_of_row
    o = jnp.where(own, (o + p_new * vn) / denom, 0.0)
    slabs = []
    for i in range(GROUP):
        lo = i * N_KV_HEADS * nseq
        slabs.append(sum(o[lo + g * nseq:lo + (g + 1) * nseq] for g in range(N_KV_HEADS)))
    ya_ref[...] = _rms(_group_major(jnp.concatenate(slabs, axis=-1)), ag_ref[...]).astype(BF16)


def _attn_sample(q, k_new, v_new, k_new_t, v_new_t, cache_kt, cache_vt, sinks, attn_g, *, nseq):
    n = q.shape[0]
    rows = lambda i: (i, 0)
    seqs = lambda i: (i, 0, 0)
    const = lambda i: (0, 0)
    return pl.pallas_call(
        _attn_sample_kernel,
        grid=(n // nseq,),
        in_specs=[
            pl.BlockSpec(memory_space=pltpu.SMEM),
            pl.BlockSpec((nseq, ATTN_DIM), rows),
            pl.BlockSpec((nseq, KV_DIM), rows),
            pl.BlockSpec((nseq, KV_DIM), rows),
            pl.BlockSpec((KV_DIM, n), const),
            pl.BlockSpec((KV_DIM, n), const),
            pl.BlockSpec((nseq, KV_DIM, WINDOW), seqs),
            pl.BlockSpec((nseq, KV_DIM, WINDOW), seqs),
            pl.BlockSpec((1, ATTN_DIM), const),
        ],
        out_specs=[
            pl.BlockSpec((nseq, ATTN_DIM), rows),
            pl.BlockSpec((nseq, KV_DIM, WINDOW), seqs),
            pl.BlockSpec((nseq, KV_DIM, WINDOW), seqs),
        ],
        out_shape=[
            jax.ShapeDtypeStruct((n, ATTN_DIM), BF16),
            jax.ShapeDtypeStruct((n, KV_DIM, WINDOW), F32),
            jax.ShapeDtypeStruct((n, KV_DIM, WINDOW), F32),
        ],
        compiler_params=pltpu.CompilerParams(
            dimension_semantics=("arbitrary",), vmem_limit_bytes=VMEM_LIMIT),
        name="attn_sample",
    )(sinks, q, k_new, v_new, k_new_t, v_new_t, cache_kt, cache_vt, attn_g)


def _mlp_kernel(chunks, x_ref, yc_ref, ya_ref, wo_ref, g2_ref, wg_hbm, wu_hbm, wd_hbm, gf_ref,
                y_ref, hn_scr, wg_buf, wu_buf, wd_buf, sem):
    i = pl.program_id(0)
    n_tiles = pl.num_programs(0)
    n_slots = wg_buf.shape[0]
    n_chunks = len(chunks)
    ahead = n_slots - 1

    def chunk_copies(c):
        slot = c % n_slots
        off, width = chunks[c]
        cols = pl.ds(off, width)
        return (pltpu.make_async_copy(wg_hbm.at[:, cols], wg_buf.at[slot, :, :width], sem.at[0, slot]),
                pltpu.make_async_copy(wu_hbm.at[:, cols], wu_buf.at[slot, :, :width], sem.at[1, slot]),
                pltpu.make_async_copy(wd_hbm.at[cols, :], wd_buf.at[slot, :width, :], sem.at[2, slot]))

    def start(c):
        for cp in chunk_copies(c):
            cp.start()

    def wait(c):
        for cp in chunk_copies(c):
            cp.wait()

    @pl.when(i == 0)
    def _():
        for c in range(min(ahead, n_chunks)):
            start(c)

    mix = (jnp.dot(yc_ref[...], wo_ref[:CONV_DIM, :], preferred_element_type=F32)
           + jnp.dot(ya_ref[...], wo_ref[CONV_DIM:, :], preferred_element_type=F32))
    h = x_ref[...] + mix
    y_ref[...] = h
    hn_scr[...] = _rms(h, g2_ref[...]).astype(BF16)

    for c in range(n_chunks):
        nxt = c + ahead
        if nxt >= n_chunks:
            @pl.when(i + 1 < n_tiles)
            def _():
                start(nxt - n_chunks)
        wait(c)
        slot = c % n_slots
        width = chunks[c][1]
        hn = hn_scr[...]
        gate = jnp.dot(hn, wg_buf[slot, :, :width], preferred_element_type=F32)
        if nxt < n_chunks:
            start(nxt)
        up = jnp.dot(hn, wu_buf[slot, :, :width], preferred_element_type=F32)
        act = (gate * jax.nn.sigmoid(gate) * up).astype(BF16)
        y_ref[...] += jnp.dot(act, wd_buf[slot, :width, :], preferred_element_type=F32)

    y_ref[...] = _rms(y_ref[...], gf_ref[...])


def _mlp_chunks(d_ff, tf):
    bounds = list(range(0, d_ff, tf)) + [d_ff]
    return tuple((lo, hi - lo) for lo, hi in zip(bounds[:-1], bounds[1:]))


def _mlp_call(x, yc, ya, w_out, g2, w_gate, w_up, w_down, gf, *, tm, tf, n_slots, name):
    r = x.shape[0]
    chunks = _mlp_chunks(w_gate.shape[1], tf)
    assert r == tm or len(chunks) % n_slots == 0
    assert len(chunks) >= n_slots and all(w % LANES == 0 for _, w in chunks)
    rows = lambda i: (i, 0)
    const = lambda i: (0, 0)
    return pl.pallas_call(
        functools.partial(_mlp_kernel, chunks),
        grid=(r // tm,),
        in_specs=[
            pl.BlockSpec((tm, D_MODEL), rows),
            pl.BlockSpec((tm, CONV_DIM), rows),
            pl.BlockSpec((tm, ATTN_DIM), rows),
            pl.BlockSpec((D_MODEL, D_MODEL), const, pipeline_mode=pl.Buffered(1)),
            pl.BlockSpec((1, D_MODEL), const),
            pl.BlockSpec(memory_space=pl.ANY),
            pl.BlockSpec(memory_space=pl.ANY),
            pl.BlockSpec(memory_space=pl.ANY),
            pl.BlockSpec((1, D_MODEL), const),
        ],
        out_specs=pl.BlockSpec((tm, D_MODEL), rows),
        out_shape=jax.ShapeDtypeStruct((r, D_MODEL), F32),
        scratch_shapes=[
            pltpu.VMEM((tm, D_MODEL), BF16),
            pltpu.VMEM((n_slots, D_MODEL, tf), BF16),
            pltpu.VMEM((n_slots, D_MODEL, tf), BF16),
            pltpu.VMEM((n_slots, tf, D_MODEL), BF16),
            pltpu.SemaphoreType.DMA((3, n_slots)),
        ],
        compiler_params=pltpu.CompilerParams(
            dimension_semantics=("arbitrary",), vmem_limit_bytes=VMEM_LIMIT),
        name=name,
    )(x, yc, ya, w_out, g2, w_gate, w_up, w_down, gf)


def _rope_tables(pos):
    inv = ROPE_THETA ** (-jnp.arange(HALF_HEAD, dtype=F32) / HALF_HEAD)
    ang = pos.astype(F32)[:, None] * inv[None, :]
    cos = jnp.cos(ang)
    sin = jnp.sin(ang)
    reps = LANES // HEAD_DIM
    return (jnp.tile(jnp.concatenate([cos, cos], axis=-1), (1, reps)),
            jnp.tile(jnp.concatenate([-sin, sin], axis=-1), (1, reps)))


def kernel(x_prompt, x_sample, cache_k, cache_v, state_conv, meta_tokens, norm1_g, w_in, conv_w, conv_norm_g,
           attn_norm_g, attn_sinks, w_out, norm2_g, w_gate, w_up, w_down, final_norm_g):
    depth = w_in.shape[0]
    assert depth == 1, "single-layer step only"
    nb, seq, _ = x_prompt.shape
    ns, dec_seq, _ = x_sample.shape
    assert dec_seq == 1

    g1 = norm1_g[0][None]
    g2 = norm2_g[0][None]
    gf = final_norm_g[None]
    cg = conv_norm_g[0][None]
    ag = attn_norm_g[0][None]
    cw = conv_w[0]
    sinks = attn_sinks[0]
    xs = x_sample.reshape(ns, D_MODEL)
    cos_m, sin_m = _rope_tables(jnp.arange(N_META))
    cos_s, sin_s = _rope_tables(jnp.full((ns,), PAST_LEN))
    prev_s = state_conv[0].reshape(ns, (CONV_W - 1) * CONV_DIM)
    w_in_b, kd_meta, vd_meta, u_m, yc_s, q_s, k_st, v_st, k_s, v_s, u_s = _small_proj_call(
        meta_tokens, xs, g1, w_in[0], cw, cg, cos_m, sin_m, cos_s, sin_s, prev_s)

    cos_p, sin_p = _rope_tables(N_META + jnp.arange(seq))
    yc, q, kd, vd, k_tail, v_tail, u_tail, w_out_b, w_gate_b, w_up_b, w_down_b = _proj_call(
        x_prompt, g1, w_in_b, cw, cg, cos_p, sin_p, u_m, tm=PROJ_ROWS, n_sub=PROJ_SUB_TILES,
        cast=(w_out[0], w_gate[0], w_up[0], w_down[0]))
    ya = _attn_prompt(q, kd, vd, kd_meta, vd_meta, sinks, ag, n_sub=ATTN_BLOCKS_PER_STEP)
    rp = nb * seq
    y_prompt = _mlp_call(x_prompt.reshape(rp, D_MODEL), yc.reshape(rp, CONV_DIM), ya.reshape(rp, ATTN_DIM),
                         w_out_b, g2, w_gate_b, w_up_b, w_down_b, gf,
                         tm=MLP_ROWS, tf=MLP_CHUNK, n_slots=MLP_SLOTS, name="mlp_prompt")
    y_prompt = y_prompt.reshape(nb, seq, D_MODEL)
    new_k_prompt = k_tail.reshape(1, nb, WINDOW, N_KV_HEADS, HEAD_DIM)
    new_v_prompt = v_tail.reshape(1, nb, WINDOW, N_KV_HEADS, HEAD_DIM)
    new_conv_prompt = u_tail[None]

    cache_kt = jnp.transpose(cache_k[0], (0, 2, 3, 1)).reshape(ns, KV_DIM, WINDOW)
    cache_vt = jnp.transpose(cache_v[0], (0, 2, 3, 1)).reshape(ns, KV_DIM, WINDOW)
    ya_s, nk_s, nv_s = _attn_sample(q_s, k_s, v_s, k_st, v_st, cache_kt, cache_vt, sinks, ag,
                                    nseq=SAMPLE_SEQS_PER_STEP)
    y_sample = _mlp_call(xs, yc_s, ya_s, w_out_b, g2, w_gate_b, w_up_b, w_down_b, gf,
                         tm=ns, tf=SAMPLE_MLP_CHUNK, n_slots=SAMPLE_MLP_SLOTS, name="mlp_sample")
    y_sample = y_sample.reshape(ns, 1, D_MODEL)
    new_k_sample = jnp.transpose(nk_s.reshape(ns, N_KV_HEADS, HEAD_DIM, WINDOW), (0, 3, 1, 2))[None]
    new_v_sample = jnp.transpose(nv_s.reshape(ns, N_KV_HEADS, HEAD_DIM, WINDOW), (0, 3, 1, 2))[None]
    new_conv_sample = u_s.reshape(1, ns, CONV_W - 1, CONV_DIM)

    return (y_prompt, y_sample, new_k_prompt, new_v_prompt, new_conv_prompt,
            new_k_sample, new_v_sample, new_conv_sample)
```
